```python
import math
import jax, jax.numpy as jnp
from jax import lax
import numpy as np


D_MODEL = 1024
BATCH = 4
SEQ = 4096
DEPTH = 2
DEC_BATCH = 8
DEC_SEQ = 2048
PAST_LEN = 128

GRID_W = 64
BLOCK = 128
EPS = 1e-6
ROPE_THETA = 10000.0

A_HEADS = 8
A_KV_HEADS = 2
A_HEAD_DIM = 64
A_GROUP = A_HEADS // A_KV_HEADS
A_WIDTH = A_HEADS * A_HEAD_DIM
A_KV_WIDTH = A_KV_HEADS * A_HEAD_DIM

B_HEADS = 4
B_KEY_DIM = 64
B_VAL_DIM = 128
RET_CHUNK = 128
B_QK_WIDTH = B_HEADS * B_KEY_DIM
B_V_WIDTH = B_HEADS * B_VAL_DIM

C_HEADS = 16
C_KV_HEADS = 2
C_HEAD_DIM = 64
C_GROUP = C_HEADS // C_KV_HEADS
C_WIDTH = C_HEADS * C_HEAD_DIM
C_KV_WIDTH = C_KV_HEADS * C_HEAD_DIM
WINDOW = 128
REL_BUCKETS = 32
REL_MAX_DIST = 128

AB_SPLITS = [A_WIDTH, A_KV_WIDTH, A_KV_WIDTH, A_WIDTH, B_QK_WIDTH, B_QK_WIDTH, B_V_WIDTH, B_V_WIDTH]
AB_IN = sum(AB_SPLITS)
AB_OUT = A_WIDTH + B_V_WIDTH
C_IN = 2 * C_WIDTH + 2 * C_KV_WIDTH
C_OUT = C_WIDTH
N_EVEN = (DEPTH + 1) // 2
N_ODD = DEPTH // 2

kernel_name = 'hybrid_axial_retention_window_encoder'

F32 = jnp.float32


def rmsnorm(x, g):
    xf = x.astype(F32)
    y = xf * lax.rsqrt(jnp.mean(xf * xf, axis=-1, keepdims=True) + EPS)
    if g is not None:
        y = y * g.astype(F32)
    return y.astype(x.dtype)


def rotate_pairs(x, ang):
    d = x.shape[-1]
    xf = x.astype(F32).reshape(*x.shape[:-1], d // 2, 2)
    cos = jnp.cos(ang)[:, None, :]
    sin = jnp.sin(ang)[:, None, :]
    x0, x1 = xf[..., 0], xf[..., 1]
    out = jnp.stack([x0 * cos - x1 * sin, x0 * sin + x1 * cos], axis=-1)
    return out.reshape(x.shape).astype(x.dtype)


def axial_angles(n):
    rows = n // GRID_W
    row = jnp.repeat(jnp.arange(rows, dtype=F32), GRID_W)
    col = jnp.tile(jnp.arange(GRID_W, dtype=F32), rows)
    quarter = A_HEAD_DIM // 4
    freqs = ROPE_THETA ** (-jnp.arange(quarter, dtype=F32) / quarter)
    return jnp.concatenate([row[:, None] * freqs, col[:, None] * freqs], axis=-1)


def linear_angles(n, d):
    half = d // 2
    freqs = ROPE_THETA ** (-jnp.arange(half, dtype=F32) / half)
    return jnp.arange(n, dtype=F32)[:, None] * freqs


def dense_gqa(q, k, v):
    b, n = q.shape[:2]
    nb = n // BLOCK
    qb = q.reshape(b, nb, BLOCK, *q.shape[2:]).swapaxes(0, 1)

    def attend(qblk):
        s = jnp.einsum('bqkgd,bskd->bkgqs', qblk, k).astype(F32)
        p = jax.nn.softmax(s, axis=-1).astype(v.dtype)
        return jnp.einsum('bkgqs,bskd->bqkgd', p, v)

    o = lax.map(attend, qb)
    return o.swapaxes(0, 1).reshape(b, n, -1)


def t5_bucket(rel):
    half = REL_BUCKETS // 2
    max_exact = half // 2
    ret = (rel > 0).astype(np.int32) * half
    dist = np.abs(rel)
    large = max_exact + (np.log(np.maximum(dist, 1) / max_exact) / np.log(REL_MAX_DIST / max_exact)
                         * (half - max_exact)).astype(np.int32)
    large = np.minimum(large, half - 1)
    return ret + np.where(dist < max_exact, dist, large)


def banded_sink_gqa(q, k, v, sink, rel_bias):
    b, n, kvh, grp, hd = q.shape
    nb = n // BLOCK
    pad = ((0, 0), (BLOCK, BLOCK), (0, 0), (0, 0))
    kp = jnp.pad(k, pad).reshape(b, nb + 2, BLOCK, kvh, hd)
    vp = jnp.pad(v, pad).reshape(b, nb + 2, BLOCK, kvh, hd)
    kwin = jnp.concatenate([kp[:, j:j + nb] for j in range(3)], axis=2)
    vwin = jnp.concatenate([vp[:, j:j + nb] for j in range(3)], axis=2)
    rel = np.arange(3 * BLOCK)[None, :] - BLOCK - np.arange(BLOCK)[:, None]
    in_win = jnp.asarray(np.abs(rel) <= WINDOW)
    bias = rel_bias[jnp.asarray(t5_bucket(rel))].astype(F32)
    bias = bias.transpose(2, 0, 1).reshape(kvh, grp, BLOCK, 3 * BLOCK)
    kpos = (jnp.arange(nb)[:, None] - 1) * BLOCK + jnp.arange(3 * BLOCK)[None, :]
    kvalid = (kpos >= 0) & (kpos < n)
    mask = in_win[None] & kvalid[:, None, :]
    sink_l = sink.reshape(kvh, grp)[..., None, None].astype(F32)
    qb = q.reshape(b, nb, BLOCK, kvh, grp, hd).swapaxes(0, 1)

    def attend(args):
        qblk, kblk, vblk, m = args
        s = jnp.einsum('bqkgd,bskd->bkgqs', qblk, kblk).astype(F32) + bias
        s = jnp.where(m, s, -jnp.inf)
        mx = jnp.maximum(jnp.max(s, axis=-1, keepdims=True), sink_l)
        e = jnp.exp(s - mx)
        denom = jnp.sum(e, axis=-1, keepdims=True) + jnp.exp(sink_l - mx)
        p = (e / denom).astype(vblk.dtype)
        return jnp.einsum('bkgqs,bskd->bqkgd', p, vblk)

    o = lax.map(attend, (qb, kwin.swapaxes(0, 1), vwin.swapaxes(0, 1), mask))
    return o.swapaxes(0, 1).reshape(b, n, -1)


def retention(q, k, v, log_gamma, inclusive):
    b, n, h, dk = q.shape
    dv = v.shape[-1]
    c = RET_CHUNK
    nc = n // c
    qc = q.reshape(b, nc, c, h, dk)
    kc = k.reshape(b, nc, c, h, dk)
    vc = v.reshape(b, nc, c, h, dv)
    idx = jnp.arange(c, dtype=F32)
    diff = idx[:, None] - idx[None, :]
    mask = (diff >= 0) if inclusive else (diff > 0)
    intra_decay = jnp.where(mask, jnp.exp(log_gamma[:, None, None] * jnp.where(mask, diff, 0.0)), 0.0)
    s = jnp.einsum('bnihd,bnjhd->bnhij', qc, kc) * intra_decay
    intra = jnp.einsum('bnhij,bnjhe->bnihe', s, vc)
    q_decay = jnp.exp((idx[:, None] + 1.0) * log_gamma[None, :])
    k_decay = jnp.exp((c - 1.0 - idx)[:, None] * log_gamma[None, :])
    chunk_decay = jnp.exp(c * log_gamma)
    kv = jnp.einsum('bnjhd,bnjhe->nbhde', kc * k_decay[:, :, None], vc)

    def step(state, kv_n):
        return state * chunk_decay[:, None, None] + kv_n, state

    _, states = lax.scan(step, jnp.zeros((b, h, dk, dv), F32), kv)
    cross = jnp.einsum('bnihd,nbhde->bnihe', qc * q_decay[:, :, None], states)
    return (intra + cross).reshape(b, n, h, dv)


def attn_retention_layer(x, g, w_in, qk_gain, ret_decay, w_out):
    b, n, _ = x.shape
    z = rmsnorm(x, g) @ w_in
    cuts = [int(i) for i in np.cumsum(AB_SPLITS)[:-1]]
    qa, ka, va, ga, qb, kb, vb, gb = jnp.split(z, cuts, axis=-1)
    ang = axial_angles(n)
    qa = rotate_pairs(rmsnorm(qa.reshape(b, n, A_HEADS, A_HEAD_DIM), qk_gain[0]), ang) * (A_HEAD_DIM ** -0.5)
    ka = rotate_pairs(rmsnorm(ka.reshape(b, n, A_KV_HEADS, A_HEAD_DIM), qk_gain[1]), ang)
    va = va.reshape(b, n, A_KV_HEADS, A_HEAD_DIM)
    oa = dense_gqa(qa.reshape(b, n, A_KV_HEADS, A_GROUP, A_HEAD_DIM), ka, va)
    ang_b = linear_angles(n, B_KEY_DIM)
    qb = rotate_pairs(qb.reshape(b, n, B_HEADS, B_KEY_DIM), ang_b).astype(F32)
    kb = rotate_pairs(kb.reshape(b, n, B_HEADS, B_KEY_DIM), ang_b).astype(F32) * (B_KEY_DIM ** -0.5)
    vb = vb.reshape(b, n, B_HEADS, B_VAL_DIM).astype(F32)
    log_gamma = -jnp.exp(ret_decay.astype(F32))
    fwd = retention(qb, kb, vb, log_gamma[0], True)
    bwd = jnp.flip(retention(jnp.flip(qb, 1), jnp.flip(kb, 1), jnp.flip(vb, 1), log_gamma[1], False), 1)
    ob = rmsnorm(fwd + bwd, None).reshape(b, n, B_V_WIDTH).astype(x.dtype)
    mixed = jnp.concatenate([jax.nn.silu(ga) * oa, jax.nn.silu(gb) * ob], axis=-1)
    return x + mixed @ w_out


def windowed_layer(x, g, w_in, sink, rel_bias, w_out):
    b, n, _ = x.shape
    z = rmsnorm(x, g) @ w_in
    q, k, v, gate = jnp.split(z, [C_WIDTH, C_WIDTH + C_KV_WIDTH, C_WIDTH + 2 * C_KV_WIDTH], axis=-1)
    q = q.reshape(b, n, C_KV_HEADS, C_GROUP, C_HEAD_DIM) * (C_HEAD_DIM ** -0.5)
    k = k.reshape(b, n, C_KV_HEADS, C_HEAD_DIM)
    v = v.reshape(b, n, C_KV_HEADS, C_HEAD_DIM)
    o = banded_sink_gqa(q, k, v, sink, rel_bias)
    return x + (jax.nn.silu(gate) * o) @ w_out


def trunk(x, norm_g, w_in_ab, qk_norm_a, ret_decay, w_out_ab, w_in_c, sink_c, w_out_c, rel_bias, final_norm):
    for layer in range(DEPTH):
        i = layer // 2
        if layer % 2 == 0:
            x = attn_retention_layer(x, norm_g[layer], w_in_ab[i], qk_norm_a[i], ret_decay[i], w_out_ab[i])
        else:
            x = windowed_layer(x, norm_g[layer], w_in_c[i], sink_c[i], rel_bias, w_out_c[i])
    return rmsnorm(x, final_norm)


def setup_inputs(seed: int = 0) -> dict:
    key = jax.random.key(seed)
    ks = jax.random.split(key, 12)
    nrm = jax.random.normal
    base_decay = np.log(-np.log(1.0 - 2.0 ** (-5.0 - np.arange(B_HEADS)))).astype(np.float32)
    return {
        'x_prompt': nrm(ks[0], (BATCH, SEQ, D_MODEL), F32),
        'x_sample': nrm(ks[1], (DEC_BATCH, DEC_SEQ, D_MODEL), F32),
        'norm_g': 1.0 + 0.02 * nrm(ks[2], (DEPTH, D_MODEL), F32),
        'w_in_ab': nrm(ks[3], (N_EVEN, D_MODEL, AB_IN), F32) * (D_MODEL ** -0.5),
        'qk_norm_a': 1.0 + 0.02 * nrm(ks[4], (N_EVEN, 2, A_HEAD_DIM), F32),
        'ret_decay': jnp.asarray(base_decay) + 0.05 * nrm(ks[5], (N_EVEN, 2, B_HEADS), F32),
        'w_out_ab': nrm(ks[6], (N_EVEN, AB_OUT, D_MODEL), F32) * (AB_OUT ** -0.5),
        'w_in_c': nrm(ks[7], (N_ODD, D_MODEL, C_IN), F32) * (D_MODEL ** -0.5),
        'sink_c': 0.5 * nrm(ks[8], (N_ODD, C_HEADS), F32),
        'w_out_c': nrm(ks[9], (N_ODD, C_OUT, D_MODEL), F32) * (C_OUT ** -0.5),
        'rel_bias': 0.5 * nrm(ks[10], (REL_BUCKETS, C_HEADS), F32),
        'final_norm': 1.0 + 0.02 * nrm(ks[11], (D_MODEL,), F32),
    }


def reference(x_prompt, x_sample, norm_g, w_in_ab, qk_norm_a, ret_decay, w_out_ab, w_in_c, sink_c, w_out_c, rel_bias, final_norm):
    y_prompt = trunk(x_prompt, norm_g, w_in_ab, qk_norm_a, ret_decay, w_out_ab, w_in_c, sink_c, w_out_c, rel_bias, final_norm)
    y_sample = trunk(x_sample, norm_g, w_in_ab, qk_norm_a, ret_decay, w_out_ab, w_in_c, sink_c, w_out_c, rel_bias, final_norm)
    return (y_prompt, y_sample)
```

```python
import functools

import jax
import jax.numpy as jnp
import numpy as np
from jax import lax
from jax.experimental import pallas as pl
from jax.experimental.pallas import tpu as pltpu

F32 = jnp.float32
BF16 = jnp.bfloat16

D_MODEL = 1024
GRID_W = 64
EPS = 1e-6
ROPE_THETA = 10000.0
HEAD_DIM = 64
HALF = HEAD_DIM // 2
LANES = 128

A_HEADS, A_KV = 8, 2
A_GROUP = A_HEADS // A_KV
A_W, A_KVW = A_HEADS * HEAD_DIM, A_KV * HEAD_DIM
B_HEADS, B_VAL = 4, 128
B_QKW, B_VW = B_HEADS * HEAD_DIM, B_HEADS * B_VAL
C_HEADS, C_KV = 16, 2
C_GROUP = C_HEADS // C_KV
C_W, C_KVW = C_HEADS * HEAD_DIM, C_KV * HEAD_DIM
BLOCK = 128
WINDOW = 128
REL_BUCKETS, REL_MAX_DIST = 32, 128
CHUNK = 128
NEG = -1e30

AB_QA, AB_KA, AB_VA, AB_GA = 0, A_W, A_W + A_KVW, A_W + 2 * A_KVW
AB_QB = AB_GA + A_W
AB_KB, AB_VB, AB_GB = AB_QB + B_QKW, AB_QB + 2 * B_QKW, AB_QB + 2 * B_QKW + B_VW
AB_IN = AB_GB + B_VW
C_Q, C_K, C_V, C_G = 0, C_W, C_W + C_KVW, C_W + 2 * C_KVW
C_IN = C_G + C_W

ROW_TILE = 512
ATT_TQ = 128
ATT_TK = 512
VMEM_LIMIT = 48 * 1024 * 1024


def _params(sem):
    return pltpu.CompilerParams(dimension_semantics=sem, vmem_limit_bytes=VMEM_LIMIT)


def _deinterleave_perm(width):
    heads = width // HEAD_DIM
    within = np.concatenate([np.arange(0, HEAD_DIM, 2), np.arange(1, HEAD_DIM, 2)])
    return (np.arange(heads)[:, None] * HEAD_DIM + within[None, :]).reshape(-1)


def _rope_tables(ang):
    c, s = np.cos(ang), np.sin(ang)
    cos_t = np.concatenate([c, c, c, c], axis=-1)
    sin_t = np.concatenate([-s, s, -s, s], axis=-1)
    return jnp.asarray(cos_t, F32), jnp.asarray(sin_t, F32)


def _axial_angles(n):
    t = np.arange(n)
    quarter = HEAD_DIM // 4
    freqs = ROPE_THETA ** (-np.arange(quarter, dtype=np.float64) / quarter)
    row = (t // GRID_W).astype(np.float64)
    col = (t % GRID_W).astype(np.float64)
    return np.concatenate([row[:, None] * freqs, col[:, None] * freqs], axis=-1)


def _linear_angles(n):
    freqs = ROPE_THETA ** (-np.arange(HALF, dtype=np.float64) / HALF)
    return np.arange(n, dtype=np.float64)[:, None] * freqs


def _t5_bucket(rel):
    half = REL_BUCKETS // 2
    max_exact = half // 2
    ret = (rel > 0).astype(np.int32) * half
    dist = np.abs(rel)
    large = max_exact + (np.log(np.maximum(dist, 1) / max_exact) / np.log(REL_MAX_DIST / max_exact)
                         * (half - max_exact)).astype(np.int32)
    large = np.minimum(large, half - 1)
    return ret + np.where(dist < max_exact, dist, large)


def _rmsnorm_rows(x, gain):
    return x * lax.rsqrt(jnp.mean(x * x, axis=-1, keepdims=True) + EPS) * gain


def _silu(z):
    return z * jax.nn.sigmoid(z)


def _rotate_halves(x, cos_t, sin_t):
    lane = lax.broadcasted_iota(jnp.int32, (1, LANES), 1)
    first_half = (lane % HEAD_DIM) < HALF
    partner = jnp.where(first_half, pltpu.roll(x, LANES - HALF, 1), pltpu.roll(x, HALF, 1))
    return x * cos_t + partner * sin_t


def _split_heads_store(ref, tile):
    ref[0] = tile[:, :HEAD_DIM].astype(ref.dtype)
    ref[1] = tile[:, HEAD_DIM:].astype(ref.dtype)


def _inproj_ab_kernel(x_ref, g_ref, w_ref, gain_ref, ones_ref, cosa_ref, sina_ref, cosb_ref, sinb_ref,
                      qa_ref, ka_ref, va_ref, sga_ref, qb_ref, kb_ref, vb_ref, sgb_ref):
    xn = _rmsnorm_rows(x_ref[...], g_ref[...]).astype(BF16)

    def proj(lo, hi):
        return jnp.dot(xn, w_ref[:, lo:hi], preferred_element_type=F32)

    cosa, sina = cosa_ref[...], sina_ref[...]
    cosb, sinb = cosb_ref[...], sinb_ref[...]

    def head_norm_rope(tile, gain):
        ss = jnp.dot((tile * tile).astype(BF16), ones_ref[...], preferred_element_type=F32)
        tile = tile * lax.rsqrt(ss * (1.0 / HEAD_DIM) + EPS) * gain
        return _rotate_halves(tile, cosa, sina)

    zq = proj(AB_QA, AB_KA)
    for c in range(A_W // LANES):
        tile = head_norm_rope(zq[:, c * LANES:(c + 1) * LANES], gain_ref[0:1, :])
        qa_ref[:, c * LANES:(c + 1) * LANES] = tile.astype(BF16)
    zkv = proj(AB_KA, AB_GA)
    _split_heads_store(ka_ref, head_norm_rope(zkv[:, :LANES], gain_ref[1:2, :]))
    _split_heads_store(va_ref, zkv[:, LANES:])
    sga_ref[...] = _silu(proj(AB_GA, AB_QB)).astype(BF16)

    zqk = proj(AB_QB, AB_VB)
    for c in range(B_QKW // LANES):
        qb_ref[:, c * LANES:(c + 1) * LANES] = _rotate_halves(
            zqk[:, c * LANES:(c + 1) * LANES], cosb, sinb).astype(BF16)
        kt = zqk[:, B_QKW + c * LANES:B_QKW + (c + 1) * LANES] * (HEAD_DIM ** -0.5)
        kb_ref[:, c * LANES:(c + 1) * LANES] = _rotate_halves(kt, cosb, sinb).astype(BF16)
    vb_ref[...] = proj(AB_VB, AB_GB).astype(BF16)
    sgb_ref[...] = _silu(proj(AB_GB, AB_IN)).astype(BF16)


def _inproj_ab(x2d, seq, g, w, gain, ones, tabs):
    rows = x2d.shape[0]
    tm = ROW_TILE
    per_seq = seq // tm
    row = lambda i: (i, 0)
    const = lambda i: (0, 0)
    tab = lambda i: (i % per_seq, 0)
    kvrow = lambda i: (0, i, 0)
    out_shape = (
        jax.ShapeDtypeStruct((rows, A_W), BF16), jax.ShapeDtypeStruct((A_KV, rows, HEAD_DIM), BF16),
        jax.ShapeDtypeStruct((A_KV, rows, HEAD_DIM), BF16), jax.ShapeDtypeStruct((rows, A_W), BF16),
        jax.ShapeDtypeStruct((rows, B_QKW), BF16), jax.ShapeDtypeStruct((rows, B_QKW), BF16),
        jax.ShapeDtypeStruct((rows, B_VW), BF16), jax.ShapeDtypeStruct((rows, B_VW), BF16),
    )
    out_specs = (
        pl.BlockSpec((tm, A_W), row), pl.BlockSpec((A_KV, tm, HEAD_DIM), kvrow),
        pl.BlockSpec((A_KV, tm, HEAD_DIM), kvrow), pl.BlockSpec((tm, A_W), row),
        pl.BlockSpec((tm, B_QKW), row), pl.BlockSpec((tm, B_QKW), row),
        pl.BlockSpec((tm, B_VW), row), pl.BlockSpec((tm, B_VW), row),
    )
    in_specs = [
        pl.BlockSpec((tm, D_MODEL), row), pl.BlockSpec((1, D_MODEL), const),
        pl.BlockSpec((D_MODEL, AB_IN), const), pl.BlockSpec((2, LANES), const),
        pl.BlockSpec((LANES, LANES), const),
    ] + [pl.BlockSpec((tm, LANES), tab)] * 4
    return pl.pallas_call(
        _inproj_ab_kernel, grid=(rows // tm,), in_specs=in_specs, out_specs=out_specs,
        out_shape=out_shape, compiler_params=_params(("parallel",)), name="inproj_ab",
    )(x2d, g, w, gain, ones, *tabs)


def _dense_attn_kernel(q_ref, k_ref, v_ref, sg_ref, o_ref, *, seq):
    tq, tk = ATT_TQ, ATT_TK
    q = q_ref[...]
    qs = jnp.concatenate([q[:, h * HEAD_DIM:(h + 1) * HEAD_DIM] for h in range(A_GROUP)], axis=0)
    rows = A_GROUP * tq

    def body(c, carry):
        m, l, acc = carry
        start = pl.multiple_of(c * tk, tk)
        k = k_ref[pl.ds(start, tk), :]
        v = v_ref[pl.ds(start, tk), :]
        s = lax.dot_general(qs, k, (((1,), (1,)), ((), ())), preferred_element_type=F32)
        m_new = jnp.maximum(m, jnp.max(s, axis=-1, keepdims=True))
        alpha = jnp.exp(m - m_new)
        p = jnp.exp(s - m_new)
        l = alpha * l + jnp.sum(p, axis=-1, keepdims=True)
        acc = alpha * acc + jnp.dot(p.astype(BF16), v, preferred_element_type=F32)
        return m_new, l, acc

    init = (jnp.full((rows, 1), NEG, F32), jnp.zeros((rows, 1), F32), jnp.zeros((rows, HEAD_DIM), F32))
    _, l, acc = lax.fori_loop(0, seq // tk, body, init)
    o = acc / l
    o = jnp.concatenate([o[h * tq:(h + 1) * tq] for h in range(A_GROUP)], axis=1)
    o_ref[...] = (o * sg_ref[...].astype(F32)).astype(BF16)


def _dense_attn(qa, ka, va, sga, batch, seq):
    gw = A_GROUP * HEAD_DIM
    q3 = qa.reshape(batch, seq, A_W)
    sg3 = sga.reshape(batch, seq, A_W)
    k4 = ka.reshape(A_KV, batch, seq, HEAD_DIM)
    v4 = va.reshape(A_KV, batch, seq, HEAD_DIM)
    qspec = pl.BlockSpec((None, ATT_TQ, gw), lambda b, g, i: (b, i, g))
    kspec = pl.BlockSpec((None, None, seq, HEAD_DIM), lambda b, g, i: (g, b, 0, 0))
    out = pl.pallas_call(
        functools.partial(_dense_attn_kernel, seq=seq), grid=(batch, A_KV, seq // ATT_TQ),
        in_specs=[qspec, kspec, kspec, qspec], out_specs=qspec,
        out_shape=jax.ShapeDtypeStruct((batch, seq, A_W), BF16),
        compiler_params=_params(("parallel", "parallel", "parallel")), name="dense_attn",
    )(q3, k4, v4, sg3)
    return out.reshape(batch * seq, A_W)


def _retention_kernel(lg_ref, q_ref, k_ref, v_ref, sg_ref, o_ref, st_ref, *, seq):
    c = CHUNK
    nc = seq // c
    pair = pl.program_id(1)
    lgf = [lg_ref[0, 2 * pair + hh] for hh in range(2)]
    lgb = [lg_ref[1, 2 * pair + hh] for hh in range(2)]
    lane = lax.broadcasted_iota(jnp.int32, (1, LANES), 1)
    head_of_lane = [lane < HEAD_DIM, lane >= HEAD_DIM]
    lane_lgf = jnp.where(head_of_lane[0], lgf[0], lgf[1])
    lane_lgb = jnp.where(head_of_lane[0], lgb[0], lgb[1])
    t = lax.broadcasted_iota(jnp.int32, (c, 1), 0).astype(F32)
    kdec_f = jnp.exp((c - 1.0 - t) * lane_lgf)
    kdec_b = jnp.exp(t * lane_lgb)
    qdec_f = jnp.exp((t + 1.0) * lane_lgf)
    qdec_b = jnp.exp((c - t) * lane_lgb)
    row = lax.broadcasted_iota(jnp.int32, (c, 1), 0)
    row_first = row < HEAD_DIM
    cdec_f = jnp.exp(c * jnp.where(row_first, lgf[0], lgf[1]))
    cdec_b = jnp.exp(c * jnp.where(row_first, lgb[0], lgb[1]))
    diff = (lax.broadcasted_iota(jnp.int32, (c, c), 0) - lax.broadcasted_iota(jnp.int32, (c, c), 1)).astype(F32)
    intra_decay = [jnp.where(diff >= 0, jnp.exp(lgf[hh] * jnp.maximum(diff, 0.0)),
                             jnp.exp(lgb[hh] * jnp.maximum(-diff, 0.0))) for hh in range(2)]

    def chunk_rows(n):
        return pl.ds(pl.multiple_of(n * c, c), c)

    def kv_body(n, carry):
        rows = chunk_rows(n)
        k = k_ref[rows, :].astype(F32)
        kf, kb = k * kdec_f, k * kdec_b
        kvf = jnp.zeros((LANES, B_VAL), F32)
        kvb = jnp.zeros((LANES, B_VAL), F32)
        for hh in range(2):
            v = v_ref[rows, hh * B_VAL:(hh + 1) * B_VAL]
            dn = (((0,), (0,)), ((), ()))
            kvf += lax.dot_general(jnp.where(head_of_lane[hh], kf, 0.0).astype(BF16), v, dn,
                                   preferred_element_type=F32)
            kvb += lax.dot_general(jnp.where(head_of_lane[hh], kb, 0.0).astype(BF16), v, dn,
                                   preferred_element_type=F32)
        st_ref[n, 0:LANES, :] = kvf
        st_ref[n, LANES:2 * LANES, :] = kvb
        return carry

    lax.fori_loop(0, nc, kv_body, 0)

    def scan_fwd(n, state):
        kv = st_ref[n, 0:LANES, :]
        st_ref[n, 0:LANES, :] = state
        return state * cdec_f + kv

    lax.fori_loop(0, nc, scan_fwd, jnp.zeros((LANES, B_VAL), F32))

    def scan_bwd(i, state):
        n = nc - 1 - i
        kv = st_ref[n, LANES:2 * LANES, :]
        st_ref[n, LANES:2 * LANES, :] = state
        return state * cdec_b + kv

    lax.fori_loop(0, nc, scan_bwd, jnp.zeros((LANES, B_VAL), F32))

    def out_body(n, carry):
        rows = chunk_rows(n)
        q = q_ref[rows, :].astype(F32)
        k = k_ref[rows, :]
        qf, qb = q * qdec_f, q * qdec_b
        state = st_ref[n].astype(BF16)
        for hh in range(2):
            v = v_ref[rows, hh * B_VAL:(hh + 1) * B_VAL]
            qm = jnp.where(head_of_lane[hh], q, 0.0).astype(BF16)
            s = lax.dot_general(qm, k, (((1,), (1,)), ((), ())), preferred_element_type=F32)
            intra = jnp.dot((s * intra_decay[hh]).astype(BF16), v, preferred_element_type=F32)
            lhs = jnp.concatenate([jnp.where(head_of_lane[hh], qf, 0.0),
                                   jnp.where(head_of_lane[hh], qb, 0.0)], axis=1).astype(BF16)
            o = intra + jnp.dot(lhs, state, preferred_element_type=F32)
            o = o * lax.rsqrt(jnp.mean(o * o, axis=-1, keepdims=True) + EPS)
            o = o * sg_ref[rows, hh * B_VAL:(hh + 1) * B_VAL].astype(F32)
            o_ref[rows, hh * B_VAL:(hh + 1) * B_VAL] = o.astype(BF16)
        return carry

    lax.fori_loop(0, nc, out_body, 0)


def _retention(log_gamma, qb, kb, vb, sgb, batch, seq):
    pairs = B_HEADS // 2
    qk_spec = pl.BlockSpec((None, seq, LANES), lambda b, j: (b, 0, j))
    v_spec = pl.BlockSpec((None, seq, 2 * B_VAL), lambda b, j: (b, 0, j))
    out = pl.pallas_call(
        functools.partial(_retention_kernel, seq=seq), grid=(batch, pairs),
        in_specs=[pl.BlockSpec(memory_space=pltpu.SMEM), qk_spec, qk_spec, v_spec, v_spec],
        out_specs=v_spec, out_shape=jax.ShapeDtypeStruct((batch, seq, B_VW), BF16),
        scratch_shapes=[pltpu.VMEM((seq // CHUNK, 2 * LANES, B_VAL), F32)],
        compiler_params=_params(("parallel", "parallel")), name="retention",
    )(log_gamma, qb.reshape(batch, seq, B_QKW), kb.reshape(batch, seq, B_QKW),
      vb.reshape(batch, seq, B_VW), sgb.reshape(batch, seq, B_VW))
    return out.reshape(batch * seq, B_VW)


def _mid_proj_kernel(x_ref, ma_ref, mb_ref, wo_ref, g_ref, wc_ref, x1_ref, q_ref, k_ref, v_ref, sg_ref):
    y = jnp.dot(ma_ref[...], wo_ref[0:A_W, :], preferred_element_type=F32)
    y += jnp.dot(mb_ref[...], wo_ref[A_W:A_W + B_VW, :], preferred_element_type=F32)
    x1 = x_ref[...] + y
    x1_ref[...] = x1
    xn = _rmsnorm_rows(x1, g_ref[...]).astype(BF16)

    def proj(lo, hi):
        return jnp.dot(xn, wc_ref[:, lo:hi], preferred_element_type=F32)

    q_ref[...] = (proj(C_Q, C_K) * (HEAD_DIM ** -0.5)).astype(BF16)
    zkv = proj(C_K, C_G)
    _split_heads_store(k_ref, zkv[:, :LANES])
    _split_heads_store(v_ref, zkv[:, LANES:])
    sg_ref[...] = _silu(proj(C_G, C_IN)).astype(BF16)


def _mid_proj(x2d, ma, mb, wo, g, wc):
    rows = x2d.shape[0]
    tm = ROW_TILE
    row = lambda i: (i, 0)
    const = lambda i: (0, 0)
    kvrow = lambda i: (0, i, 0)
    out_shape = (
        jax.ShapeDtypeStruct((rows, D_MODEL), F32), jax.ShapeDtypeStruct((rows, C_W), BF16),
        jax.ShapeDtypeStruct((C_KV, rows, HEAD_DIM), BF16), jax.ShapeDtypeStruct((C_KV, rows, HEAD_DIM), BF16),
        jax.ShapeDtypeStruct((rows, C_W), BF16),
    )
    out_specs = (
        pl.BlockSpec((tm, D_MODEL), row), pl.BlockSpec((tm, C_W), row),
        pl.BlockSpec((C_KV, tm, HEAD_DIM), kvrow), pl.BlockSpec((C_KV, tm, HEAD_DIM), kvrow),
        pl.BlockSpec((tm, C_W), row),
    )
    in_specs = [
        pl.BlockSpec((tm, D_MODEL), row), pl.BlockSpec((tm, A_W), row), pl.BlockSpec((tm, B_VW), row),
        pl.BlockSpec((A_W + B_VW, D_MODEL), const), pl.BlockSpec((1, D_MODEL), const),
        pl.BlockSpec((D_MODEL, C_IN), const),
    ]
    return pl.pallas_call(
        _mid_proj_kernel, grid=(rows // tm,), in_specs=in_specs, out_specs=out_specs,
        out_shape=out_shape, compiler_params=_params(("parallel",)), name="mid_proj",
    )(x2d, ma, mb, wo, g, wc)


def _win_attn_kernel(q_ref, kp_ref, kc_ref, kn_ref, vp_ref, vc_ref, vn_ref, bias_ref, sink_ref, sg_ref, o_ref):
    i = pl.program_id(1)
    nb = pl.num_programs(1)
    col = lax.broadcasted_iota(jnp.int32, (1, 3 * BLOCK), 1)
    outside = ((col < BLOCK) & (i == 0)) | ((col >= 2 * BLOCK) & (i == nb - 1))
    edge = jnp.where(outside, NEG, 0.0)
    gw = C_GROUP * HEAD_DIM
    for g in range(C_KV):
        q = q_ref[:, g * gw:(g + 1) * gw]
        qs = jnp.concatenate([q[:, h * HEAD_DIM:(h + 1) * HEAD_DIM] for h in range(C_GROUP)], axis=0)
        kw = jnp.concatenate([kp_ref[g], kc_ref[g], kn_ref[g]], axis=0)
        vw = jnp.concatenate([vp_ref[g], vc_ref[g], vn_ref[g]], axis=0)
        s = lax.dot_general(qs, kw, (((1,), (1,)), ((), ())), preferred_element_type=F32)
        s = s + bias_ref[g] + edge
        sink = sink_ref[g]
        mx = jnp.maximum(jnp.max(s, axis=-1, keepdims=True), sink)
        e = jnp.exp(s - mx)
        denom = jnp.sum(e, axis=-1, keepdims=True) + jnp.exp(sink - mx)
        o = jnp.dot(e.astype(BF16), vw, preferred_element_type=F32) / denom
        o = jnp.concatenate([o[h * BLOCK:(h + 1) * BLOCK] for h in range(C_GROUP)], axis=1)
        o_ref[:, g * gw:(g + 1) * gw] = (o * sg_ref[:, g * gw:(g + 1) * gw].astype(F32)).astype(BF16)


def _win_attn(q, k, v, bias, sink, sg, batch, seq):
    nb = seq // BLOCK
    q3 = q.reshape(batch, seq, C_W)
    sg3 = sg.reshape(batch, seq, C_W)
    k4 = k.reshape(C_KV, batch, seq, HEAD_DIM)
    v4 = v.reshape(C_KV, batch, seq, HEAD_DIM)
    qspec = pl.BlockSpec((None, BLOCK, C_W), lambda b, i: (b, i, 0))
    kv_block = (C_KV, None, BLOCK, HEAD_DIM)
    prev = pl.BlockSpec(kv_block, lambda b, i: (0, b, jnp.maximum(i - 1, 0), 0))
    cur = pl.BlockSpec(kv_block, lambda b, i: (0, b, i, 0))
    nxt = pl.BlockSpec(kv_block, lambda b, i: (0, b, jnp.minimum(i + 1, nb - 1), 0))
    const3 = lambda b, i: (0, 0, 0)
    out = pl.pallas_call(
        _win_attn_kernel, grid=(batch, nb),
        in_specs=[qspec, prev, cur, nxt, prev, cur, nxt,
                  pl.BlockSpec((C_KV, C_GROUP * BLOCK, 3 * BLOCK), const3),
                  pl.BlockSpec((C_KV, C_GROUP * BLOCK, 1), const3), qspec],
        out_specs=qspec, out_shape=jax.ShapeDtypeStruct((batch, seq, C_W), BF16),
        compiler_params=_params(("parallel", "parallel")), name="win_attn",
    )(q3, k4, k4, k4, v4, v4, v4, bias, sink, sg3)
    return out.reshape(batch * seq, C_W)


def _out_proj_kernel(x_ref, m_ref, wo_ref, g_ref, y_ref):
    x2 = x_ref[...] + jnp.dot(m_ref[...], wo_ref[...], preferred_element_type=F32)
    y_ref[...] = _rmsnorm_rows(x2, g_ref[...])


def _out_proj(x2d, m, wo, g):
    rows = x2d.shape[0]
    tm = ROW_TILE
    row = lambda i: (i, 0)
    const = lambda i: (0, 0)
    return pl.pallas_call(
        _out_proj_kernel, grid=(rows // tm,),
        in_specs=[pl.BlockSpec((tm, D_MODEL), row), pl.BlockSpec((tm, C_W), row),
                  pl.BlockSpec((C_W, D_MODEL), const), pl.BlockSpec((1, D_MODEL), const)],
        out_specs=pl.BlockSpec((tm, D_MODEL), row), out_shape=jax.ShapeDtypeStruct((rows, D_MODEL), F32),
        compiler_params=_params(("parallel",)), name="out_proj",
    )(x2d, m, wo, g)


def _trunk(x, prm):
    batch, seq, _ = x.shape
    x2d = x.reshape(batch * seq, D_MODEL)
    qa, ka, va, sga, qb, kb, vb, sgb = _inproj_ab(
        x2d, seq, prm["g0"], prm["w_in_ab"], prm["gain"], prm["ones"], prm["tabs"])
    ma = _dense_attn(qa, ka, va, sga, batch, seq)
    mb = _retention(prm["log_gamma"], qb, kb, vb, sgb, batch, seq)
    x1, qc, kc, vc, sgc = _mid_proj(x2d, ma, mb, prm["w_out_ab"], prm["g1"], prm["w_in_c"])
    mc = _win_attn(qc, kc, vc, prm["bias"], prm["sink"], sgc, batch, seq)
    y = _out_proj(x1, mc, prm["w_out_c"], prm["gf"])
    return y.reshape(batch, seq, D_MODEL)


def kernel(x_prompt, x_sample, norm_g, w_in_ab, qk_norm_a, ret_decay, w_out_ab, w_in_c, sink_c, w_out_c,
           rel_bias, final_norm):
    assert w_in_ab.shape[0] == 1 and w_in_c.shape[0] == 1, "two-layer trunk: one layer of each kind"
    max_seq = max(x_prompt.shape[1], x_sample.shape[1])

    cols = np.arange(AB_IN)
    for lo, width in ((AB_QA, A_W), (AB_KA, A_KVW), (AB_QB, B_QKW), (AB_KB, B_QKW)):
        cols[lo:lo + width] = lo + _deinterleave_perm(width)
    head_perm = _deinterleave_perm(HEAD_DIM)
    gain = jnp.stack([jnp.tile(qk_norm_a[0, 0][head_perm], 2) * (HEAD_DIM ** -0.5),
                      jnp.tile(qk_norm_a[0, 1][head_perm], 2)]).astype(F32)
    ones = np.kron(np.eye(2), np.ones((HEAD_DIM, HEAD_DIM)))

    rel = np.arange(3 * BLOCK)[None, :] - BLOCK - np.arange(BLOCK)[:, None]
    bias = rel_bias.astype(F32)[jnp.asarray(_t5_bucket(rel))]
    bias = jnp.where(jnp.asarray(np.abs(rel) <= WINDOW)[:, :, None], bias, NEG)
    bias = bias.transpose(2, 0, 1).reshape(C_KV, C_GROUP * BLOCK, 3 * BLOCK)
    sink = jnp.repeat(sink_c[0].astype(F32), BLOCK).reshape(C_KV, C_GROUP * BLOCK, 1)

    prm = {
        "g0": norm_g[0].reshape(1, D_MODEL), "g1": norm_g[1].reshape(1, D_MODEL),
        "gf": final_norm.reshape(1, D_MODEL),
        "w_in_ab": w_in_ab[0][:, cols].astype(BF16), "w_out_ab": w_out_ab[0].astype(BF16),
        "w_in_c": w_in_c[0].astype(BF16), "w_out_c": w_out_c[0].astype(BF16),
        "gain": gain, "ones": jnp.asarray(ones, BF16),
        "tabs": _rope_tables(_axial_angles(max_seq)) + _rope_tables(_linear_angles(max_seq)),
        "log_gamma": -jnp.exp(ret_decay[0].astype(F32)),
        "bias": bias, "sink": sink,
    }
    return _trunk(x_prompt, prm), _trunk(x_sample, prm)
```

```python
import functools

import jax
import jax.numpy as jnp
import numpy as np
from jax import lax
from jax.experimental import pallas as pl
from jax.experimental.pallas import tpu as pltpu

F32 = jnp.float32
BF16 = jnp.bfloat16

D_MODEL = 1024
GRID_W = 64
EPS = 1e-6
ROPE_THETA = 10000.0
HEAD_DIM = 64
HALF = HEAD_DIM // 2
LANES = 128

A_HEADS, A_KV = 8, 2
A_GROUP = A_HEADS // A_KV
A_W, A_KVW = A_HEADS * HEAD_DIM, A_KV * HEAD_DIM
B_HEADS, B_VAL = 4, 128
B_QKW, B_VW = B_HEADS * HEAD_DIM, B_HEADS * B_VAL
C_HEADS, C_KV = 16, 2
C_GROUP = C_HEADS // C_KV
C_W, C_KVW = C_HEADS * HEAD_DIM, C_KV * HEAD_DIM
BLOCK = 128
WINDOW = 128
REL_BUCKETS, REL_MAX_DIST = 32, 128
CHUNK = 128
NEG = -1e30
LOG2E = float(np.log2(np.e))
ONES_ROWS = 16

AB_QA, AB_KA, AB_VA, AB_GA = 0, A_W, A_W + A_KVW, A_W + 2 * A_KVW
AB_QB = AB_GA + A_W
AB_KB, AB_VB, AB_GB = AB_QB + B_QKW, AB_QB + 2 * B_QKW, AB_QB + 2 * B_QKW + B_VW
AB_IN = AB_GB + B_VW
C_Q, C_K, C_V, C_G = 0, C_W, C_W + C_KVW, C_W + 2 * C_KVW
C_IN = C_G + C_W

ROW_TILE = 512
ATT_TQ = 128
ATT_TK = 512
VMEM_LIMIT = 48 * 1024 * 1024


def _params(sem):
    return pltpu.CompilerParams(dimension_semantics=sem, vmem_limit_bytes=VMEM_LIMIT)


def _deinterleave_perm(width):
    heads = width // HEAD_DIM
    within = np.concatenate([np.arange(0, HEAD_DIM, 2), np.arange(1, HEAD_DIM, 2)])
    return (np.arange(heads)[:, None] * HEAD_DIM + within[None, :]).reshape(-1)


def _rope_tables(ang):
    c, s = np.cos(ang), np.sin(ang)
    cos_t = np.concatenate([c, c, c, c], axis=-1)
    sin_t = np.concatenate([-s, s, -s, s], axis=-1)
    return jnp.asarray(cos_t, F32), jnp.asarray(sin_t, F32)


def _axial_angles(n):
    t = np.arange(n)
    quarter = HEAD_DIM // 4
    freqs = ROPE_THETA ** (-np.arange(quarter, dtype=np.float64) / quarter)
    row = (t // GRID_W).astype(np.float64)
    col = (t % GRID_W).astype(np.float64)
    return np.concatenate([row[:, None] * freqs, col[:, None] * freqs], axis=-1)


def _linear_angles(n):
    freqs = ROPE_THETA ** (-np.arange(HALF, dtype=np.float64) / HALF)
    return np.arange(n, dtype=np.float64)[:, None] * freqs


def _t5_bucket(rel):
    half = REL_BUCKETS // 2
    max_exact = half // 2
    ret = (rel > 0).astype(np.int32) * half
    dist = np.abs(rel)
    large = max_exact + (np.log(np.maximum(dist, 1) / max_exact) / np.log(REL_MAX_DIST / max_exact)
                         * (half - max_exact)).astype(np.int32)
    large = np.minimum(large, half - 1)
    return ret + np.where(dist < max_exact, dist, large)


def _rmsnorm_rows(x, gain):
    return x * lax.rsqrt(jnp.mean(x * x, axis=-1, keepdims=True) + EPS) * gain


def _silu(z):
    return z * jax.nn.sigmoid(z)


def _rotate_halves(x, cos_t, sin_t):
    lane = lax.broadcasted_iota(jnp.int32, (1, LANES), 1)
    first_half = (lane % HEAD_DIM) < HALF
    partner = jnp.where(first_half, pltpu.roll(x, LANES - HALF, 1), pltpu.roll(x, HALF, 1))
    return x * cos_t + partner * sin_t


def _split_heads_store(ref, tile):
    ref[0] = tile[:, :HEAD_DIM].astype(ref.dtype)
    ref[1] = tile[:, HEAD_DIM:].astype(ref.dtype)


def _transposed_chunks_store(ref, tile):
    chunk = ref.shape[2]
    tile_t = tile.T
    for c in range(ref.shape[0]):
        ref[c] = tile_t[:, c * chunk:(c + 1) * chunk].astype(ref.dtype)


def _stack_heads_on_rows(q, heads):
    return jnp.concatenate([q[:, h * HEAD_DIM:(h + 1) * HEAD_DIM] for h in range(heads)], axis=0)


def _unstack_transposed(o_t, heads, tokens):
    by_head = jnp.concatenate([o_t[:, h * tokens:(h + 1) * tokens] for h in range(heads)], axis=0)
    return by_head.T


def _with_ones_rows(v_t):
    return jnp.concatenate([v_t, jnp.ones((ONES_ROWS, v_t.shape[1]), v_t.dtype)], axis=0)


def _inproj_ab_kernel(x_ref, g_ref, w_ref, gain_ref, ones_ref, cosa_ref, sina_ref, cosb_ref, sinb_ref,
                      qa_ref, ka_ref, va_ref, sga_ref, qb_ref, kb_ref, vb_ref, sgb_ref):
    xn = _rmsnorm_rows(x_ref[...], g_ref[...]).astype(BF16)

    def proj(lo, hi):
        return jnp.dot(xn, w_ref[:, lo:hi], preferred_element_type=F32)

    cosa, sina = cosa_ref[...], sina_ref[...]
    cosb, sinb = cosb_ref[...], sinb_ref[...]

    def head_norm_rope(tile, gain):
        ss = jnp.dot((tile * tile).astype(BF16), ones_ref[...], preferred_element_type=F32)
        tile = tile * lax.rsqrt(ss * (1.0 / HEAD_DIM) + EPS) * gain
        return _rotate_halves(tile, cosa, sina)

    zq = proj(AB_QA, AB_KA)
    for c in range(A_W // LANES):
        tile = head_norm_rope(zq[:, c * LANES:(c + 1) * LANES], gain_ref[0:1, :])
        qa_ref[:, c * LANES:(c + 1) * LANES] = tile.astype(BF16)
    zkv = proj(AB_KA, AB_GA)
    _split_heads_store(ka_ref, head_norm_rope(zkv[:, :LANES], gain_ref[1:2, :]))
    _transposed_chunks_store(va_ref, zkv[:, LANES:])
    sga_ref[...] = _silu(proj(AB_GA, AB_QB)).astype(BF16)

    zqk = proj(AB_QB, AB_VB)
    for c in range(B_QKW // LANES):
        qb_ref[:, c * LANES:(c + 1) * LANES] = _rotate_halves(
            zqk[:, c * LANES:(c + 1) * LANES], cosb, sinb).astype(BF16)
        kt = zqk[:, B_QKW + c * LANES:B_QKW + (c + 1) * LANES] * (HEAD_DIM ** -0.5)
        kb_ref[:, c * LANES:(c + 1) * LANES] = _rotate_halves(kt, cosb, sinb).astype(BF16)
    vb_ref[...] = proj(AB_VB, AB_GB).astype(BF16)
    sgb_ref[...] = _silu(proj(AB_GB, AB_IN)).astype(BF16)


def _inproj_ab(x2d, seq, g, w, gain, ones, tabs):
    rows = x2d.shape[0]
    tm = ROW_TILE
    per_seq = seq // tm
    row = lambda i: (i, 0)
    const = lambda i: (0, 0)
    tab = lambda i: (i % per_seq, 0)
    kvrow = lambda i: (0, i, 0)
    vt_chunks = tm // ATT_TK
    out_shape = (
        jax.ShapeDtypeStruct((rows, A_W), BF16), jax.ShapeDtypeStruct((A_KV, rows, HEAD_DIM), BF16),
        jax.ShapeDtypeStruct((rows // ATT_TK, A_KVW, ATT_TK), BF16), jax.ShapeDtypeStruct((rows, A_W), BF16),
        jax.ShapeDtypeStruct((rows, B_QKW), BF16), jax.ShapeDtypeStruct((rows, B_QKW), BF16),
        jax.ShapeDtypeStruct((rows, B_VW), BF16), jax.ShapeDtypeStruct((rows, B_VW), BF16),
    )
    out_specs = (
        pl.BlockSpec((tm, A_W), row), pl.BlockSpec((A_KV, tm, HEAD_DIM), kvrow),
        pl.BlockSpec((vt_chunks, A_KVW, ATT_TK), lambda i: (i, 0, 0)), pl.BlockSpec((tm, A_W), row),
        pl.BlockSpec((tm, B_QKW), row), pl.BlockSpec((tm, B_QKW), row),
        pl.BlockSpec((tm, B_VW), row), pl.BlockSpec((tm, B_VW), row),
    )
    in_specs = [
        pl.BlockSpec((tm, D_MODEL), row), pl.BlockSpec((1, D_MODEL), const),
        pl.BlockSpec((D_MODEL, AB_IN), const), pl.BlockSpec((2, LANES), const),
        pl.BlockSpec((LANES, LANES), const),
    ] + [pl.BlockSpec((tm, LANES), tab)] * 4
    return pl.pallas_call(
        _inproj_ab_kernel, grid=(rows // tm,), in_specs=in_specs, out_specs=out_specs,
        out_shape=out_shape, compiler_params=_params(("parallel",)), name="inproj_ab",
    )(x2d, g, w, gain, ones, *tabs)


def _dense_attn_kernel(q_ref, k_ref, vt_ref, sg_ref, o_ref, *, seq):
    tq, tk = ATT_TQ, ATT_TK
    qs = _stack_heads_on_rows(q_ref[...], A_GROUP)
    rows = A_GROUP * tq

    def body(c, carry):
        m, acc = carry
        k = k_ref[pl.ds(pl.multiple_of(c * tk, tk), tk), :]
        s_t = lax.dot_general(k, qs, (((1,), (1,)), ((), ())), preferred_element_type=F32)
        m_new = jnp.maximum(m, jnp.max(s_t, axis=0, keepdims=True))
        alpha = jnp.exp2(m - m_new)
        p_t = jnp.exp2(s_t - m_new).astype(BF16)
        acc = alpha * acc + jnp.dot(_with_ones_rows(vt_ref[c]), p_t, preferred_element_type=F32)
        return m_new, acc

    init = (jnp.full((1, rows), NEG, F32), jnp.zeros((HEAD_DIM + ONES_ROWS, rows), F32))
    _, acc = lax.fori_loop(0, seq // tk, body, init, unroll=True)
    o_t = acc[:HEAD_DIM] / acc[HEAD_DIM:HEAD_DIM + 1]
    o = _unstack_transposed(o_t, A_GROUP, tq)
    o_ref[...] = (o * sg_ref[...].astype(F32)).astype(BF16)


def _dense_attn(qa, ka, vat, sga, batch, seq):
    gw = A_GROUP * HEAD_DIM
    q3 = qa.reshape(batch, seq, A_W)
    sg3 = sga.reshape(batch, seq, A_W)
    k4 = ka.reshape(A_KV, batch, seq, HEAD_DIM)
    vt4 = vat.reshape(batch, seq // ATT_TK, A_KVW, ATT_TK)
    qspec = pl.BlockSpec((None, ATT_TQ, gw), lambda b, g, i: (b, i, g))
    kspec = pl.BlockSpec((None, None, seq, HEAD_DIM), lambda b, g, i: (g, b, 0, 0))
    vtspec = pl.BlockSpec((None, seq // ATT_TK, HEAD_DIM, ATT_TK), lambda b, g, i: (b, 0, g, 0))
    out = pl.pallas_call(
        functools.partial(_dense_attn_kernel, seq=seq), grid=(batch, A_KV, seq // ATT_TQ),
        in_specs=[qspec, kspec, vtspec, qspec], out_specs=qspec,
        out_shape=jax.ShapeDtypeStruct((batch, seq, A_W), BF16),
        compiler_params=_params(("parallel", "parallel", "parallel")), name="dense_attn",
    )(q3, k4, vt4, sg3)
    return out.reshape(batch * seq, A_W)


def _retention_kernel(lg_ref, q_ref, k_ref, v_ref, sg_ref, o_ref, st_ref, *, seq):
    c = CHUNK
    nc = seq // c
    pair = pl.program_id(1)
    lgf = [lg_ref[0, 2 * pair + hh] for hh in range(2)]
    lgb = [lg_ref[1, 2 * pair + hh] for hh in range(2)]
    lane = lax.broadcasted_iota(jnp.int32, (1, LANES), 1)
    head_of_lane = [lane < HEAD_DIM, lane >= HEAD_DIM]
    lane_lgf = jnp.where(head_of_lane[0], lgf[0], lgf[1])
    lane_lgb = jnp.where(head_of_lane[0], lgb[0], lgb[1])
    t = lax.broadcasted_iota(jnp.int32, (c, 1), 0).astype(F32)
    kdec_f = jnp.exp((c - 1.0 - t) * lane_lgf)
    kdec_b = jnp.exp(t * lane_lgb)
    qdec_f = jnp.exp((t + 1.0) * lane_lgf)
    qdec_b = jnp.exp((c - t) * lane_lgb)
    row = lax.broadcasted_iota(jnp.int32, (c, 1), 0)
    row_first = row < HEAD_DIM
    cdec_f = jnp.exp(c * jnp.where(row_first, lgf[0], lgf[1]))
    cdec_b = jnp.exp(c * jnp.where(row_first, lgb[0], lgb[1]))
    diff = (lax.broadcasted_iota(jnp.int32, (c, c), 0) - lax.broadcasted_iota(jnp.int32, (c, c), 1)).astype(F32)
    intra_decay = [jnp.where(diff >= 0, jnp.exp(lgf[hh] * jnp.maximum(diff, 0.0)),
                             jnp.exp(lgb[hh] * jnp.maximum(-diff, 0.0))) for hh in range(2)]

    def chunk_rows(n):
        return pl.ds(pl.multiple_of(n * c, c), c)

    def kv_body(n, carry):
        rows = chunk_rows(n)
        k = k_ref[rows, :].astype(F32)
        kf, kb = k * kdec_f, k * kdec_b
        kvf = jnp.zeros((LANES, B_VAL), F32)
        kvb = jnp.zeros((LANES, B_VAL), F32)
        for hh in range(2):
            v = v_ref[rows, hh * B_VAL:(hh + 1) * B_VAL]
            dn = (((0,), (0,)), ((), ()))
            kvf += lax.dot_general(jnp.where(head_of_lane[hh], kf, 0.0).astype(BF16), v, dn,
                                   preferred_element_type=F32)
            kvb += lax.dot_general(jnp.where(head_of_lane[hh], kb, 0.0).astype(BF16), v, dn,
                                   preferred_element_type=F32)
        st_ref[n, 0:LANES, :] = kvf
        st_ref[n, LANES:2 * LANES, :] = kvb
        return carry

    lax.fori_loop(0, nc, kv_body, 0)

    def scan_fwd(n, state):
        kv = st_ref[n, 0:LANES, :]
        st_ref[n, 0:LANES, :] = state
        return state * cdec_f + kv

    lax.fori_loop(0, nc, scan_fwd, jnp.zeros((LANES, B_VAL), F32))

    def scan_bwd(i, state):
        n = nc - 1 - i
        kv = st_ref[n, LANES:2 * LANES, :]
        st_ref[n, LANES:2 * LANES, :] = state
        return state * cdec_b + kv

    lax.fori_loop(0, nc, scan_bwd, jnp.zeros((LANES, B_VAL), F32))

    def out_body(n, carry):
        rows = chunk_rows(n)
        q = q_ref[rows, :].astype(F32)
        k = k_ref[rows, :]
        qf, qb = q * qdec_f, q * qdec_b
        state = st_ref[n].astype(BF16)
        for hh in range(2):
            v = v_ref[rows, hh * B_VAL:(hh + 1) * B_VAL]
            qm = jnp.where(head_of_lane[hh], q, 0.0).astype(BF16)
            s = lax.dot_general(qm, k, (((1,), (1,)), ((), ())), preferred_element_type=F32)
            intra = jnp.dot((s * intra_decay[hh]).astype(BF16), v, preferred_element_type=F32)
            lhs = jnp.concatenate([jnp.where(head_of_lane[hh], qf, 0.0),
                                   jnp.where(head_of_lane[hh], qb, 0.0)], axis=1).astype(BF16)
            o = intra + jnp.dot(lhs, state, preferred_element_type=F32)
            o = o * lax.rsqrt(jnp.mean(o * o, axis=-1, keepdims=True) + EPS)
            o = o * sg_ref[rows, hh * B_VAL:(hh + 1) * B_VAL].astype(F32)
            o_ref[rows, hh * B_VAL:(hh + 1) * B_VAL] = o.astype(BF16)
        return carry

    lax.fori_loop(0, nc, out_body, 0)


def _retention(log_gamma, qb, kb, vb, sgb, batch, seq):
    pairs = B_HEADS // 2
    qk_spec = pl.BlockSpec((None, seq, LANES), lambda b, j: (b, 0, j))
    v_spec = pl.BlockSpec((None, seq, 2 * B_VAL), lambda b, j: (b, 0, j))
    out = pl.pallas_call(
        functools.partial(_retention_kernel, seq=seq), grid=(batch, pairs),
        in_specs=[pl.BlockSpec(memory_space=pltpu.SMEM), qk_spec, qk_spec, v_spec, v_spec],
        out_specs=v_spec, out_shape=jax.ShapeDtypeStruct((batch, seq, B_VW), BF16),
        scratch_shapes=[pltpu.VMEM((seq // CHUNK, 2 * LANES, B_VAL), F32)],
        compiler_params=_params(("parallel", "parallel")), name="retention",
    )(log_gamma, qb.reshape(batch, seq, B_QKW), kb.reshape(batch, seq, B_QKW),
      vb.reshape(batch, seq, B_VW), sgb.reshape(batch, seq, B_VW))
    return out.reshape(batch * seq, B_VW)


def _mid_proj_kernel(x_ref, ma_ref, mb_ref, wo_ref, g_ref, wc_ref, x1_ref, q_ref, k_ref, v_ref, sg_ref):
    y = jnp.dot(ma_ref[...], wo_ref[0:A_W, :], preferred_element_type=F32)
    y += jnp.dot(mb_ref[...], wo_ref[A_W:A_W + B_VW, :], preferred_element_type=F32)
    x1 = x_ref[...] + y
    x1_ref[...] = x1
    xn = _rmsnorm_rows(x1, g_ref[...]).astype(BF16)

    def proj(lo, hi):
        return jnp.dot(xn, wc_ref[:, lo:hi], preferred_element_type=F32)

    q_ref[...] = (proj(C_Q, C_K) * (HEAD_DIM ** -0.5 * LOG2E)).astype(BF16)
    zkv = proj(C_K, C_G)
    _split_heads_store(k_ref, zkv[:, :LANES])
    _transposed_chunks_store(v_ref, zkv[:, LANES:])
    sg_ref[...] = _silu(proj(C_G, C_IN)).astype(BF16)


def _mid_proj(x2d, ma, mb, wo, g, wc):
    rows = x2d.shape[0]
    tm = ROW_TILE
    row = lambda i: (i, 0)
    const = lambda i: (0, 0)
    kvrow = lambda i: (0, i, 0)
    out_shape = (
        jax.ShapeDtypeStruct((rows, D_MODEL), F32), jax.ShapeDtypeStruct((rows, C_W), BF16),
        jax.ShapeDtypeStruct((C_KV, rows, HEAD_DIM), BF16), jax.ShapeDtypeStruct((rows // BLOCK, C_KVW, BLOCK), BF16),
        jax.ShapeDtypeStruct((rows, C_W), BF16),
    )
    out_specs = (
        pl.BlockSpec((tm, D_MODEL), row), pl.BlockSpec((tm, C_W), row),
        pl.BlockSpec((C_KV, tm, HEAD_DIM), kvrow), pl.BlockSpec((tm // BLOCK, C_KVW, BLOCK), lambda i: (i, 0, 0)),
        pl.BlockSpec((tm, C_W), row),
    )
    in_specs = [
        pl.BlockSpec((tm, D_MODEL), row), pl.BlockSpec((tm, A_W), row), pl.BlockSpec((tm, B_VW), row),
        pl.BlockSpec((A_W + B_VW, D_MODEL), const), pl.BlockSpec((1, D_MODEL), const),
        pl.BlockSpec((D_MODEL, C_IN), const),
    ]
    return pl.pallas_call(
        _mid_proj_kernel, grid=(rows // tm,), in_specs=in_specs, out_specs=out_specs,
        out_shape=out_shape, compiler_params=_params(("parallel",)), name="mid_proj",
    )(x2d, ma, mb, wo, g, wc)


def _win_attn_kernel(q_ref, kp_ref, kc_ref, kn_ref, vp_ref, vc_ref, vn_ref, bias_ref, sink_ref, sg_ref, o_ref):
    i = pl.program_id(1)
    nb = pl.num_programs(1)
    key_row = lax.broadcasted_iota(jnp.int32, (3 * BLOCK, 1), 0)
    outside = ((key_row < BLOCK) & (i == 0)) | ((key_row >= 2 * BLOCK) & (i == nb - 1))
    edge = jnp.where(outside, NEG, 0.0)
    gw = C_GROUP * HEAD_DIM
    for g in range(C_KV):
        qs = _stack_heads_on_rows(q_ref[:, g * gw:(g + 1) * gw], C_GROUP)
        kw = jnp.concatenate([kp_ref[g], kc_ref[g], kn_ref[g]], axis=0)
        heads = slice(g * HEAD_DIM, (g + 1) * HEAD_DIM)
        vw_t = jnp.concatenate([vp_ref[0, heads, :], vc_ref[0, heads, :], vn_ref[0, heads, :]], axis=1)
        s_t = lax.dot_general(kw, qs, (((1,), (1,)), ((), ())), preferred_element_type=F32)
        s_t = s_t + bias_ref[g] + edge
        sink = sink_ref[g]
        mx = jnp.maximum(jnp.max(s_t, axis=0, keepdims=True), sink)
        p_t = jnp.exp2(s_t - mx).astype(BF16)
        acc = jnp.dot(_with_ones_rows(vw_t), p_t, preferred_element_type=F32)
        denom = acc[HEAD_DIM:HEAD_DIM + 1] + jnp.exp2(sink - mx)
        o = _unstack_transposed(acc[:HEAD_DIM] / denom, C_GROUP, BLOCK)
        o_ref[:, g * gw:(g + 1) * gw] = (o * sg_ref[:, g * gw:(g + 1) * gw].astype(F32)).astype(BF16)


def _win_attn(q, k, vt, bias_t, sink, sg, batch, seq):
    nb = seq // BLOCK
    q3 = q.reshape(batch, seq, C_W)
    sg3 = sg.reshape(batch, seq, C_W)
    k4 = k.reshape(C_KV, batch, seq, HEAD_DIM)
    qspec = pl.BlockSpec((None, BLOCK, C_W), lambda b, i: (b, i, 0))
    k_block = (C_KV, None, BLOCK, HEAD_DIM)
    vt_block = (1, C_KVW, BLOCK)
    before = lambda i: jnp.maximum(i - 1, 0)
    after = lambda i: jnp.minimum(i + 1, nb - 1)
    const3 = lambda b, i: (0, 0, 0)
    out = pl.pallas_call(
        _win_attn_kernel, grid=(batch, nb),
        in_specs=[qspec,
                  pl.BlockSpec(k_block, lambda b, i: (0, b, before(i), 0)),
                  pl.BlockSpec(k_block, lambda b, i: (0, b, i, 0)),
                  pl.BlockSpec(k_block, lambda b, i: (0, b, after(i), 0)),
                  pl.BlockSpec(vt_block, lambda b, i: (b * nb + before(i), 0, 0)),
                  pl.BlockSpec(vt_block, lambda b, i: (b * nb + i, 0, 0)),
                  pl.BlockSpec(vt_block, lambda b, i: (b * nb + after(i), 0, 0)),
                  pl.BlockSpec((C_KV, 3 * BLOCK, C_GROUP * BLOCK), const3),
                  pl.BlockSpec((C_KV, 1, C_GROUP * BLOCK), const3), qspec],
        out_specs=qspec, out_shape=jax.ShapeDtypeStruct((batch, seq, C_W), BF16),
        compiler_params=_params(("parallel", "parallel")), name="win_attn",
    )(q3, k4, k4, k4, vt, vt, vt, bias_t, sink, sg3)
    return out.reshape(batch * seq, C_W)


def _out_proj_kernel(x_ref, m_ref, wo_ref, g_ref, y_ref):
    x2 = x_ref[...] + jnp.dot(m_ref[...], wo_ref[...], preferred_element_type=F32)
    y_ref[...] = _rmsnorm_rows(x2, g_ref[...])


def _out_proj(x2d, m, wo, g):
    rows = x2d.shape[0]
    tm = ROW_TILE
    row = lambda i: (i, 0)
    const = lambda i: (0, 0)
    return pl.pallas_call(
        _out_proj_kernel, grid=(rows // tm,),
        in_specs=[pl.BlockSpec((tm, D_MODEL), row), pl.BlockSpec((tm, C_W), row),
                  pl.BlockSpec((C_W, D_MODEL), const), pl.BlockSpec((1, D_MODEL), const)],
        out_specs=pl.BlockSpec((tm, D_MODEL), row), out_shape=jax.ShapeDtypeStruct((rows, D_MODEL), F32),
        compiler_params=_params(("parallel",)), name="out_proj",
    )(x2d, m, wo, g)


def _trunk(x, prm):
    batch, seq, _ = x.shape
    x2d = x.reshape(batch * seq, D_MODEL)
    qa, ka, va, sga, qb, kb, vb, sgb = _inproj_ab(
        x2d, seq, prm["g0"], prm["w_in_ab"], prm["gain"], prm["ones"], prm["tabs"])
    ma = _dense_attn(qa, ka, va, sga, batch, seq)
    mb = _retention(prm["log_gamma"], qb, kb, vb, sgb, batch, seq)
    x1, qc, kc, vc, sgc = _mid_proj(x2d, ma, mb, prm["w_out_ab"], prm["g1"], prm["w_in_c"])
    mc = _win_attn(qc, kc, vc, prm["bias"], prm["sink"], sgc, batch, seq)
    y = _out_proj(x1, mc, prm["w_out_c"], prm["gf"])
    return y.reshape(batch, seq, D_MODEL)


def kernel(x_prompt, x_sample, norm_g, w_in_ab, qk_norm_a, ret_decay, w_out_ab, w_in_c, sink_c, w_out_c,
           rel_bias, final_norm):
    assert w_in_ab.shape[0] == 1 and w_in_c.shape[0] == 1, "two-layer trunk: one layer of each kind"
    max_seq = max(x_prompt.shape[1], x_sample.shape[1])

    def deinterleaved(w_cols):
        d_in, width = w_cols.shape
        pairs = w_cols.reshape(d_in, width // HEAD_DIM, HALF, 2)
        return pairs.transpose(0, 1, 3, 2).reshape(d_in, width)

    w_ab = w_in_ab[0].astype(BF16)
    w_ab = jnp.concatenate([
        deinterleaved(w_ab[:, AB_QA:AB_KA]), deinterleaved(w_ab[:, AB_KA:AB_VA]), w_ab[:, AB_VA:AB_QB],
        deinterleaved(w_ab[:, AB_QB:AB_KB]), deinterleaved(w_ab[:, AB_KB:AB_VB]), w_ab[:, AB_VB:]], axis=1)
    head_perm = _deinterleave_perm(HEAD_DIM)
    gain = jnp.stack([jnp.tile(qk_norm_a[0, 0][head_perm], 2) * (HEAD_DIM ** -0.5 * LOG2E),
                      jnp.tile(qk_norm_a[0, 1][head_perm], 2)]).astype(F32)
    ones = np.kron(np.eye(2), np.ones((HEAD_DIM, HEAD_DIM)))

    rel = np.arange(3 * BLOCK)[:, None] - BLOCK - np.arange(BLOCK)[None, :]
    bias = rel_bias.astype(F32)[jnp.asarray(_t5_bucket(rel))] * LOG2E
    bias = jnp.where(jnp.asarray(np.abs(rel) <= WINDOW)[:, :, None], bias, NEG)
    bias = bias.reshape(3 * BLOCK, BLOCK, C_KV, C_GROUP).transpose(2, 0, 3, 1)
    bias = bias.reshape(C_KV, 3 * BLOCK, C_GROUP * BLOCK)
    sink = jnp.repeat(sink_c[0].astype(F32) * LOG2E, BLOCK).reshape(C_KV, 1, C_GROUP * BLOCK)

    prm = {
        "g0": norm_g[0].reshape(1, D_MODEL), "g1": norm_g[1].reshape(1, D_MODEL),
        "gf": final_norm.reshape(1, D_MODEL),
        "w_in_ab": w_ab, "w_out_ab": w_out_ab[0].astype(BF16),
        "w_in_c": w_in_c[0].astype(BF16), "w_out_c": w_out_c[0].astype(BF16),
        "gain": gain, "ones": jnp.asarray(ones, BF16),
        "tabs": _rope_tables(_axial_angles(max_seq)) + _rope_tables(_linear_angles(max_seq)),
        "log_gamma": -jnp.exp(ret_decay[0].astype(F32)),
        "bias": bias, "sink": sink,
    }
    return _trunk(x_prompt, prm), _trunk(x_sample, prm)
```

```python
import functools

import jax
import jax.numpy as jnp
import numpy as np
from jax import lax
from jax.experimental import pallas as pl
from jax.experimental.pallas import tpu as pltpu

F32 = jnp.float32
BF16 = jnp.bfloat16

D_MODEL = 1024
GRID_W = 64
EPS = 1e-6
ROPE_THETA = 10000.0
HEAD_DIM = 64
HALF = HEAD_DIM // 2
LANES = 128

A_HEADS, A_KV = 8, 2
A_GROUP = A_HEADS // A_KV
A_W, A_KVW = A_HEADS * HEAD_DIM, A_KV * HEAD_DIM
B_HEADS, B_VAL = 4, 128
B_QKW, B_VW = B_HEADS * HEAD_DIM, B_HEADS * B_VAL
C_HEADS, C_KV = 16, 2
C_GROUP = C_HEADS // C_KV
C_W, C_KVW = C_HEADS * HEAD_DIM, C_KV * HEAD_DIM
BLOCK = 128
WINDOW = 128
REL_BUCKETS, REL_MAX_DIST = 32, 128
CHUNK = 128
NEG = -1e30
LOG2E = float(np.log2(np.e))
ONES_ROWS = 16

AB_QA, AB_KA, AB_VA, AB_GA = 0, A_W, A_W + A_KVW, A_W + 2 * A_KVW
AB_QB = AB_GA + A_W
AB_KB, AB_VB, AB_GB = AB_QB + B_QKW, AB_QB + 2 * B_QKW, AB_QB + 2 * B_QKW + B_VW
AB_IN = AB_GB + B_VW
C_Q, C_K, C_V, C_G = 0, C_W, C_W + C_KVW, C_W + 2 * C_KVW
C_IN = C_G + C_W

ROW_TILE = 512
ATT_TQ = 128
ATT_TK = 512
VMEM_LIMIT = 48 * 1024 * 1024


def _params(sem):
    return pltpu.CompilerParams(dimension_semantics=sem, vmem_limit_bytes=VMEM_LIMIT)


def _deinterleave_perm(width):
    heads = width // HEAD_DIM
    within = np.concatenate([np.arange(0, HEAD_DIM, 2), np.arange(1, HEAD_DIM, 2)])
    return (np.arange(heads)[:, None] * HEAD_DIM + within[None, :]).reshape(-1)


def _rope_tables(ang):
    c, s = np.cos(ang), np.sin(ang)
    cos_t = np.concatenate([c, c, c, c], axis=-1)
    sin_t = np.concatenate([-s, s, -s, s], axis=-1)
    return jnp.asarray(cos_t, F32), jnp.asarray(sin_t, F32)


def _axial_angles(n):
    t = np.arange(n)
    quarter = HEAD_DIM // 4
    freqs = ROPE_THETA ** (-np.arange(quarter, dtype=np.float64) / quarter)
    row = (t // GRID_W).astype(np.float64)
    col = (t % GRID_W).astype(np.float64)
    return np.concatenate([row[:, None] * freqs, col[:, None] * freqs], axis=-1)


def _linear_angles(n):
    freqs = ROPE_THETA ** (-np.arange(HALF, dtype=np.float64) / HALF)
    return np.arange(n, dtype=np.float64)[:, None] * freqs


def _t5_bucket(rel):
    half = REL_BUCKETS // 2
    max_exact = half // 2
    ret = (rel > 0).astype(np.int32) * half
    dist = np.abs(rel)
    large = max_exact + (np.log(np.maximum(dist, 1) / max_exact) / np.log(REL_MAX_DIST / max_exact)
                         * (half - max_exact)).astype(np.int32)
    large = np.minimum(large, half - 1)
    return ret + np.where(dist < max_exact, dist, large)


def _rmsnorm_rows(x, gain):
    return x * lax.rsqrt(jnp.mean(x * x, axis=-1, keepdims=True) + EPS) * gain


def _silu(z):
    return z * jax.nn.sigmoid(z)


def _rotate_halves(x, cos_t, sin_t):
    lane = lax.broadcasted_iota(jnp.int32, (1, LANES), 1)
    first_half = (lane % HEAD_DIM) < HALF
    partner = jnp.where(first_half, pltpu.roll(x, LANES - HALF, 1), pltpu.roll(x, HALF, 1))
    return x * cos_t + partner * sin_t


def _split_heads_store(ref, tile):
    ref[0] = tile[:, :HEAD_DIM].astype(ref.dtype)
    ref[1] = tile[:, HEAD_DIM:].astype(ref.dtype)


def _transposed_chunks_store(ref, tile):
    chunk = ref.shape[2]
    tile_t = tile.T
    for c in range(ref.shape[0]):
        ref[c] = tile_t[:, c * chunk:(c + 1) * chunk].astype(ref.dtype)


def _stack_heads_on_rows(q, heads):
    return jnp.concatenate([q[:, h * HEAD_DIM:(h + 1) * HEAD_DIM] for h in range(heads)], axis=0)


def _unstack_transposed(o_t, heads, tokens):
    by_head = jnp.concatenate([o_t[:, h * tokens:(h + 1) * tokens] for h in range(heads)], axis=0)
    return by_head.T


def _with_ones_rows(v_t):
    return jnp.concatenate([v_t, jnp.ones((ONES_ROWS, v_t.shape[1]), v_t.dtype)], axis=0)


def _inproj_ab_kernel(x_ref, g_ref, w_ref, gain_ref, ones_ref, cosa_ref, sina_ref, cosb_ref, sinb_ref,
                      qa_ref, ka_ref, va_ref, sga_ref, qb_ref, kb_ref, vb_ref, sgb_ref):
    xn = _rmsnorm_rows(x_ref[...], g_ref[...]).astype(BF16)

    def proj(lo, hi):
        return jnp.dot(xn, w_ref[:, lo:hi], preferred_element_type=F32)

    cosa, sina = cosa_ref[...], sina_ref[...]
    cosb, sinb = cosb_ref[...], sinb_ref[...]

    def head_norm_rope(tile, gain):
        ss = jnp.dot((tile * tile).astype(BF16), ones_ref[...], preferred_element_type=F32)
        tile = tile * lax.rsqrt(ss * (1.0 / HEAD_DIM) + EPS) * gain
        return _rotate_halves(tile, cosa, sina)

    zq = proj(AB_QA, AB_KA)
    for c in range(A_W // LANES):
        tile = head_norm_rope(zq[:, c * LANES:(c + 1) * LANES], gain_ref[0:1, :])
        qa_ref[:, c * LANES:(c + 1) * LANES] = tile.astype(BF16)
    zkv = proj(AB_KA, AB_GA)
    _split_heads_store(ka_ref, head_norm_rope(zkv[:, :LANES], gain_ref[1:2, :]))
    _transposed_chunks_store(va_ref, zkv[:, LANES:])
    sga_ref[...] = _silu(proj(AB_GA, AB_QB)).astype(BF16)

    zqk = proj(AB_QB, AB_VB)
    for c in range(B_QKW // LANES):
        qb_ref[:, c * LANES:(c + 1) * LANES] = _rotate_halves(
            zqk[:, c * LANES:(c + 1) * LANES], cosb, sinb).astype(BF16)
        kt = zqk[:, B_QKW + c * LANES:B_QKW + (c + 1) * LANES] * (HEAD_DIM ** -0.5)
        kb_ref[:, c * LANES:(c + 1) * LANES] = _rotate_halves(kt, cosb, sinb).astype(BF16)
    vb_ref[...] = proj(AB_VB, AB_GB).astype(BF16)
    sgb_ref[...] = _silu(proj(AB_GB, AB_IN)).astype(BF16)


def _inproj_ab(x2d, seq, g, w, gain, ones, tabs):
    rows = x2d.shape[0]
    tm = ROW_TILE
    per_seq = seq // tm
    row = lambda i: (i, 0)
    const = lambda i: (0, 0)
    tab = lambda i: (i % per_seq, 0)
    kvrow = lambda i: (0, i, 0)
    vt_chunks = tm // ATT_TK
    out_shape = (
        jax.ShapeDtypeStruct((rows, A_W), BF16), jax.ShapeDtypeStruct((A_KV, rows, HEAD_DIM), BF16),
        jax.ShapeDtypeStruct((rows // ATT_TK, A_KVW, ATT_TK), BF16), jax.ShapeDtypeStruct((rows, A_W), BF16),
        jax.ShapeDtypeStruct((rows, B_QKW), BF16), jax.ShapeDtypeStruct((rows, B_QKW), BF16),
        jax.ShapeDtypeStruct((rows, B_VW), BF16), jax.ShapeDtypeStruct((rows, B_VW), BF16),
    )
    out_specs = (
        pl.BlockSpec((tm, A_W), row), pl.BlockSpec((A_KV, tm, HEAD_DIM), kvrow),
        pl.BlockSpec((vt_chunks, A_KVW, ATT_TK), lambda i: (i, 0, 0)), pl.BlockSpec((tm, A_W), row),
        pl.BlockSpec((tm, B_QKW), row), pl.BlockSpec((tm, B_QKW), row),
        pl.BlockSpec((tm, B_VW), row), pl.BlockSpec((tm, B_VW), row),
    )
    in_specs = [
        pl.BlockSpec((tm, D_MODEL), row), pl.BlockSpec((1, D_MODEL), const),
        pl.BlockSpec((D_MODEL, AB_IN), const), pl.BlockSpec((2, LANES), const),
        pl.BlockSpec((LANES, LANES), const),
    ] + [pl.BlockSpec((tm, LANES), tab)] * 4
    return pl.pallas_call(
        _inproj_ab_kernel, grid=(rows // tm,), in_specs=in_specs, out_specs=out_specs,
        out_shape=out_shape, compiler_params=_params(("parallel",)), name="inproj_ab",
    )(x2d, g, w, gain, ones, *tabs)


def _dense_attn_kernel(q_ref, k_ref, vt_ref, sg_ref, o_ref, s_buf, p_buf, *, seq):
    tq, tk = ATT_TQ, ATT_TK
    rows = A_GROUP * tq
    n = seq // tk
    blocks = seq // tq
    assert n % 2 == 0 and n >= 2

    def query_rows(i):
        return pl.ds(pl.multiple_of(i * tq, tq), tq)

    def stacked_q(i):
        return _stack_heads_on_rows(q_ref[query_rows(i), :], A_GROUP)

    def scores(qs, c):
        k = k_ref[c * tk:(c + 1) * tk, :]
        s_buf[c % 2] = lax.dot_general(k, qs, (((1,), (1,)), ((), ())), preferred_element_type=F32)

    def softmax(m, c):
        s_t = s_buf[c % 2]
        m_new = jnp.maximum(m, jnp.max(s_t, axis=0, keepdims=True))
        p_buf[c % 2] = jnp.exp2(s_t - m_new).astype(BF16)
        return m_new, jnp.exp2(m - m_new)

    fresh_m = jnp.full((1, rows), NEG, F32)
    first_q = stacked_q(0)
    scores(first_q, 0)
    scores(first_q, 1)

    def block(i, carry):
        m, alpha = carry
        qs = stacked_q(i)
        qs_next = stacked_q(jnp.minimum(i + 1, blocks - 1))
        acc = jnp.zeros((HEAD_DIM + ONES_ROWS, rows), F32)
        for t in range(n):
            if t + 2 < n:
                scores(qs, t + 2)
            else:
                scores(qs_next, t + 2 - n)
            if t + 1 < n:
                m, alpha_next = softmax(m, t + 1)
            else:
                m, alpha_next = softmax(fresh_m, 0)
            pv = jnp.dot(_with_ones_rows(vt_ref[t]), p_buf[t % 2], preferred_element_type=F32)
            acc = alpha * acc + pv
            alpha = alpha_next
        o_t = acc[:HEAD_DIM] / acc[HEAD_DIM:HEAD_DIM + 1]
        o = _unstack_transposed(o_t, A_GROUP, tq)
        o_ref[query_rows(i), :] = (o * sg_ref[query_rows(i), :].astype(F32)).astype(BF16)
        return m, alpha

    lax.fori_loop(0, blocks, block, softmax(fresh_m, 0))


def _dense_attn(qa, ka, vat, sga, batch, seq):
    gw = A_GROUP * HEAD_DIM
    q3 = qa.reshape(batch, seq, A_W)
    sg3 = sga.reshape(batch, seq, A_W)
    k4 = ka.reshape(A_KV, batch, seq, HEAD_DIM)
    vt4 = vat.reshape(batch, seq // ATT_TK, A_KVW, ATT_TK)
    qspec = pl.BlockSpec((None, seq, gw), lambda b, g: (b, 0, g))
    kspec = pl.BlockSpec((None, None, seq, HEAD_DIM), lambda b, g: (g, b, 0, 0))
    vtspec = pl.BlockSpec((None, seq // ATT_TK, HEAD_DIM, ATT_TK), lambda b, g: (b, 0, g, 0))
    out = pl.pallas_call(
        functools.partial(_dense_attn_kernel, seq=seq), grid=(batch, A_KV),
        in_specs=[qspec, kspec, vtspec, qspec], out_specs=qspec,
        out_shape=jax.ShapeDtypeStruct((batch, seq, A_W), BF16),
        scratch_shapes=[pltpu.VMEM((2, ATT_TK, A_GROUP * ATT_TQ), F32),
                        pltpu.VMEM((2, ATT_TK, A_GROUP * ATT_TQ), BF16)],
        compiler_params=_params(("parallel", "parallel")), name="dense_attn",
    )(q3, k4, vt4, sg3)
    return out.reshape(batch * seq, A_W)


def _retention_kernel(lg_ref, q_ref, k_ref, v_ref, sg_ref, o_ref, st_ref, *, seq):
    c = CHUNK
    nc = seq // c
    pair = pl.program_id(1)
    lgf = [lg_ref[0, 2 * pair + hh] for hh in range(2)]
    lgb = [lg_ref[1, 2 * pair + hh] for hh in range(2)]
    lane = lax.broadcasted_iota(jnp.int32, (1, LANES), 1)
    head_of_lane = [lane < HEAD_DIM, lane >= HEAD_DIM]
    lane_lgf = jnp.where(head_of_lane[0], lgf[0], lgf[1])
    lane_lgb = jnp.where(head_of_lane[0], lgb[0], lgb[1])
    t = lax.broadcasted_iota(jnp.int32, (c, 1), 0).astype(F32)
    kdec_f = jnp.exp((c - 1.0 - t) * lane_lgf)
    kdec_b = jnp.exp(t * lane_lgb)
    qdec_f = jnp.exp((t + 1.0) * lane_lgf)
    qdec_b = jnp.exp((c - t) * lane_lgb)
    row = lax.broadcasted_iota(jnp.int32, (c, 1), 0)
    row_first = row < HEAD_DIM
    cdec_f = jnp.exp(c * jnp.where(row_first, lgf[0], lgf[1]))
    cdec_b = jnp.exp(c * jnp.where(row_first, lgb[0], lgb[1]))
    diff = (lax.broadcasted_iota(jnp.int32, (c, c), 0) - lax.broadcasted_iota(jnp.int32, (c, c), 1)).astype(F32)
    intra_decay = [jnp.where(diff >= 0, jnp.exp(lgf[hh] * jnp.maximum(diff, 0.0)),
                             jnp.exp(lgb[hh] * jnp.maximum(-diff, 0.0))) for hh in range(2)]

    def chunk_rows(n):
        return pl.ds(pl.multiple_of(n * c, c), c)

    def kv_body(n, carry):
        rows = chunk_rows(n)
        k = k_ref[rows, :].astype(F32)
        kf, kb = k * kdec_f, k * kdec_b
        kvf = jnp.zeros((LANES, B_VAL), F32)
        kvb = jnp.zeros((LANES, B_VAL), F32)
        for hh in range(2):
            v = v_ref[rows, hh * B_VAL:(hh + 1) * B_VAL]
            dn = (((0,), (0,)), ((), ()))
            kvf += lax.dot_general(jnp.where(head_of_lane[hh], kf, 0.0).astype(BF16), v, dn,
                                   preferred_element_type=F32)
            kvb += lax.dot_general(jnp.where(head_of_lane[hh], kb, 0.0).astype(BF16), v, dn,
                                   preferred_element_type=F32)
        st_ref[n, 0:LANES, :] = kvf
        st_ref[n, LANES:2 * LANES, :] = kvb
        return carry

    lax.fori_loop(0, nc, kv_body, 0)

    def scan_fwd(n, state):
        kv = st_ref[n, 0:LANES, :]
        st_ref[n, 0:LANES, :] = state
        return state * cdec_f + kv

    lax.fori_loop(0, nc, scan_fwd, jnp.zeros((LANES, B_VAL), F32))

    def scan_bwd(i, state):
        n = nc - 1 - i
        kv = st_ref[n, LANES:2 * LANES, :]
        st_ref[n, LANES:2 * LANES, :] = state
        return state * cdec_b + kv

    lax.fori_loop(0, nc, scan_bwd, jnp.zeros((LANES, B_VAL), F32))

    def out_body(n, carry):
        rows = chunk_rows(n)
        q = q_ref[rows, :].astype(F32)
        k = k_ref[rows, :]
        qf, qb = q * qdec_f, q * qdec_b
        state = st_ref[n].astype(BF16)
        for hh in range(2):
            v = v_ref[rows, hh * B_VAL:(hh + 1) * B_VAL]
            qm = jnp.where(head_of_lane[hh], q, 0.0).astype(BF16)
            s = lax.dot_general(qm, k, (((1,), (1,)), ((), ())), preferred_element_type=F32)
            intra = jnp.dot((s * intra_decay[hh]).astype(BF16), v, preferred_element_type=F32)
            lhs = jnp.concatenate([jnp.where(head_of_lane[hh], qf, 0.0),
                                   jnp.where(head_of_lane[hh], qb, 0.0)], axis=1).astype(BF16)
            o = intra + jnp.dot(lhs, state, preferred_element_type=F32)
            o = o * lax.rsqrt(jnp.mean(o * o, axis=-1, keepdims=True) + EPS)
            o = o * sg_ref[rows, hh * B_VAL:(hh + 1) * B_VAL].astype(F32)
            o_ref[rows, hh * B_VAL:(hh + 1) * B_VAL] = o.astype(BF16)
        return carry

    lax.fori_loop(0, nc, out_body, 0)


def _retention(log_gamma, qb, kb, vb, sgb, batch, seq):
    pairs = B_HEADS // 2
    qk_spec = pl.BlockSpec((None, seq, LANES), lambda b, j: (b, 0, j))
    v_spec = pl.BlockSpec((None, seq, 2 * B_VAL), lambda b, j: (b, 0, j))
    out = pl.pallas_call(
        functools.partial(_retention_kernel, seq=seq), grid=(batch, pairs),
        in_specs=[pl.BlockSpec(memory_space=pltpu.SMEM), qk_spec, qk_spec, v_spec, v_spec],
        out_specs=v_spec, out_shape=jax.ShapeDtypeStruct((batch, seq, B_VW), BF16),
        scratch_shapes=[pltpu.VMEM((seq // CHUNK, 2 * LANES, B_VAL), F32)],
        compiler_params=_params(("parallel", "parallel")), name="retention",
    )(log_gamma, qb.reshape(batch, seq, B_QKW), kb.reshape(batch, seq, B_QKW),
      vb.reshape(batch, seq, B_VW), sgb.reshape(batch, seq, B_VW))
    return out.reshape(batch * seq, B_VW)


def _mid_proj_kernel(x_ref, ma_ref, mb_ref, wo_ref, g_ref, wc_ref, x1_ref, q_ref, k_ref, v_ref, sg_ref):
    y = jnp.dot(ma_ref[...], wo_ref[0:A_W, :], preferred_element_type=F32)
    y += jnp.dot(mb_ref[...], wo_ref[A_W:A_W + B_VW, :], preferred_element_type=F32)
    x1 = x_ref[...] + y
    x1_ref[...] = x1
    xn = _rmsnorm_rows(x1, g_ref[...]).astype(BF16)

    def proj(lo, hi):
        return jnp.dot(xn, wc_ref[:, lo:hi], preferred_element_type=F32)

    q_ref[...] = (proj(C_Q, C_K) * (HEAD_DIM ** -0.5 * LOG2E)).astype(BF16)
    zkv = proj(C_K, C_G)
    _split_heads_store(k_ref, zkv[:, :LANES])
    _transposed_chunks_store(v_ref, zkv[:, LANES:])
    sg_ref[...] = _silu(proj(C_G, C_IN)).astype(BF16)


def _mid_proj(x2d, ma, mb, wo, g, wc):
    rows = x2d.shape[0]
    tm = ROW_TILE
    row = lambda i: (i, 0)
    const = lambda i: (0, 0)
    kvrow = lambda i: (0, i, 0)
    out_shape = (
        jax.ShapeDtypeStruct((rows, D_MODEL), F32), jax.ShapeDtypeStruct((rows, C_W), BF16),
        jax.ShapeDtypeStruct((C_KV, rows, HEAD_DIM), BF16), jax.ShapeDtypeStruct((rows // BLOCK, C_KVW, BLOCK), BF16),
        jax.ShapeDtypeStruct((rows, C_W), BF16),
    )
    out_specs = (
        pl.BlockSpec((tm, D_MODEL), row), pl.BlockSpec((tm, C_W), row),
        pl.BlockSpec((C_KV, tm, HEAD_DIM), kvrow), pl.BlockSpec((tm // BLOCK, C_KVW, BLOCK), lambda i: (i, 0, 0)),
        pl.BlockSpec((tm, C_W), row),
    )
    in_specs = [
        pl.BlockSpec((tm, D_MODEL), row), pl.BlockSpec((tm, A_W), row), pl.BlockSpec((tm, B_VW), row),
        pl.BlockSpec((A_W + B_VW, D_MODEL), const), pl.BlockSpec((1, D_MODEL), const),
        pl.BlockSpec((D_MODEL, C_IN), const),
    ]
    return pl.pallas_call(
        _mid_proj_kernel, grid=(rows // tm,), in_specs=in_specs, out_specs=out_specs,
        out_shape=out_shape, compiler_params=_params(("parallel",)), name="mid_proj",
    )(x2d, ma, mb, wo, g, wc)


def _win_attn_kernel(q_ref, kp_ref, kc_ref, kn_ref, vp_ref, vc_ref, vn_ref, bias_ref, sink_ref, sg_ref, o_ref):
    i = pl.program_id(1)
    nb = pl.num_programs(1)
    key_row = lax.broadcasted_iota(jnp.int32, (3 * BLOCK, 1), 0)
    outside = ((key_row < BLOCK) & (i == 0)) | ((key_row >= 2 * BLOCK) & (i == nb - 1))
    edge = jnp.where(outside, NEG, 0.0)
    gw = C_GROUP * HEAD_DIM
    for g in range(C_KV):
        qs = _stack_heads_on_rows(q_ref[:, g * gw:(g + 1) * gw], C_GROUP)
        kw = jnp.concatenate([kp_ref[g], kc_ref[g], kn_ref[g]], axis=0)
        heads = slice(g * HEAD_DIM, (g + 1) * HEAD_DIM)
        vw_t = jnp.concatenate([vp_ref[0, heads, :], vc_ref[0, heads, :], vn_ref[0, heads, :]], axis=1)
        s_t = lax.dot_general(kw, qs, (((1,), (1,)), ((), ())), preferred_element_type=F32)
        s_t = s_t + bias_ref[g] + edge
        sink = sink_ref[g]
        mx = jnp.maximum(jnp.max(s_t, axis=0, keepdims=True), sink)
        p_t = jnp.exp2(s_t - mx).astype(BF16)
        acc = jnp.dot(_with_ones_rows(vw_t), p_t, preferred_element_type=F32)
        denom = acc[HEAD_DIM:HEAD_DIM + 1] + jnp.exp2(sink - mx)
        o = _unstack_transposed(acc[:HEAD_DIM] / denom, C_GROUP, BLOCK)
        o_ref[:, g * gw:(g + 1) * gw] = (o * sg_ref[:, g * gw:(g + 1) * gw].astype(F32)).astype(BF16)


def _win_attn(q, k, vt, bias_t, sink, sg, batch, seq):
    nb = seq // BLOCK
    q3 = q.reshape(batch, seq, C_W)
    sg3 = sg.reshape(batch, seq, C_W)
    k4 = k.reshape(C_KV, batch, seq, HEAD_DIM)
    qspec = pl.BlockSpec((None, BLOCK, C_W), lambda b, i: (b, i, 0))
    k_block = (C_KV, None, BLOCK, HEAD_DIM)
    vt_block = (1, C_KVW, BLOCK)
    before = lambda i: jnp.maximum(i - 1, 0)
    after = lambda i: jnp.minimum(i + 1, nb - 1)
    const3 = lambda b, i: (0, 0, 0)
    out = pl.pallas_call(
        _win_attn_kernel, grid=(batch, nb),
        in_specs=[qspec,
                  pl.BlockSpec(k_block, lambda b, i: (0, b, before(i), 0)),
                  pl.BlockSpec(k_block, lambda b, i: (0, b, i, 0)),
                  pl.BlockSpec(k_block, lambda b, i: (0, b, after(i), 0)),
                  pl.BlockSpec(vt_block, lambda b, i: (b * nb + before(i), 0, 0)),
                  pl.BlockSpec(vt_block, lambda b, i: (b * nb + i, 0, 0)),
                  pl.BlockSpec(vt_block, lambda b, i: (b * nb + after(i), 0, 0)),
                  pl.BlockSpec((C_KV, 3 * BLOCK, C_GROUP * BLOCK), const3),
                  pl.BlockSpec((C_KV, 1, C_GROUP * BLOCK), const3), qspec],
        out_specs=qspec, out_shape=jax.ShapeDtypeStruct((batch, seq, C_W), BF16),
        compiler_params=_params(("parallel", "parallel")), name="win_attn",
    )(q3, k4, k4, k4, vt, vt, vt, bias_t, sink, sg3)
    return out.reshape(batch * seq, C_W)


def _out_proj_kernel(x_ref, m_ref, wo_ref, g_ref, y_ref):
    x2 = x_ref[...] + jnp.dot(m_ref[...], wo_ref[...], preferred_element_type=F32)
    y_ref[...] = _rmsnorm_rows(x2, g_ref[...])


def _out_proj(x2d, m, wo, g):
    rows = x2d.shape[0]
    tm = ROW_TILE
    row = lambda i: (i, 0)
    const = lambda i: (0, 0)
    return pl.pallas_call(
        _out_proj_kernel, grid=(rows // tm,),
        in_specs=[pl.BlockSpec((tm, D_MODEL), row), pl.BlockSpec((tm, C_W), row),
                  pl.BlockSpec((C_W, D_MODEL), const), pl.BlockSpec((1, D_MODEL), const)],
        out_specs=pl.BlockSpec((tm, D_MODEL), row), out_shape=jax.ShapeDtypeStruct((rows, D_MODEL), F32),
        compiler_params=_params(("parallel",)), name="out_proj",
    )(x2d, m, wo, g)


def _trunk(x, prm):
    batch, seq, _ = x.shape
    x2d = x.reshape(batch * seq, D_MODEL)
    qa, ka, va, sga, qb, kb, vb, sgb = _inproj_ab(
        x2d, seq, prm["g0"], prm["w_in_ab"], prm["gain"], prm["ones"], prm["tabs"])
    ma = _dense_attn(qa, ka, va, sga, batch, seq)
    mb = _retention(prm["log_gamma"], qb, kb, vb, sgb, batch, seq)
    x1, qc, kc, vc, sgc = _mid_proj(x2d, ma, mb, prm["w_out_ab"], prm["g1"], prm["w_in_c"])
    mc = _win_attn(qc, kc, vc, prm["bias"], prm["sink"], sgc, batch, seq)
    y = _out_proj(x1, mc, prm["w_out_c"], prm["gf"])
    return y.reshape(batch, seq, D_MODEL)


def kernel(x_prompt, x_sample, norm_g, w_in_ab, qk_norm_a, ret_decay, w_out_ab, w_in_c, sink_c, w_out_c,
           rel_bias, final_norm):
    assert w_in_ab.shape[0] == 1 and w_in_c.shape[0] == 1, "two-layer trunk: one layer of each kind"
    max_seq = max(x_prompt.shape[1], x_sample.shape[1])

    def deinterleaved(w_cols):
        d_in, width = w_cols.shape
        pairs = w_cols.reshape(d_in, width // HEAD_DIM, HALF, 2)
        return pairs.transpose(0, 1, 3, 2).reshape(d_in, width)

    w_ab = w_in_ab[0].astype(BF16)
    w_ab = jnp.concatenate([
        deinterleaved(w_ab[:, AB_QA:AB_KA]), deinterleaved(w_ab[:, AB_KA:AB_VA]), w_ab[:, AB_VA:AB_QB],
        deinterleaved(w_ab[:, AB_QB:AB_KB]), deinterleaved(w_ab[:, AB_KB:AB_VB]), w_ab[:, AB_VB:]], axis=1)
    head_perm = _deinterleave_perm(HEAD_DIM)
    gain = jnp.stack([jnp.tile(qk_norm_a[0, 0][head_perm], 2) * (HEAD_DIM ** -0.5 * LOG2E),
                      jnp.tile(qk_norm_a[0, 1][head_perm], 2)]).astype(F32)
    ones = np.kron(np.eye(2), np.ones((HEAD_DIM, HEAD_DIM)))

    rel = np.arange(3 * BLOCK)[:, None] - BLOCK - np.arange(BLOCK)[None, :]
    bucket_onehot = jax.nn.one_hot(jnp.asarray(_t5_bucket(rel)), REL_BUCKETS, dtype=F32)
    bias = jnp.einsum("kqb,bh->kqh", bucket_onehot, rel_bias.astype(F32) * LOG2E,
                      precision=lax.Precision.HIGHEST)
    bias = jnp.where(jnp.asarray(np.abs(rel) <= WINDOW)[:, :, None], bias, NEG)
    bias = bias.reshape(3 * BLOCK, BLOCK, C_KV, C_GROUP).transpose(2, 0, 3, 1)
    bias = bias.reshape(C_KV, 3 * BLOCK, C_GROUP * BLOCK)
    sink = jnp.repeat(sink_c[0].astype(F32) * LOG2E, BLOCK).reshape(C_KV, 1, C_GROUP * BLOCK)

    prm = {
        "g0": norm_g[0].reshape(1, D_MODEL), "g1": norm_g[1].reshape(1, D_MODEL),
        "gf": final_norm.reshape(1, D_MODEL),
        "w_in_ab": w_ab, "w_out_ab": w_out_ab[0].astype(BF16),
        "w_in_c": w_in_c[0].astype(BF16), "w_out_c": w_out_c[0].astype(BF16),
        "gain": gain, "ones": jnp.asarray(ones, BF16),
        "tabs": _rope_tables(_axial_angles(max_seq)) + _rope_tables(_linear_angles(max_seq)),
        "log_gamma": -jnp.exp(ret_decay[0].astype(F32)),
        "bias": bias, "sink": sink,
    }
    return _trunk(x_prompt, prm), _trunk(x_sample, prm)
```

```python
import functools

import jax
import jax.numpy as jnp
import numpy as np
from jax import lax
from jax.experimental import pallas as pl
from jax.experimental.pallas import tpu as pltpu

F32 = jnp.float32
BF16 = jnp.bfloat16

D_MODEL = 1024
GRID_W = 64
EPS = 1e-6
ROPE_THETA = 10000.0
HEAD_DIM = 64
HALF = HEAD_DIM // 2
LANES = 128

A_HEADS, A_KV = 8, 2
A_GROUP = A_HEADS // A_KV
A_W, A_KVW = A_HEADS * HEAD_DIM, A_KV * HEAD_DIM
B_HEADS, B_VAL = 4, 128
B_QKW, B_VW = B_HEADS * HEAD_DIM, B_HEADS * B_VAL
C_HEADS, C_KV = 16, 2
C_GROUP = C_HEADS // C_KV
C_W, C_KVW = C_HEADS * HEAD_DIM, C_KV * HEAD_DIM
BLOCK = 128
WINDOW = 128
REL_BUCKETS, REL_MAX_DIST = 32, 128
CHUNK = 128
RET_UNROLL = 8
NEG = -1e30
LOG2E = float(np.log2(np.e))
ONES_ROWS = 16

AB_QA, AB_KA, AB_VA, AB_GA = 0, A_W, A_W + A_KVW, A_W + 2 * A_KVW
AB_QB = AB_GA + A_W
AB_KB, AB_VB, AB_GB = AB_QB + B_QKW, AB_QB + 2 * B_QKW, AB_QB + 2 * B_QKW + B_VW
AB_IN = AB_GB + B_VW
C_Q, C_K, C_V, C_G = 0, C_W, C_W + C_KVW, C_W + 2 * C_KVW
C_IN = C_G + C_W

ROW_TILE = 512
ATT_TQ = 128
ATT_TK = 512
ATT_BODY_CHUNKS = 16
WIN_SUPER = 2048
WIN_UNROLL = 4
VMEM_LIMIT = 48 * 1024 * 1024


def _params(sem):
    return pltpu.CompilerParams(dimension_semantics=sem, vmem_limit_bytes=VMEM_LIMIT)


def _deinterleave_perm(width):
    heads = width // HEAD_DIM
    within = np.concatenate([np.arange(0, HEAD_DIM, 2), np.arange(1, HEAD_DIM, 2)])
    return (np.arange(heads)[:, None] * HEAD_DIM + within[None, :]).reshape(-1)


def _rope_tables(ang):
    c, s = np.cos(ang), np.sin(ang)
    cos_t = np.concatenate([c, c, c, c], axis=-1)
    sin_t = np.concatenate([-s, s, -s, s], axis=-1)
    return jnp.asarray(cos_t, F32), jnp.asarray(sin_t, F32)


def _axial_angles(n):
    t = np.arange(n)
    quarter = HEAD_DIM // 4
    freqs = ROPE_THETA ** (-np.arange(quarter, dtype=np.float64) / quarter)
    row = (t // GRID_W).astype(np.float64)
    col = (t % GRID_W).astype(np.float64)
    return np.concatenate([row[:, None] * freqs, col[:, None] * freqs], axis=-1)


def _linear_angles(n):
    freqs = ROPE_THETA ** (-np.arange(HALF, dtype=np.float64) / HALF)
    return np.arange(n, dtype=np.float64)[:, None] * freqs


def _t5_bucket(rel):
    half = REL_BUCKETS // 2
    max_exact = half // 2
    ret = (rel > 0).astype(np.int32) * half
    dist = np.abs(rel)
    large = max_exact + (np.log(np.maximum(dist, 1) / max_exact) / np.log(REL_MAX_DIST / max_exact)
                         * (half - max_exact)).astype(np.int32)
    large = np.minimum(large, half - 1)
    return ret + np.where(dist < max_exact, dist, large)


def _rmsnorm_rows(x, gain):
    return x * lax.rsqrt(jnp.mean(x * x, axis=-1, keepdims=True) + EPS) * gain


def _silu(z):
    return z * jax.nn.sigmoid(z)


def _rotate_halves(x, cos_t, sin_t):
    lane = lax.broadcasted_iota(jnp.int32, (1, LANES), 1)
    first_half = (lane % HEAD_DIM) < HALF
    partner = jnp.where(first_half, pltpu.roll(x, LANES - HALF, 1), pltpu.roll(x, HALF, 1))
    return x * cos_t + partner * sin_t


def _split_heads_store(ref, tile):
    ref[0] = tile[:, :HEAD_DIM].astype(ref.dtype)
    ref[1] = tile[:, HEAD_DIM:].astype(ref.dtype)


def _transposed_chunks_store(ref, tile):
    chunk = ref.shape[2]
    tile_t = tile.T
    for c in range(ref.shape[0]):
        ref[c] = tile_t[:, c * chunk:(c + 1) * chunk].astype(ref.dtype)


def _stack_heads_on_rows(q, heads):
    return jnp.concatenate([q[:, h * HEAD_DIM:(h + 1) * HEAD_DIM] for h in range(heads)], axis=0)


def _unstack_transposed(o_t, heads, tokens):
    by_head = jnp.concatenate([o_t[:, h * tokens:(h + 1) * tokens] for h in range(heads)], axis=0)
    return by_head.T


def _with_ones_rows(v_t):
    return jnp.concatenate([v_t, jnp.ones((ONES_ROWS, v_t.shape[1]), v_t.dtype)], axis=0)


def _inproj_ab_kernel(x_ref, g_ref, w_ref, gain_ref, ones_ref, cosa_ref, sina_ref, cosb_ref, sinb_ref,
                      qa_ref, ka_ref, va_ref, sga_ref, qb_ref, kb_ref, vb_ref, sgb_ref):
    xn = _rmsnorm_rows(x_ref[...], g_ref[...]).astype(BF16)

    def proj(lo, hi):
        return jnp.dot(xn, w_ref[:, lo:hi], preferred_element_type=F32)

    cosa, sina = cosa_ref[...], sina_ref[...]
    cosb, sinb = cosb_ref[...], sinb_ref[...]

    def head_norm_rope(tile, gain):
        ss = jnp.dot((tile * tile).astype(BF16), ones_ref[...], preferred_element_type=F32)
        tile = tile * lax.rsqrt(ss * (1.0 / HEAD_DIM) + EPS) * gain
        return _rotate_halves(tile, cosa, sina)

    zq = proj(AB_QA, AB_KA)
    for c in range(A_W // LANES):
        tile = head_norm_rope(zq[:, c * LANES:(c + 1) * LANES], gain_ref[0:1, :])
        qa_ref[:, c * LANES:(c + 1) * LANES] = tile.astype(BF16)
    zkv = proj(AB_KA, AB_GA)
    _split_heads_store(ka_ref, head_norm_rope(zkv[:, :LANES], gain_ref[1:2, :]))
    _transposed_chunks_store(va_ref, zkv[:, LANES:])
    sga_ref[...] = _silu(proj(AB_GA, AB_QB)).astype(BF16)

    zqk = proj(AB_QB, AB_VB)
    for c in range(B_QKW // LANES):
        qb_ref[:, c * LANES:(c + 1) * LANES] = _rotate_halves(
            zqk[:, c * LANES:(c + 1) * LANES], cosb, sinb).astype(BF16)
        kt = zqk[:, B_QKW + c * LANES:B_QKW + (c + 1) * LANES] * (HEAD_DIM ** -0.5)
        kb_ref[:, c * LANES:(c + 1) * LANES] = _rotate_halves(kt, cosb, sinb).astype(BF16)
    vb_ref[...] = proj(AB_VB, AB_GB).astype(BF16)
    sgb_ref[...] = _silu(proj(AB_GB, AB_IN)).astype(BF16)


def _inproj_ab(x2d, seq, g, w, gain, ones, tabs):
    rows = x2d.shape[0]
    tm = ROW_TILE
    per_seq = seq // tm
    row = lambda i: (i, 0)
    const = lambda i: (0, 0)
    tab = lambda i: (i % per_seq, 0)
    kvrow = lambda i: (0, i, 0)
    vt_chunks = tm // ATT_TK
    out_shape = (
        jax.ShapeDtypeStruct((rows, A_W), BF16), jax.ShapeDtypeStruct((A_KV, rows, HEAD_DIM), BF16),
        jax.ShapeDtypeStruct((rows // ATT_TK, A_KVW, ATT_TK), BF16), jax.ShapeDtypeStruct((rows, A_W), BF16),
        jax.ShapeDtypeStruct((rows, B_QKW), BF16), jax.ShapeDtypeStruct((rows, B_QKW), BF16),
        jax.ShapeDtypeStruct((rows, B_VW), BF16), jax.ShapeDtypeStruct((rows, B_VW), BF16),
    )
    out_specs = (
        pl.BlockSpec((tm, A_W), row), pl.BlockSpec((A_KV, tm, HEAD_DIM), kvrow),
        pl.BlockSpec((vt_chunks, A_KVW, ATT_TK), lambda i: (i, 0, 0)), pl.BlockSpec((tm, A_W), row),
        pl.BlockSpec((tm, B_QKW), row), pl.BlockSpec((tm, B_QKW), row),
        pl.BlockSpec((tm, B_VW), row), pl.BlockSpec((tm, B_VW), row),
    )
    in_specs = [
        pl.BlockSpec((tm, D_MODEL), row), pl.BlockSpec((1, D_MODEL), const),
        pl.BlockSpec((D_MODEL, AB_IN), const), pl.BlockSpec((2, LANES), const),
        pl.BlockSpec((LANES, LANES), const),
    ] + [pl.BlockSpec((tm, LANES), tab)] * 4
    return pl.pallas_call(
        _inproj_ab_kernel, grid=(rows // tm,), in_specs=in_specs, out_specs=out_specs,
        out_shape=out_shape, compiler_params=_params(("parallel",)), name="inproj_ab",
    )(x2d, g, w, gain, ones, *tabs)


def _dense_attn_kernel(q_ref, k_ref, vt_ref, sg_ref, o_ref, s_buf, p_buf, *, seq):
    tq, tk = ATT_TQ, ATT_TK
    rows = A_GROUP * tq
    n = seq // tk
    blocks = seq // tq
    assert n % 2 == 0 and n >= 2

    def query_rows(i):
        return pl.ds(pl.multiple_of(i * tq, tq), tq)

    def stacked_q(i):
        return _stack_heads_on_rows(q_ref[query_rows(i), :], A_GROUP)

    def scores(qs, c):
        k = k_ref[c * tk:(c + 1) * tk, :]
        s_buf[c % 2] = lax.dot_general(k, qs, (((1,), (1,)), ((), ())), preferred_element_type=F32)

    def softmax(m, c):
        s_t = s_buf[c % 2]
        m_new = jnp.maximum(m, jnp.max(s_t, axis=0, keepdims=True))
        p_buf[c % 2] = jnp.exp2(s_t - m_new).astype(BF16)
        return m_new, jnp.exp2(m - m_new)

    fresh_m = jnp.full((1, rows), NEG, F32)
    first_q = stacked_q(0)
    scores(first_q, 0)
    scores(first_q, 1)

    def block(i, m, alpha):
        qs = stacked_q(i)
        qs_next = stacked_q(jnp.minimum(i + 1, blocks - 1))
        acc = jnp.zeros((HEAD_DIM + ONES_ROWS, rows), F32)
        for t in range(n):
            if t + 2 < n:
                scores(qs, t + 2)
            else:
                scores(qs_next, t + 2 - n)
            if t + 1 < n:
                m, alpha_next = softmax(m, t + 1)
            else:
                m, alpha_next = softmax(fresh_m, 0)
            pv = jnp.dot(_with_ones_rows(vt_ref[t]), p_buf[t % 2], preferred_element_type=F32)
            acc = alpha * acc + pv
            alpha = alpha_next
        o_t = acc[:HEAD_DIM] / acc[HEAD_DIM:HEAD_DIM + 1]
        o = _unstack_transposed(o_t, A_GROUP, tq)
        o_ref[query_rows(i), :] = (o * sg_ref[query_rows(i), :].astype(F32)).astype(BF16)
        return m, alpha

    per_body = max(1, ATT_BODY_CHUNKS // n)

    def body(step, carry):
        for u in range(per_body):
            carry = block(step * per_body + u, *carry)
        return carry

    lax.fori_loop(0, blocks // per_body, body, softmax(fresh_m, 0))


def _dense_attn(qa, ka, vat, sga, batch, seq):
    gw = A_GROUP * HEAD_DIM
    q3 = qa.reshape(batch, seq, A_W)
    sg3 = sga.reshape(batch, seq, A_W)
    k4 = ka.reshape(A_KV, batch, seq, HEAD_DIM)
    vt4 = vat.reshape(batch, seq // ATT_TK, A_KVW, ATT_TK)
    qspec = pl.BlockSpec((None, seq, gw), lambda b, g: (b, 0, g))
    kspec = pl.BlockSpec((None, None, seq, HEAD_DIM), lambda b, g: (g, b, 0, 0))
    vtspec = pl.BlockSpec((None, seq // ATT_TK, HEAD_DIM, ATT_TK), lambda b, g: (b, 0, g, 0))
    out = pl.pallas_call(
        functools.partial(_dense_attn_kernel, seq=seq), grid=(batch, A_KV),
        in_specs=[qspec, kspec, vtspec, qspec], out_specs=qspec,
        out_shape=jax.ShapeDtypeStruct((batch, seq, A_W), BF16),
        scratch_shapes=[pltpu.VMEM((2, ATT_TK, A_GROUP * ATT_TQ), F32),
                        pltpu.VMEM((2, ATT_TK, A_GROUP * ATT_TQ), BF16)],
        compiler_params=_params(("parallel", "parallel")), name="dense_attn",
    )(q3, k4, vt4, sg3)
    return out.reshape(batch * seq, A_W)


def _retention_kernel(lg_ref, q_ref, k_ref, v_ref, sg_ref, o_ref, st_ref, *, seq):
    c = CHUNK
    nc = seq // c
    pair = pl.program_id(1)
    lgf = [lg_ref[0, 2 * pair + hh] for hh in range(2)]
    lgb = [lg_ref[1, 2 * pair + hh] for hh in range(2)]
    lane = lax.broadcasted_iota(jnp.int32, (1, LANES), 1)
    head_of_lane = [lane < HEAD_DIM, lane >= HEAD_DIM]
    lane_lgf = jnp.where(head_of_lane[0], lgf[0], lgf[1])
    lane_lgb = jnp.where(head_of_lane[0], lgb[0], lgb[1])
    t = lax.broadcasted_iota(jnp.int32, (c, 1), 0).astype(F32)
    kdec_f = jnp.exp((c - 1.0 - t) * lane_lgf)
    kdec_b = jnp.exp(t * lane_lgb)
    qdec_f = jnp.exp((t + 1.0) * lane_lgf)
    qdec_b = jnp.exp((c - t) * lane_lgb)
    row = lax.broadcasted_iota(jnp.int32, (c, 1), 0)
    row_first = row < HEAD_DIM
    cdec_f = jnp.exp(c * jnp.where(row_first, lgf[0], lgf[1]))
    cdec_b = jnp.exp(c * jnp.where(row_first, lgb[0], lgb[1]))
    diff = (lax.broadcasted_iota(jnp.int32, (c, c), 0) - lax.broadcasted_iota(jnp.int32, (c, c), 1)).astype(F32)
    intra_decay = [jnp.where(diff >= 0, jnp.exp(lgf[hh] * jnp.maximum(diff, 0.0)),
                             jnp.exp(lgb[hh] * jnp.maximum(-diff, 0.0))) for hh in range(2)]

    def chunk_rows(n):
        return pl.ds(pl.multiple_of(n * c, c), c)

    def kv_body(n, carry):
        rows = chunk_rows(n)
        k = k_ref[rows, :].astype(F32)
        kf, kb = k * kdec_f, k * kdec_b
        kvf = jnp.zeros((LANES, B_VAL), F32)
        kvb = jnp.zeros((LANES, B_VAL), F32)
        for hh in range(2):
            v = v_ref[rows, hh * B_VAL:(hh + 1) * B_VAL]
            dn = (((0,), (0,)), ((), ()))
            kvf += lax.dot_general(jnp.where(head_of_lane[hh], kf, 0.0).astype(BF16), v, dn,
                                   preferred_element_type=F32)
            kvb += lax.dot_general(jnp.where(head_of_lane[hh], kb, 0.0).astype(BF16), v, dn,
                                   preferred_element_type=F32)
        st_ref[n, 0:LANES, :] = kvf
        st_ref[n, LANES:2 * LANES, :] = kvb
        return carry

    lax.fori_loop(0, nc, kv_body, 0, unroll=RET_UNROLL)

    def scan_fwd(n, state):
        kv = st_ref[n, 0:LANES, :]
        st_ref[n, 0:LANES, :] = state
        return state * cdec_f + kv

    lax.fori_loop(0, nc, scan_fwd, jnp.zeros((LANES, B_VAL), F32))

    def scan_bwd(i, state):
        n = nc - 1 - i
        kv = st_ref[n, LANES:2 * LANES, :]
        st_ref[n, LANES:2 * LANES, :] = state
        return state * cdec_b + kv

    lax.fori_loop(0, nc, scan_bwd, jnp.zeros((LANES, B_VAL), F32))

    def out_body(n, carry):
        rows = chunk_rows(n)
        q = q_ref[rows, :].astype(F32)
        k = k_ref[rows, :]
        qf, qb = q * qdec_f, q * qdec_b
        state = st_ref[n].astype(BF16)
        for hh in range(2):
            v = v_ref[rows, hh * B_VAL:(hh + 1) * B_VAL]
            qm = jnp.where(head_of_lane[hh], q, 0.0).astype(BF16)
            s = lax.dot_general(qm, k, (((1,), (1,)), ((), ())), preferred_element_type=F32)
            intra = jnp.dot((s * intra_decay[hh]).astype(BF16), v, preferred_element_type=F32)
            lhs = jnp.concatenate([jnp.where(head_of_lane[hh], qf, 0.0),
                                   jnp.where(head_of_lane[hh], qb, 0.0)], axis=1).astype(BF16)
            o = intra + jnp.dot(lhs, state, preferred_element_type=F32)
            o = o * lax.rsqrt(jnp.mean(o * o, axis=-1, keepdims=True) + EPS)
            o = o * sg_ref[rows, hh * B_VAL:(hh + 1) * B_VAL].astype(F32)
            o_ref[rows, hh * B_VAL:(hh + 1) * B_VAL] = o.astype(BF16)
        return carry

    lax.fori_loop(0, nc, out_body, 0, unroll=RET_UNROLL)


def _retention(log_gamma, qb, kb, vb, sgb, batch, seq):
    pairs = B_HEADS // 2
    qk_spec = pl.BlockSpec((None, seq, LANES), lambda b, j: (b, 0, j))
    v_spec = pl.BlockSpec((None, seq, 2 * B_VAL), lambda b, j: (b, 0, j))
    out = pl.pallas_call(
        functools.partial(_retention_kernel, seq=seq), grid=(batch, pairs),
        in_specs=[pl.BlockSpec(memory_space=pltpu.SMEM), qk_spec, qk_spec, v_spec, v_spec],
        out_specs=v_spec, out_shape=jax.ShapeDtypeStruct((batch, seq, B_VW), BF16),
        scratch_shapes=[pltpu.VMEM((seq // CHUNK, 2 * LANES, B_VAL), F32)],
        compiler_params=_params(("parallel", "parallel")), name="retention",
    )(log_gamma, qb.reshape(batch, seq, B_QKW), kb.reshape(batch, seq, B_QKW),
      vb.reshape(batch, seq, B_VW), sgb.reshape(batch, seq, B_VW))
    return out.reshape(batch * seq, B_VW)


def _mid_proj_kernel(x_ref, ma_ref, mb_ref, wo_ref, g_ref, wc_ref, x1_ref, q_ref, k_ref, v_ref, sg_ref):
    y = jnp.dot(ma_ref[...], wo_ref[0:A_W, :], preferred_element_type=F32)
    y += jnp.dot(mb_ref[...], wo_ref[A_W:A_W + B_VW, :], preferred_element_type=F32)
    x1 = x_ref[...] + y
    x1_ref[...] = x1
    xn = _rmsnorm_rows(x1, g_ref[...]).astype(BF16)

    def proj(lo, hi):
        return jnp.dot(xn, wc_ref[:, lo:hi], preferred_element_type=F32)

    q_ref[...] = (proj(C_Q, C_K) * (HEAD_DIM ** -0.5 * LOG2E)).astype(BF16)
    zkv = proj(C_K, C_G)
    _split_heads_store(k_ref, zkv[:, :LANES])
    _transposed_chunks_store(v_ref, zkv[:, LANES:])
    sg_ref[...] = _silu(proj(C_G, C_IN)).astype(BF16)


def _mid_proj(x2d, ma, mb, wo, g, wc):
    rows = x2d.shape[0]
    tm = ROW_TILE
    row = lambda i: (i, 0)
    const = lambda i: (0, 0)
    kvrow = lambda i: (0, i, 0)
    out_shape = (
        jax.ShapeDtypeStruct((rows, D_MODEL), F32), jax.ShapeDtypeStruct((rows, C_W), BF16),
        jax.ShapeDtypeStruct((C_KV, rows, HEAD_DIM), BF16), jax.ShapeDtypeStruct((rows // BLOCK, C_KVW, BLOCK), BF16),
        jax.ShapeDtypeStruct((rows, C_W), BF16),
    )
    out_specs = (
        pl.BlockSpec((tm, D_MODEL), row), pl.BlockSpec((tm, C_W), row),
        pl.BlockSpec((C_KV, tm, HEAD_DIM), kvrow), pl.BlockSpec((tm // BLOCK, C_KVW, BLOCK), lambda i: (i, 0, 0)),
        pl.BlockSpec((tm, C_W), row),
    )
    in_specs = [
        pl.BlockSpec((tm, D_MODEL), row), pl.BlockSpec((tm, A_W), row), pl.BlockSpec((tm, B_VW), row),
        pl.BlockSpec((A_W + B_VW, D_MODEL), const), pl.BlockSpec((1, D_MODEL), const),
        pl.BlockSpec((D_MODEL, C_IN), const),
    ]
    return pl.pallas_call(
        _mid_proj_kernel, grid=(rows // tm,), in_specs=in_specs, out_specs=out_specs,
        out_shape=out_shape, compiler_params=_params(("parallel",)), name="mid_proj",
    )(x2d, ma, mb, wo, g, wc)


def _win_attn_kernel(q_ref, kp_ref, k_ref, kn_ref, vp_ref, v_ref, vn_ref, bias_ref, sink_ref, sg_ref, o_ref,
                     kw_all, vt_all, s_buf, p_buf, *, blocks_per_seq):
    j = pl.program_id(1)
    nblk = q_ref.shape[0] // BLOCK
    gw = C_GROUP * HEAD_DIM
    for g in range(C_KV):
        kw_all[g, 0:BLOCK] = kp_ref[g]
        kw_all[g, BLOCK:(nblk + 1) * BLOCK] = k_ref[g]
        kw_all[g, (nblk + 1) * BLOCK:(nblk + 2) * BLOCK] = kn_ref[g]
    vt_all[0] = vp_ref[0]
    vt_all[1:nblk + 1] = v_ref[...]
    vt_all[nblk + 1] = vn_ref[0]

    def block_rows(i):
        return pl.ds(pl.multiple_of(i * BLOCK, BLOCK), BLOCK)

    def scores(i, g):
        qs = _stack_heads_on_rows(q_ref[block_rows(i), g * gw:(g + 1) * gw], C_GROUP)
        kw = kw_all[g, pl.ds(pl.multiple_of(i * BLOCK, BLOCK), 3 * BLOCK), :]
        s_buf[g] = lax.dot_general(kw, qs, (((1,), (1,)), ((), ())), preferred_element_type=F32)

    def softmax(i, g):
        block_in_seq = j * nblk + i
        prev_bias = jnp.where(block_in_seq == 0, 3, 0)
        next_bias = jnp.where(block_in_seq == blocks_per_seq - 1, 3, 2)
        s_t = jnp.concatenate([s_buf[g, 0:BLOCK] + bias_ref[prev_bias, g],
                               s_buf[g, BLOCK:2 * BLOCK] + bias_ref[1, g],
                               s_buf[g, 2 * BLOCK:3 * BLOCK] + bias_ref[next_bias, g]], axis=0)
        sink = sink_ref[g]
        mx = jnp.maximum(jnp.max(s_t, axis=0, keepdims=True), sink)
        p_buf[g] = jnp.exp2(s_t - mx).astype(BF16)
        return jnp.exp2(sink - mx)

    def finish(i, g, sink_term):
        heads = slice(g * HEAD_DIM, (g + 1) * HEAD_DIM)
        vw_t = jnp.concatenate([vt_all[i, heads, :], vt_all[i + 1, heads, :], vt_all[i + 2, heads, :]], axis=1)
        acc = jnp.dot(_with_ones_rows(vw_t), p_buf[g], preferred_element_type=F32)
        denom = acc[HEAD_DIM:HEAD_DIM + 1] + sink_term
        o = _unstack_transposed(acc[:HEAD_DIM] / denom, C_GROUP, BLOCK)
        gate = sg_ref[block_rows(i), g * gw:(g + 1) * gw].astype(F32)
        o_ref[block_rows(i), g * gw:(g + 1) * gw] = (o * gate).astype(BF16)

    scores(0, 0)
    scores(0, 1)

    def body(step, sink_term0):
        for u in range(WIN_UNROLL):
            i = step * WIN_UNROLL + u
            nxt = jnp.minimum(i + 1, nblk - 1)
            scores(nxt, 0)
            sink_term1 = softmax(i, 1)
            finish(i, 0, sink_term0)
            scores(nxt, 1)
            sink_term0 = softmax(nxt, 0)
            finish(i, 1, sink_term1)
        return sink_term0

    lax.fori_loop(0, nblk // WIN_UNROLL, body, softmax(0, 0))


def _win_attn(q, k, vt, bias, sink, sg, batch, seq):
    nb = seq // BLOCK
    sup = WIN_SUPER
    nblk = sup // BLOCK
    q3 = q.reshape(batch, seq, C_W)
    sg3 = sg.reshape(batch, seq, C_W)
    k4 = k.reshape(C_KV, batch, seq, HEAD_DIM)
    qspec = pl.BlockSpec((None, sup, C_W), lambda b, j: (b, j, 0))
    before = lambda j: jnp.maximum(j * nblk - 1, 0)
    after = lambda j: jnp.minimum((j + 1) * nblk, nb - 1)
    halo_k = (C_KV, None, BLOCK, HEAD_DIM)
    halo_v = (1, C_KVW, BLOCK)
    out = pl.pallas_call(
        functools.partial(_win_attn_kernel, blocks_per_seq=nb), grid=(batch, seq // sup),
        in_specs=[qspec,
                  pl.BlockSpec(halo_k, lambda b, j: (0, b, before(j), 0)),
                  pl.BlockSpec((C_KV, None, sup, HEAD_DIM), lambda b, j: (0, b, j, 0)),
                  pl.BlockSpec(halo_k, lambda b, j: (0, b, after(j), 0)),
                  pl.BlockSpec(halo_v, lambda b, j: (b * nb + before(j), 0, 0)),
                  pl.BlockSpec((nblk, C_KVW, BLOCK), lambda b, j: (b * (nb // nblk) + j, 0, 0)),
                  pl.BlockSpec(halo_v, lambda b, j: (b * nb + after(j), 0, 0)),
                  pl.BlockSpec((4, C_KV, BLOCK, C_GROUP * BLOCK), lambda b, j: (0, 0, 0, 0)),
                  pl.BlockSpec((C_KV, 1, C_GROUP * BLOCK), lambda b, j: (0, 0, 0)), qspec],
        out_specs=qspec, out_shape=jax.ShapeDtypeStruct((batch, seq, C_W), BF16),
        scratch_shapes=[pltpu.VMEM((C_KV, sup + 2 * BLOCK, HEAD_DIM), BF16),
                        pltpu.VMEM((nblk + 2, C_KVW, BLOCK), BF16),
                        pltpu.VMEM((C_KV, 3 * BLOCK, C_GROUP * BLOCK), F32),
                        pltpu.VMEM((C_KV, 3 * BLOCK, C_GROUP * BLOCK), BF16)],
        compiler_params=_params(("parallel", "parallel")), name="win_attn",
    )(q3, k4, k4, k4, vt, vt, vt, bias, sink, sg3)
    return out.reshape(batch * seq, C_W)


def _out_proj_kernel(x_ref, m_ref, wo_ref, g_ref, y_ref):
    x2 = x_ref[...] + jnp.dot(m_ref[...], wo_ref[...], preferred_element_type=F32)
    y_ref[...] = _rmsnorm_rows(x2, g_ref[...])


def _out_proj(x2d, m, wo, g):
    rows = x2d.shape[0]
    tm = ROW_TILE
    row = lambda i: (i, 0)
    const = lambda i: (0, 0)
    return pl.pallas_call(
        _out_proj_kernel, grid=(rows // tm,),
        in_specs=[pl.BlockSpec((tm, D_MODEL), row), pl.BlockSpec((tm, C_W), row),
                  pl.BlockSpec((C_W, D_MODEL), const), pl.BlockSpec((1, D_MODEL), const)],
        out_specs=pl.BlockSpec((tm, D_MODEL), row), out_shape=jax.ShapeDtypeStruct((rows, D_MODEL), F32),
        compiler_params=_params(("parallel",)), name="out_proj",
    )(x2d, m, wo, g)


def _trunk(x, prm):
    batch, seq, _ = x.shape
    x2d = x.reshape(batch * seq, D_MODEL)
    qa, ka, va, sga, qb, kb, vb, sgb = _inproj_ab(
        x2d, seq, prm["g0"], prm["w_in_ab"], prm["gain"], prm["ones"], prm["tabs"])
    ma = _dense_attn(qa, ka, va, sga, batch, seq)
    mb = _retention(prm["log_gamma"], qb, kb, vb, sgb, batch, seq)
    x1, qc, kc, vc, sgc = _mid_proj(x2d, ma, mb, prm["w_out_ab"], prm["g1"], prm["w_in_c"])
    mc = _win_attn(qc, kc, vc, prm["bias"], prm["sink"], sgc, batch, seq)
    y = _out_proj(x1, mc, prm["w_out_c"], prm["gf"])
    return y.reshape(batch, seq, D_MODEL)


def kernel(x_prompt, x_sample, norm_g, w_in_ab, qk_norm_a, ret_decay, w_out_ab, w_in_c, sink_c, w_out_c,
           rel_bias, final_norm):
    assert w_in_ab.shape[0] == 1 and w_in_c.shape[0] == 1, "two-layer trunk: one layer of each kind"
    max_seq = max(x_prompt.shape[1], x_sample.shape[1])

    def deinterleaved(w_cols):
        d_in, width = w_cols.shape
        pairs = w_cols.reshape(d_in, width // HEAD_DIM, HALF, 2)
        return pairs.transpose(0, 1, 3, 2).reshape(d_in, width)

    w_ab = w_in_ab[0].astype(BF16)
    w_ab = jnp.concatenate([
        deinterleaved(w_ab[:, AB_QA:AB_KA]), deinterleaved(w_ab[:, AB_KA:AB_VA]), w_ab[:, AB_VA:AB_QB],
        deinterleaved(w_ab[:, AB_QB:AB_KB]), deinterleaved(w_ab[:, AB_KB:AB_VB]), w_ab[:, AB_VB:]], axis=1)
    head_perm = _deinterleave_perm(HEAD_DIM)
    gain = jnp.stack([jnp.tile(qk_norm_a[0, 0][head_perm], 2) * (HEAD_DIM ** -0.5 * LOG2E),
                      jnp.tile(qk_norm_a[0, 1][head_perm], 2)]).astype(F32)
    ones = np.kron(np.eye(2), np.ones((HEAD_DIM, HEAD_DIM)))

    rel = np.arange(3 * BLOCK)[:, None] - BLOCK - np.arange(BLOCK)[None, :]
    bucket_onehot = jax.nn.one_hot(jnp.asarray(_t5_bucket(rel)), REL_BUCKETS, dtype=F32)
    bias = jnp.einsum("kqb,bh->kqh", bucket_onehot, rel_bias.astype(F32) * LOG2E,
                      precision=lax.Precision.HIGHEST)
    bias = jnp.where(jnp.asarray(np.abs(rel) <= WINDOW)[:, :, None], bias, NEG)
    bias = bias.reshape(3, BLOCK, BLOCK, C_KV, C_GROUP).transpose(0, 3, 1, 4, 2)
    bias = bias.reshape(3, C_KV, BLOCK, C_GROUP * BLOCK)
    bias = jnp.concatenate([bias, jnp.full((1,) + bias.shape[1:], NEG, F32)], axis=0)
    sink = jnp.repeat(sink_c[0].astype(F32) * LOG2E, BLOCK).reshape(C_KV, 1, C_GROUP * BLOCK)

    prm = {
        "g0": norm_g[0].reshape(1, D_MODEL), "g1": norm_g[1].reshape(1, D_MODEL),
        "gf": final_norm.reshape(1, D_MODEL),
        "w_in_ab": w_ab, "w_out_ab": w_out_ab[0].astype(BF16),
        "w_in_c": w_in_c[0].astype(BF16), "w_out_c": w_out_c[0].astype(BF16),
        "gain": gain, "ones": jnp.asarray(ones, BF16),
        "tabs": _rope_tables(_axial_angles(max_seq)) + _rope_tables(_linear_angles(max_seq)),
        "log_gamma": -jnp.exp(ret_decay[0].astype(F32)),
        "bias": bias, "sink": sink,
    }
    return _trunk(x_prompt, prm), _trunk(x_sample, prm)
```

```python
import functools

import jax
import jax.numpy as jnp
import numpy as np
from jax import lax
from jax.experimental import pallas as pl
from jax.experimental.pallas import tpu as pltpu

F32 = jnp.float32
BF16 = jnp.bfloat16

D_MODEL = 1024
GRID_W = 64
EPS = 1e-6
ROPE_THETA = 10000.0
HEAD_DIM = 64
HALF = HEAD_DIM // 2
LANES = 128

A_HEADS, A_KV = 8, 2
A_GROUP = A_HEADS // A_KV
A_W, A_KVW = A_HEADS * HEAD_DIM, A_KV * HEAD_DIM
B_HEADS, B_VAL = 4, 128
B_QKW, B_VW = B_HEADS * HEAD_DIM, B_HEADS * B_VAL
C_HEADS, C_KV = 16, 2
C_GROUP = C_HEADS // C_KV
C_W, C_KVW = C_HEADS * HEAD_DIM, C_KV * HEAD_DIM
BLOCK = 128
WINDOW = 128
REL_BUCKETS, REL_MAX_DIST = 32, 128
CHUNK = 128
RET_UNROLL = 8
NEG = -1e30
LOG2E = float(np.log2(np.e))
ONES_ROWS = 16

AB_QA, AB_KA, AB_VA, AB_GA = 0, A_W, A_W + A_KVW, A_W + 2 * A_KVW
AB_QB = AB_GA + A_W
AB_KB, AB_VB, AB_GB = AB_QB + B_QKW, AB_QB + 2 * B_QKW, AB_QB + 2 * B_QKW + B_VW
AB_IN = AB_GB + B_VW
C_Q, C_K, C_V, C_G = 0, C_W, C_W + C_KVW, C_W + 2 * C_KVW
C_IN = C_G + C_W

ROW_TILE = 1024
ATT_TQ = 256
ATT_TK = 512
ATT_BODY_CHUNKS = 16
WIN_SUPER = 2048
WIN_UNROLL = 4
VMEM_LIMIT = 48 * 1024 * 1024


def _params(sem, flags=None):
    return pltpu.CompilerParams(dimension_semantics=sem, vmem_limit_bytes=VMEM_LIMIT, flags=flags)


def _deinterleave_perm(width):
    heads = width // HEAD_DIM
    within = np.concatenate([np.arange(0, HEAD_DIM, 2), np.arange(1, HEAD_DIM, 2)])
    return (np.arange(heads)[:, None] * HEAD_DIM + within[None, :]).reshape(-1)


def _rope_tables(ang):
    c, s = np.cos(ang), np.sin(ang)
    cos_t = np.concatenate([c, c, c, c], axis=-1)
    sin_t = np.concatenate([-s, s, -s, s], axis=-1)
    return jnp.asarray(cos_t, F32), jnp.asarray(sin_t, F32)


def _axial_angles(n):
    t = np.arange(n)
    quarter = HEAD_DIM // 4
    freqs = ROPE_THETA ** (-np.arange(quarter, dtype=np.float64) / quarter)
    row = (t // GRID_W).astype(np.float64)
    col = (t % GRID_W).astype(np.float64)
    return np.concatenate([row[:, None] * freqs, col[:, None] * freqs], axis=-1)


def _linear_angles(n):
    freqs = ROPE_THETA ** (-np.arange(HALF, dtype=np.float64) / HALF)
    return np.arange(n, dtype=np.float64)[:, None] * freqs


def _t5_bucket(rel):
    half = REL_BUCKETS // 2
    max_exact = half // 2
    ret = (rel > 0).astype(np.int32) * half
    dist = np.abs(rel)
    large = max_exact + (np.log(np.maximum(dist, 1) / max_exact) / np.log(REL_MAX_DIST / max_exact)
                         * (half - max_exact)).astype(np.int32)
    large = np.minimum(large, half - 1)
    return ret + np.where(dist < max_exact, dist, large)


def _rmsnorm_rows(x, gain):
    return x * lax.rsqrt(jnp.mean(x * x, axis=-1, keepdims=True) + EPS) * gain


def _silu(z):
    return z * jax.nn.sigmoid(z)


def _rotate_halves(x, cos_t, sin_t):
    lane = lax.broadcasted_iota(jnp.int32, (1, LANES), 1)
    first_half = (lane % HEAD_DIM) < HALF
    partner = jnp.where(first_half, pltpu.roll(x, LANES - HALF, 1), pltpu.roll(x, HALF, 1))
    return x * cos_t + partner * sin_t


def _split_heads_store(ref, tile):
    ref[0] = tile[:, :HEAD_DIM].astype(ref.dtype)
    ref[1] = tile[:, HEAD_DIM:].astype(ref.dtype)


def _transposed_chunks_store(ref, tile):
    chunk = ref.shape[2]
    tile_t = tile.T
    for c in range(ref.shape[0]):
        ref[c] = tile_t[:, c * chunk:(c + 1) * chunk].astype(ref.dtype)


def _stack_heads_on_rows(q, heads):
    return jnp.concatenate([q[:, h * HEAD_DIM:(h + 1) * HEAD_DIM] for h in range(heads)], axis=0)


def _unstack_transposed(o_t, heads, tokens):
    by_head = jnp.concatenate([o_t[:, h * tokens:(h + 1) * tokens] for h in range(heads)], axis=0)
    return by_head.T


def _with_ones_rows(v_t):
    return jnp.concatenate([v_t, jnp.ones((ONES_ROWS, v_t.shape[1]), v_t.dtype)], axis=0)


def _inproj_ab_kernel(x_ref, g_ref, w_ref, gain_ref, ones_ref, cosa_ref, sina_ref, cosb_ref, sinb_ref,
                      qa_ref, ka_ref, va_ref, sga_ref, qb_ref, kb_ref, vb_ref, sgb_ref):
    xn = _rmsnorm_rows(x_ref[...], g_ref[...]).astype(BF16)

    def proj(lo, hi):
        return jnp.dot(xn, w_ref[:, lo:hi], preferred_element_type=F32)

    cosa, sina = cosa_ref[...], sina_ref[...]
    cosb, sinb = cosb_ref[...], sinb_ref[...]

    def head_norm_rope(tile, gain):
        ss = jnp.dot((tile * tile).astype(BF16), ones_ref[...], preferred_element_type=F32)
        tile = tile * lax.rsqrt(ss * (1.0 / HEAD_DIM) + EPS) * gain
        return _rotate_halves(tile, cosa, sina)

    zq = proj(AB_QA, AB_KA)
    for c in range(A_W // LANES):
        tile = head_norm_rope(zq[:, c * LANES:(c + 1) * LANES], gain_ref[0:1, :])
        qa_ref[:, c * LANES:(c + 1) * LANES] = tile.astype(BF16)
    zkv = proj(AB_KA, AB_GA)
    _split_heads_store(ka_ref, head_norm_rope(zkv[:, :LANES], gain_ref[1:2, :]))
    _transposed_chunks_store(va_ref, zkv[:, LANES:])
    sga_ref[...] = _silu(proj(AB_GA, AB_QB)).astype(BF16)

    zqk = proj(AB_QB, AB_VB)
    for c in range(B_QKW // LANES):
        qb_ref[:, c * LANES:(c + 1) * LANES] = _rotate_halves(
            zqk[:, c * LANES:(c + 1) * LANES], cosb, sinb).astype(BF16)
        kt = zqk[:, B_QKW + c * LANES:B_QKW + (c + 1) * LANES] * (HEAD_DIM ** -0.5)
        kb_ref[:, c * LANES:(c + 1) * LANES] = _rotate_halves(kt, cosb, sinb).astype(BF16)
    vb_ref[...] = proj(AB_VB, AB_GB).astype(BF16)
    sgb_ref[...] = _silu(proj(AB_GB, AB_IN)).astype(BF16)


def _inproj_ab(x2d, seq, g, w, gain, ones, tabs):
    rows = x2d.shape[0]
    tm = ROW_TILE
    per_seq = seq // tm
    row = lambda i: (i, 0)
    const = lambda i: (0, 0)
    tab = lambda i: (i % per_seq, 0)
    kvrow = lambda i: (0, i, 0)
    vt_chunks = tm // ATT_TK
    out_shape = (
        jax.ShapeDtypeStruct((rows, A_W), BF16), jax.ShapeDtypeStruct((A_KV, rows, HEAD_DIM), BF16),
        jax.ShapeDtypeStruct((rows // ATT_TK, A_KVW, ATT_TK), BF16), jax.ShapeDtypeStruct((rows, A_W), BF16),
        jax.ShapeDtypeStruct((rows, B_QKW), BF16), jax.ShapeDtypeStruct((rows, B_QKW), BF16),
        jax.ShapeDtypeStruct((rows, B_VW), BF16), jax.ShapeDtypeStruct((rows, B_VW), BF16),
    )
    out_specs = (
        pl.BlockSpec((tm, A_W), row), pl.BlockSpec((A_KV, tm, HEAD_DIM), kvrow),
        pl.BlockSpec((vt_chunks, A_KVW, ATT_TK), lambda i: (i, 0, 0)), pl.BlockSpec((tm, A_W), row),
        pl.BlockSpec((tm, B_QKW), row), pl.BlockSpec((tm, B_QKW), row),
        pl.BlockSpec((tm, B_VW), row), pl.BlockSpec((tm, B_VW), row),
    )
    in_specs = [
        pl.BlockSpec((tm, D_MODEL), row), pl.BlockSpec((1, D_MODEL), const),
        pl.BlockSpec((D_MODEL, AB_IN), const), pl.BlockSpec((2, LANES), const),
        pl.BlockSpec((LANES, LANES), const),
    ] + [pl.BlockSpec((tm, LANES), tab)] * 4
    return pl.pallas_call(
        _inproj_ab_kernel, grid=(rows // tm,), in_specs=in_specs, out_specs=out_specs,
        out_shape=out_shape, compiler_params=_params(("parallel",)), name="inproj_ab",
    )(x2d, g, w, gain, ones, *tabs)


def _dense_attn_kernel(q_ref, k_ref, vt_ref, sg_ref, o_ref, s_buf, p_buf, *, seq):
    tq, tk = ATT_TQ, ATT_TK
    rows = A_GROUP * tq
    n = seq // tk
    blocks = seq // tq
    assert n % 2 == 0 and n >= 2

    def query_rows(i):
        return pl.ds(pl.multiple_of(i * tq, tq), tq)

    def stacked_q(i):
        return _stack_heads_on_rows(q_ref[query_rows(i), :], A_GROUP)

    def scores(qs, c):
        k = k_ref[c * tk:(c + 1) * tk, :]
        s_t = lax.dot_general(k, qs, (((1,), (1,)), ((), ())), preferred_element_type=F32)
        s_buf[c % 2] = s_t
        return jnp.max(s_t, axis=0, keepdims=True)

    def softmax(m, c, chunk_max):
        m_new = jnp.maximum(m, chunk_max)
        p_buf[c % 2] = jnp.exp2(s_buf[c % 2] - m_new).astype(BF16)
        return m_new, jnp.exp2(m - m_new)

    fresh_m = jnp.full((1, rows), NEG, F32)
    first_q = stacked_q(0)
    first_max = [scores(first_q, 0), scores(first_q, 1)]

    def block(i, m, alpha, max0, max1):
        chunk_max = [max0, max1]
        qs = stacked_q(i)
        qs_next = stacked_q(jnp.minimum(i + 1, blocks - 1))
        acc = jnp.zeros((HEAD_DIM + ONES_ROWS, rows), F32)
        for t in range(n):
            if t + 2 < n:
                scored = scores(qs, t + 2)
            else:
                scored = scores(qs_next, t + 2 - n)
            if t + 1 < n:
                m, alpha_next = softmax(m, t + 1, chunk_max[(t + 1) % 2])
            else:
                m, alpha_next = softmax(fresh_m, 0, chunk_max[0])
            chunk_max[t % 2] = scored
            pv = jnp.dot(_with_ones_rows(vt_ref[t]), p_buf[t % 2], preferred_element_type=F32)
            acc = alpha * acc + pv
            alpha = alpha_next
        o_t = acc[:HEAD_DIM] / acc[HEAD_DIM:HEAD_DIM + 1]
        o = _unstack_transposed(o_t, A_GROUP, tq)
        o_ref[query_rows(i), :] = (o * sg_ref[query_rows(i), :].astype(F32)).astype(BF16)
        return m, alpha, chunk_max[0], chunk_max[1]

    per_body = max(1, ATT_BODY_CHUNKS // n)

    def body(step, carry):
        for u in range(per_body):
            carry = block(step * per_body + u, *carry)
        return carry

    lax.fori_loop(0, blocks // per_body, body, softmax(fresh_m, 0, first_max[0]) + tuple(first_max))


def _dense_attn(qa, ka, vat, sga, batch, seq):
    gw = A_GROUP * HEAD_DIM
    q3 = qa.reshape(batch, seq, A_W)
    sg3 = sga.reshape(batch, seq, A_W)
    k4 = ka.reshape(A_KV, batch, seq, HEAD_DIM)
    vt4 = vat.reshape(batch, seq // ATT_TK, A_KVW, ATT_TK)
    qspec = pl.BlockSpec((None, seq, gw), lambda b, g: (b, 0, g))
    kspec = pl.BlockSpec((None, None, seq, HEAD_DIM), lambda b, g: (g, b, 0, 0))
    vtspec = pl.BlockSpec((None, seq // ATT_TK, HEAD_DIM, ATT_TK), lambda b, g: (b, 0, g, 0))
    out = pl.pallas_call(
        functools.partial(_dense_attn_kernel, seq=seq), grid=(batch, A_KV),
        in_specs=[qspec, kspec, vtspec, qspec], out_specs=qspec,
        out_shape=jax.ShapeDtypeStruct((batch, seq, A_W), BF16),
        scratch_shapes=[pltpu.VMEM((2, ATT_TK, A_GROUP * ATT_TQ), F32),
                        pltpu.VMEM((2, ATT_TK, A_GROUP * ATT_TQ), BF16)],
        compiler_params=_params(("parallel", "parallel")), name="dense_attn",
    )(q3, k4, vt4, sg3)
    return out.reshape(batch * seq, A_W)


def _retention_kernel(lg_ref, q_ref, k_ref, v_ref, sg_ref, o_ref, st_ref, *, seq):
    c = CHUNK
    nc = seq // c
    pair = pl.program_id(1)
    lgf = [lg_ref[0, 2 * pair + hh] for hh in range(2)]
    lgb = [lg_ref[1, 2 * pair + hh] for hh in range(2)]
    lane = lax.broadcasted_iota(jnp.int32, (1, LANES), 1)
    head_of_lane = [lane < HEAD_DIM, lane >= HEAD_DIM]
    lane_lgf = jnp.where(head_of_lane[0], lgf[0], lgf[1])
    lane_lgb = jnp.where(head_of_lane[0], lgb[0], lgb[1])
    t = lax.broadcasted_iota(jnp.int32, (c, 1), 0).astype(F32)
    kdec_f = jnp.exp((c - 1.0 - t) * lane_lgf)
    kdec_b = jnp.exp(t * lane_lgb)
    qdec_f = jnp.exp((t + 1.0) * lane_lgf)
    qdec_b = jnp.exp((c - t) * lane_lgb)
    row = lax.broadcasted_iota(jnp.int32, (c, 1), 0)
    row_first = row < HEAD_DIM
    cdec_f = jnp.exp(c * jnp.where(row_first, lgf[0], lgf[1]))
    cdec_b = jnp.exp(c * jnp.where(row_first, lgb[0], lgb[1]))
    diff = (lax.broadcasted_iota(jnp.int32, (c, c), 0) - lax.broadcasted_iota(jnp.int32, (c, c), 1)).astype(F32)
    intra_decay = [jnp.where(diff >= 0, jnp.exp(lgf[hh] * jnp.maximum(diff, 0.0)),
                             jnp.exp(lgb[hh] * jnp.maximum(-diff, 0.0))) for hh in range(2)]

    def chunk_rows(n):
        return pl.ds(pl.multiple_of(n * c, c), c)

    def kv_body(n, carry):
        rows = chunk_rows(n)
        k = k_ref[rows, :].astype(F32)
        kf, kb = k * kdec_f, k * kdec_b
        kvf = jnp.zeros((LANES, B_VAL), F32)
        kvb = jnp.zeros((LANES, B_VAL), F32)
        for hh in range(2):
            v = v_ref[rows, hh * B_VAL:(hh + 1) * B_VAL]
            dn = (((0,), (0,)), ((), ()))
            kvf += lax.dot_general(jnp.where(head_of_lane[hh], kf, 0.0).astype(BF16), v, dn,
                                   preferred_element_type=F32)
            kvb += lax.dot_general(jnp.where(head_of_lane[hh], kb, 0.0).astype(BF16), v, dn,
                                   preferred_element_type=F32)
        st_ref[n, 0:LANES, :] = kvf
        st_ref[n, LANES:2 * LANES, :] = kvb
        return carry

    lax.fori_loop(0, nc, kv_body, 0, unroll=RET_UNROLL)

    def scan_fwd(n, state):
        kv = st_ref[n, 0:LANES, :]
        st_ref[n, 0:LANES, :] = state
        return state * cdec_f + kv

    lax.fori_loop(0, nc, scan_fwd, jnp.zeros((LANES, B_VAL), F32))

    def scan_bwd(i, state):
        n = nc - 1 - i
        kv = st_ref[n, LANES:2 * LANES, :]
        st_ref[n, LANES:2 * LANES, :] = state
        return state * cdec_b + kv

    lax.fori_loop(0, nc, scan_bwd, jnp.zeros((LANES, B_VAL), F32))

    def out_body(n, carry):
        rows = chunk_rows(n)
        q = q_ref[rows, :].astype(F32)
        k = k_ref[rows, :]
        qf, qb = q * qdec_f, q * qdec_b
        state = st_ref[n].astype(BF16)
        for hh in range(2):
            v = v_ref[rows, hh * B_VAL:(hh + 1) * B_VAL]
            qm = jnp.where(head_of_lane[hh], q, 0.0).astype(BF16)
            s = lax.dot_general(qm, k, (((1,), (1,)), ((), ())), preferred_element_type=F32)
            intra = jnp.dot((s * intra_decay[hh]).astype(BF16), v, preferred_element_type=F32)
            lhs = jnp.concatenate([jnp.where(head_of_lane[hh], qf, 0.0),
                                   jnp.where(head_of_lane[hh], qb, 0.0)], axis=1).astype(BF16)
            o = intra + jnp.dot(lhs, state, preferred_element_type=F32)
            o = o * lax.rsqrt(jnp.mean(o * o, axis=-1, keepdims=True) + EPS)
            o = o * sg_ref[rows, hh * B_VAL:(hh + 1) * B_VAL].astype(F32)
            o_ref[rows, hh * B_VAL:(hh + 1) * B_VAL] = o.astype(BF16)
        return carry

    lax.fori_loop(0, nc, out_body, 0, unroll=RET_UNROLL)


def _retention(log_gamma, qb, kb, vb, sgb, batch, seq):
    pairs = B_HEADS // 2
    qk_spec = pl.BlockSpec((None, seq, LANES), lambda b, j: (b, 0, j))
    v_spec = pl.BlockSpec((None, seq, 2 * B_VAL), lambda b, j: (b, 0, j))
    out = pl.pallas_call(
        functools.partial(_retention_kernel, seq=seq), grid=(batch, pairs),
        in_specs=[pl.BlockSpec(memory_space=pltpu.SMEM), qk_spec, qk_spec, v_spec, v_spec],
        out_specs=v_spec, out_shape=jax.ShapeDtypeStruct((batch, seq, B_VW), BF16),
        scratch_shapes=[pltpu.VMEM((seq // CHUNK, 2 * LANES, B_VAL), F32)],
        compiler_params=_params(("parallel", "parallel")), name="retention",
    )(log_gamma, qb.reshape(batch, seq, B_QKW), kb.reshape(batch, seq, B_QKW),
      vb.reshape(batch, seq, B_VW), sgb.reshape(batch, seq, B_VW))
    return out.reshape(batch * seq, B_VW)


def _mid_proj_kernel(x_ref, ma_ref, mb_ref, wo_ref, g_ref, wc_ref, x1_ref, q_ref, k_ref, v_ref, sg_ref):
    y = jnp.dot(ma_ref[...], wo_ref[0:A_W, :], preferred_element_type=F32)
    y += jnp.dot(mb_ref[...], wo_ref[A_W:A_W + B_VW, :], preferred_element_type=F32)
    x1 = x_ref[...] + y
    x1_ref[...] = x1
    xn = _rmsnorm_rows(x1, g_ref[...]).astype(BF16)

    def proj(lo, hi):
        return jnp.dot(xn, wc_ref[:, lo:hi], preferred_element_type=F32)

    q_ref[...] = (proj(C_Q, C_K) * (HEAD_DIM ** -0.5 * LOG2E)).astype(BF16)
    zkv = proj(C_K, C_G)
    _split_heads_store(k_ref, zkv[:, :LANES])
    _transposed_chunks_store(v_ref, zkv[:, LANES:])
    sg_ref[...] = _silu(proj(C_G, C_IN)).astype(BF16)


def _mid_proj(x2d, ma, mb, wo, g, wc):
    rows = x2d.shape[0]
    tm = ROW_TILE
    row = lambda i: (i, 0)
    const = lambda i: (0, 0)
    kvrow = lambda i: (0, i, 0)
    out_shape = (
        jax.ShapeDtypeStruct((rows, D_MODEL), F32), jax.ShapeDtypeStruct((rows, C_W), BF16),
        jax.ShapeDtypeStruct((C_KV, rows, HEAD_DIM), BF16), jax.ShapeDtypeStruct((rows // BLOCK, C_KVW, BLOCK), BF16),
        jax.ShapeDtypeStruct((rows, C_W), BF16),
    )
    out_specs = (
        pl.BlockSpec((tm, D_MODEL), row), pl.BlockSpec((tm, C_W), row),
        pl.BlockSpec((C_KV, tm, HEAD_DIM), kvrow), pl.BlockSpec((tm // BLOCK, C_KVW, BLOCK), lambda i: (i, 0, 0)),
        pl.BlockSpec((tm, C_W), row),
    )
    in_specs = [
        pl.BlockSpec((tm, D_MODEL), row), pl.BlockSpec((tm, A_W), row), pl.BlockSpec((tm, B_VW), row),
        pl.BlockSpec((A_W + B_VW, D_MODEL), const), pl.BlockSpec((1, D_MODEL), const),
        pl.BlockSpec((D_MODEL, C_IN), const),
    ]
    return pl.pallas_call(
        _mid_proj_kernel, grid=(rows // tm,), in_specs=in_specs, out_specs=out_specs,
        out_shape=out_shape, compiler_params=_params(("parallel",)), name="mid_proj",
    )(x2d, ma, mb, wo, g, wc)


def _win_attn_kernel(q_ref, kp_ref, k_ref, kn_ref, vp_ref, v_ref, vn_ref, bias_ref, sink_ref, sg_ref, o_ref,
                     kw_all, vt_all, s_buf, p_buf, *, blocks_per_seq):
    j = pl.program_id(1)
    nblk = q_ref.shape[0] // BLOCK
    gw = C_GROUP * HEAD_DIM
    for g in range(C_KV):
        kw_all[g, 0:BLOCK] = kp_ref[g]
        kw_all[g, BLOCK:(nblk + 1) * BLOCK] = k_ref[g]
        kw_all[g, (nblk + 1) * BLOCK:(nblk + 2) * BLOCK] = kn_ref[g]
    vt_all[0] = vp_ref[0]
    vt_all[1:nblk + 1] = v_ref[...]
    vt_all[nblk + 1] = vn_ref[0]

    def block_rows(i):
        return pl.ds(pl.multiple_of(i * BLOCK, BLOCK), BLOCK)

    def scores(i, g):
        qs = _stack_heads_on_rows(q_ref[block_rows(i), g * gw:(g + 1) * gw], C_GROUP)
        kw = kw_all[g, pl.ds(pl.multiple_of(i * BLOCK, BLOCK), 3 * BLOCK), :]
        s_buf[g] = lax.dot_general(kw, qs, (((1,), (1,)), ((), ())), preferred_element_type=F32)

    def softmax(i, g):
        block_in_seq = j * nblk + i
        prev_bias = jnp.where(block_in_seq == 0, 3, 0)
        next_bias = jnp.where(block_in_seq == blocks_per_seq - 1, 3, 2)
        s_t = jnp.concatenate([s_buf[g, 0:BLOCK] + bias_ref[prev_bias, g],
                               s_buf[g, BLOCK:2 * BLOCK] + bias_ref[1, g],
                               s_buf[g, 2 * BLOCK:3 * BLOCK] + bias_ref[next_bias, g]], axis=0)
        sink = sink_ref[g]
        mx = jnp.maximum(jnp.max(s_t, axis=0, keepdims=True), sink)
        p_buf[g] = jnp.exp2(s_t - mx).astype(BF16)
        return jnp.exp2(sink - mx)

    def finish(i, g, sink_term):
        heads = slice(g * HEAD_DIM, (g + 1) * HEAD_DIM)
        vw_t = jnp.concatenate([vt_all[i, heads, :], vt_all[i + 1, heads, :], vt_all[i + 2, heads, :]], axis=1)
        acc = jnp.dot(_with_ones_rows(vw_t), p_buf[g], preferred_element_type=F32)
        denom = acc[HEAD_DIM:HEAD_DIM + 1] + sink_term
        o = _unstack_transposed(acc[:HEAD_DIM] / denom, C_GROUP, BLOCK)
        gate = sg_ref[block_rows(i), g * gw:(g + 1) * gw].astype(F32)
        o_ref[block_rows(i), g * gw:(g + 1) * gw] = (o * gate).astype(BF16)

    scores(0, 0)
    scores(0, 1)

    def body(step, sink_term0):
        for u in range(WIN_UNROLL):
            i = step * WIN_UNROLL + u
            nxt = jnp.minimum(i + 1, nblk - 1)
            scores(nxt, 0)
            sink_term1 = softmax(i, 1)
            finish(i, 0, sink_term0)
            scores(nxt, 1)
            sink_term0 = softmax(nxt, 0)
            finish(i, 1, sink_term1)
        return sink_term0

    lax.fori_loop(0, nblk // WIN_UNROLL, body, softmax(0, 0))


def _win_attn(q, k, vt, bias, sink, sg, batch, seq):
    nb = seq // BLOCK
    sup = WIN_SUPER
    nblk = sup // BLOCK
    q3 = q.reshape(batch, seq, C_W)
    sg3 = sg.reshape(batch, seq, C_W)
    k4 = k.reshape(C_KV, batch, seq, HEAD_DIM)
    qspec = pl.BlockSpec((None, sup, C_W), lambda b, j: (b, j, 0))
    before = lambda j: jnp.maximum(j * nblk - 1, 0)
    after = lambda j: jnp.minimum((j + 1) * nblk, nb - 1)
    halo_k = (C_KV, None, BLOCK, HEAD_DIM)
    halo_v = (1, C_KVW, BLOCK)
    out = pl.pallas_call(
        functools.partial(_win_attn_kernel, blocks_per_seq=nb), grid=(batch, seq // sup),
        in_specs=[qspec,
                  pl.BlockSpec(halo_k, lambda b, j: (0, b, before(j), 0)),
                  pl.BlockSpec((C_KV, None, sup, HEAD_DIM), lambda b, j: (0, b, j, 0)),
                  pl.BlockSpec(halo_k, lambda b, j: (0, b, after(j), 0)),
                  pl.BlockSpec(halo_v, lambda b, j: (b * nb + before(j), 0, 0)),
                  pl.BlockSpec((nblk, C_KVW, BLOCK), lambda b, j: (b * (nb // nblk) + j, 0, 0)),
                  pl.BlockSpec(halo_v, lambda b, j: (b * nb + after(j), 0, 0)),
                  pl.BlockSpec((4, C_KV, BLOCK, C_GROUP * BLOCK), lambda b, j: (0, 0, 0, 0)),
                  pl.BlockSpec((C_KV, 1, C_GROUP * BLOCK), lambda b, j: (0, 0, 0)), qspec],
        out_specs=qspec, out_shape=jax.ShapeDtypeStruct((batch, seq, C_W), BF16),
        scratch_shapes=[pltpu.VMEM((C_KV, sup + 2 * BLOCK, HEAD_DIM), BF16),
                        pltpu.VMEM((nblk + 2, C_KVW, BLOCK), BF16),
                        pltpu.VMEM((C_KV, 3 * BLOCK, C_GROUP * BLOCK), F32),
                        pltpu.VMEM((C_KV, 3 * BLOCK, C_GROUP * BLOCK), BF16)],
        compiler_params=_params(("parallel", "parallel")), name="win_attn",
    )(q3, k4, k4, k4, vt, vt, vt, bias, sink, sg3)
    return out.reshape(batch * seq, C_W)


def _out_proj_kernel(x_ref, m_ref, wo_ref, g_ref, y_ref):
    x2 = x_ref[...] + jnp.dot(m_ref[...], wo_ref[...], preferred_element_type=F32)
    y_ref[...] = _rmsnorm_rows(x2, g_ref[...])


def _out_proj(x2d, m, wo, g):
    rows = x2d.shape[0]
    tm = ROW_TILE
    row = lambda i: (i, 0)
    const = lambda i: (0, 0)
    return pl.pallas_call(
        _out_proj_kernel, grid=(rows // tm,),
        in_specs=[pl.BlockSpec((tm, D_MODEL), row), pl.BlockSpec((tm, C_W), row),
                  pl.BlockSpec((C_W, D_MODEL), const), pl.BlockSpec((1, D_MODEL), const)],
        out_specs=pl.BlockSpec((tm, D_MODEL), row), out_shape=jax.ShapeDtypeStruct((rows, D_MODEL), F32),
        compiler_params=_params(("parallel",)), name="out_proj",
    )(x2d, m, wo, g)


def _trunk(x, prm):
    batch, seq, _ = x.shape
    x2d = x.reshape(batch * seq, D_MODEL)
    qa, ka, va, sga, qb, kb, vb, sgb = _inproj_ab(
        x2d, seq, prm["g0"], prm["w_in_ab"], prm["gain"], prm["ones"], prm["tabs"])
    ma = _dense_attn(qa, ka, va, sga, batch, seq)
    mb = _retention(prm["log_gamma"], qb, kb, vb, sgb, batch, seq)
    x1, qc, kc, vc, sgc = _mid_proj(x2d, ma, mb, prm["w_out_ab"], prm["g1"], prm["w_in_c"])
    mc = _win_attn(qc, kc, vc, prm["bias"], prm["sink"], sgc, batch, seq)
    y = _out_proj(x1, mc, prm["w_out_c"], prm["gf"])
    return y.reshape(batch, seq, D_MODEL)


def kernel(x_prompt, x_sample, norm_g, w_in_ab, qk_norm_a, ret_decay, w_out_ab, w_in_c, sink_c, w_out_c,
           rel_bias, final_norm):
    assert w_in_ab.shape[0] == 1 and w_in_c.shape[0] == 1, "two-layer trunk: one layer of each kind"
    max_seq = max(x_prompt.shape[1], x_sample.shape[1])

    def deinterleaved(w_cols):
        d_in, width = w_cols.shape
        pairs = w_cols.reshape(d_in, width // HEAD_DIM, HALF, 2)
        return pairs.transpose(0, 1, 3, 2).reshape(d_in, width)

    w_ab = w_in_ab[0].astype(BF16)
    w_ab = jnp.concatenate([
        deinterleaved(w_ab[:, AB_QA:AB_KA]), deinterleaved(w_ab[:, AB_KA:AB_VA]), w_ab[:, AB_VA:AB_QB],
        deinterleaved(w_ab[:, AB_QB:AB_KB]), deinterleaved(w_ab[:, AB_KB:AB_VB]), w_ab[:, AB_VB:]], axis=1)
    head_perm = _deinterleave_perm(HEAD_DIM)
    gain = jnp.stack([jnp.tile(qk_norm_a[0, 0][head_perm], 2) * (HEAD_DIM ** -0.5 * LOG2E),
                      jnp.tile(qk_norm_a[0, 1][head_perm], 2)]).astype(F32)
    ones = np.kron(np.eye(2), np.ones((HEAD_DIM, HEAD_DIM)))

    rel = np.arange(3 * BLOCK)[:, None] - BLOCK - np.arange(BLOCK)[None, :]
    bucket_onehot = jax.nn.one_hot(jnp.asarray(_t5_bucket(rel)), REL_BUCKETS, dtype=F32)
    bias = jnp.einsum("kqb,bh->kqh", bucket_onehot, rel_bias.astype(F32) * LOG2E,
                      precision=lax.Precision.HIGHEST)
    bias = jnp.where(jnp.asarray(np.abs(rel) <= WINDOW)[:, :, None], bias, NEG)
    bias = bias.reshape(3, BLOCK, BLOCK, C_KV, C_GROUP).transpose(0, 3, 1, 4, 2)
    bias = bias.reshape(3, C_KV, BLOCK, C_GROUP * BLOCK)
    bias = jnp.concatenate([bias, jnp.full((1,) + bias.shape[1:], NEG, F32)], axis=0)
    sink = jnp.repeat(sink_c[0].astype(F32) * LOG2E, BLOCK).reshape(C_KV, 1, C_GROUP * BLOCK)

    prm = {
        "g0": norm_g[0].reshape(1, D_MODEL), "g1": norm_g[1].reshape(1, D_MODEL),
        "gf": final_norm.reshape(1, D_MODEL),
        "w_in_ab": w_ab, "w_out_ab": w_out_ab[0].astype(BF16),
        "w_in_c": w_in_c[0].astype(BF16), "w_out_c": w_out_c[0].astype(BF16),
        "gain": gain, "ones": jnp.asarray(ones, BF16),
        "tabs": _rope_tables(_axial_angles(max_seq)) + _rope_tables(_linear_angles(max_seq)),
        "log_gamma": -jnp.exp(ret_decay[0].astype(F32)),
        "bias": bias, "sink": sink,
    }
    return _trunk(x_prompt, prm), _trunk(x_sample, prm)
```

```python
import functools

import jax
import jax.numpy as jnp
import numpy as np
from jax import lax
from jax.experimental import pallas as pl
from jax.experimental.pallas import tpu as pltpu

F32 = jnp.float32
BF16 = jnp.bfloat16

D_MODEL = 1024
GRID_W = 64
EPS = 1e-6
ROPE_THETA = 10000.0
HEAD_DIM = 64
HALF = HEAD_DIM // 2
LANES = 128

A_HEADS, A_KV = 8, 2
A_GROUP = A_HEADS // A_KV
A_W, A_KVW = A_HEADS * HEAD_DIM, A_KV * HEAD_DIM
B_HEADS, B_VAL = 4, 128
B_QKW, B_VW = B_HEADS * HEAD_DIM, B_HEADS * B_VAL
C_HEADS, C_KV = 16, 2
C_GROUP = C_HEADS // C_KV
C_W, C_KVW = C_HEADS * HEAD_DIM, C_KV * HEAD_DIM
BLOCK = 128
WINDOW = 128
REL_BUCKETS, REL_MAX_DIST = 32, 128
CHUNK = 128
RET_UNROLL = 8
NEG = -1e30
LOG2E = float(np.log2(np.e))
ONES_ROWS = 16

AB_QA, AB_KA, AB_VA, AB_GA = 0, A_W, A_W + A_KVW, A_W + 2 * A_KVW
AB_QB = AB_GA + A_W
AB_KB, AB_VB, AB_GB = AB_QB + B_QKW, AB_QB + 2 * B_QKW, AB_QB + 2 * B_QKW + B_VW
AB_IN = AB_GB + B_VW
C_Q, C_K, C_V, C_G = 0, C_W, C_W + C_KVW, C_W + 2 * C_KVW
C_IN = C_G + C_W

ROW_TILE = 1024
ATT_TQ = 256
ATT_TK_MAX = 1024
ATT_SEQ_CHUNKS = 4
ATT_BODY_CHUNKS = 16
WIN_SUPER = 2048
WIN_UNROLL = 4
VMEM_LIMIT = 48 * 1024 * 1024


def _att_tk(seq):
    return min(ATT_TK_MAX, seq // ATT_SEQ_CHUNKS)


def _params(sem, flags=None):
    return pltpu.CompilerParams(dimension_semantics=sem, vmem_limit_bytes=VMEM_LIMIT, flags=flags)


def _deinterleave_perm(width):
    heads = width // HEAD_DIM
    within = np.concatenate([np.arange(0, HEAD_DIM, 2), np.arange(1, HEAD_DIM, 2)])
    return (np.arange(heads)[:, None] * HEAD_DIM + within[None, :]).reshape(-1)


def _rope_tables(ang):
    c, s = np.cos(ang), np.sin(ang)
    cos_t = np.concatenate([c, c, c, c], axis=-1)
    sin_t = np.concatenate([-s, s, -s, s], axis=-1)
    return jnp.asarray(cos_t, F32), jnp.asarray(sin_t, F32)


def _axial_angles(n):
    t = np.arange(n)
    quarter = HEAD_DIM // 4
    freqs = ROPE_THETA ** (-np.arange(quarter, dtype=np.float64) / quarter)
    row = (t // GRID_W).astype(np.float64)
    col = (t % GRID_W).astype(np.float64)
    return np.concatenate([row[:, None] * freqs, col[:, None] * freqs], axis=-1)


def _linear_angles(n):
    freqs = ROPE_THETA ** (-np.arange(HALF, dtype=np.float64) / HALF)
    return np.arange(n, dtype=np.float64)[:, None] * freqs


def _t5_bucket(rel):
    half = REL_BUCKETS // 2
    max_exact = half // 2
    ret = (rel > 0).astype(np.int32) * half
    dist = np.abs(rel)
    large = max_exact + (np.log(np.maximum(dist, 1) / max_exact) / np.log(REL_MAX_DIST / max_exact)
                         * (half - max_exact)).astype(np.int32)
    large = np.minimum(large, half - 1)
    return ret + np.where(dist < max_exact, dist, large)


def _rmsnorm_rows(x, gain):
    return x * lax.rsqrt(jnp.mean(x * x, axis=-1, keepdims=True) + EPS) * gain


def _silu(z):
    return z * jax.nn.sigmoid(z)


def _rotate_halves(x, cos_t, sin_t):
    lane = lax.broadcasted_iota(jnp.int32, (1, LANES), 1)
    first_half = (lane % HEAD_DIM) < HALF
    partner = jnp.where(first_half, pltpu.roll(x, LANES - HALF, 1), pltpu.roll(x, HALF, 1))
    return x * cos_t + partner * sin_t


def _split_heads_store(ref, tile):
    ref[0] = tile[:, :HEAD_DIM].astype(ref.dtype)
    ref[1] = tile[:, HEAD_DIM:].astype(ref.dtype)


def _transposed_chunks_store(ref, tile):
    chunk = ref.shape[2]
    tile_t = tile.T
    for c in range(ref.shape[0]):
        ref[c] = tile_t[:, c * chunk:(c + 1) * chunk].astype(ref.dtype)


def _stack_heads_on_rows(q, heads):
    return jnp.concatenate([q[:, h * HEAD_DIM:(h + 1) * HEAD_DIM] for h in range(heads)], axis=0)


def _unstack_transposed(o_t, heads, tokens):
    by_head = jnp.concatenate([o_t[:, h * tokens:(h + 1) * tokens] for h in range(heads)], axis=0)
    return by_head.T


def _with_ones_rows(v_t):
    return jnp.concatenate([v_t, jnp.ones((ONES_ROWS, v_t.shape[1]), v_t.dtype)], axis=0)


def _inproj_ab_kernel(x_ref, g_ref, w_ref, gain_ref, ones_ref, cosa_ref, sina_ref, cosb_ref, sinb_ref,
                      qa_ref, ka_ref, va_ref, sga_ref, qb_ref, kb_ref, vb_ref, sgb_ref):
    xn = _rmsnorm_rows(x_ref[...], g_ref[...]).astype(BF16)

    def proj(lo, hi):
        return jnp.dot(xn, w_ref[:, lo:hi], preferred_element_type=F32)

    cosa, sina = cosa_ref[...], sina_ref[...]
    cosb, sinb = cosb_ref[...], sinb_ref[...]

    def head_norm_rope(tile, gain):
        ss = jnp.dot((tile * tile).astype(BF16), ones_ref[...], preferred_element_type=F32)
        tile = tile * lax.rsqrt(ss * (1.0 / HEAD_DIM) + EPS) * gain
        return _rotate_halves(tile, cosa, sina)

    zq = proj(AB_QA, AB_KA)
    for c in range(A_W // LANES):
        tile = head_norm_rope(zq[:, c * LANES:(c + 1) * LANES], gain_ref[0:1, :])
        qa_ref[:, c * LANES:(c + 1) * LANES] = tile.astype(BF16)
    zkv = proj(AB_KA, AB_GA)
    _split_heads_store(ka_ref, head_norm_rope(zkv[:, :LANES], gain_ref[1:2, :]))
    _transposed_chunks_store(va_ref, zkv[:, LANES:])
    sga_ref[...] = _silu(proj(AB_GA, AB_QB)).astype(BF16)

    zqk = proj(AB_QB, AB_VB)
    for c in range(B_QKW // LANES):
        qb_ref[:, c * LANES:(c + 1) * LANES] = _rotate_halves(
            zqk[:, c * LANES:(c + 1) * LANES], cosb, sinb).astype(BF16)
        kt = zqk[:, B_QKW + c * LANES:B_QKW + (c + 1) * LANES] * (HEAD_DIM ** -0.5)
        kb_ref[:, c * LANES:(c + 1) * LANES] = _rotate_halves(kt, cosb, sinb).astype(BF16)
    vb_ref[...] = proj(AB_VB, AB_GB).astype(BF16)
    sgb_ref[...] = _silu(proj(AB_GB, AB_IN)).astype(BF16)


def _inproj_ab(x2d, seq, g, w, gain, ones, tabs):
    rows = x2d.shape[0]
    tm = ROW_TILE
    per_seq = seq // tm
    row = lambda i: (i, 0)
    const = lambda i: (0, 0)
    tab = lambda i: (i % per_seq, 0)
    kvrow = lambda i: (0, i, 0)
    tk = _att_tk(seq)
    vt_chunks = tm // tk
    out_shape = (
        jax.ShapeDtypeStruct((rows, A_W), BF16), jax.ShapeDtypeStruct((A_KV, rows, HEAD_DIM), BF16),
        jax.ShapeDtypeStruct((rows // tk, A_KVW, tk), BF16), jax.ShapeDtypeStruct((rows, A_W), BF16),
        jax.ShapeDtypeStruct((rows, B_QKW), BF16), jax.ShapeDtypeStruct((rows, B_QKW), BF16),
        jax.ShapeDtypeStruct((rows, B_VW), BF16), jax.ShapeDtypeStruct((rows, B_VW), BF16),
    )
    out_specs = (
        pl.BlockSpec((tm, A_W), row), pl.BlockSpec((A_KV, tm, HEAD_DIM), kvrow),
        pl.BlockSpec((vt_chunks, A_KVW, tk), lambda i: (i, 0, 0)), pl.BlockSpec((tm, A_W), row),
        pl.BlockSpec((tm, B_QKW), row), pl.BlockSpec((tm, B_QKW), row),
        pl.BlockSpec((tm, B_VW), row), pl.BlockSpec((tm, B_VW), row),
    )
    in_specs = [
        pl.BlockSpec((tm, D_MODEL), row), pl.BlockSpec((1, D_MODEL), const),
        pl.BlockSpec((D_MODEL, AB_IN), const), pl.BlockSpec((2, LANES), const),
        pl.BlockSpec((LANES, LANES), const),
    ] + [pl.BlockSpec((tm, LANES), tab)] * 4
    return pl.pallas_call(
        _inproj_ab_kernel, grid=(rows // tm,), in_specs=in_specs, out_specs=out_specs,
        out_shape=out_shape, compiler_params=_params(("parallel",)), name="inproj_ab",
    )(x2d, g, w, gain, ones, *tabs)


def _dense_attn_kernel(q_ref, k_ref, vt_ref, sg_ref, o_ref, s_buf, p_buf, *, seq):
    tq, tk = ATT_TQ, _att_tk(seq)
    rows = A_GROUP * tq
    n = seq // tk
    blocks = seq // tq
    assert n % 2 == 0 and n >= 2

    def query_rows(i):
        return pl.ds(pl.multiple_of(i * tq, tq), tq)

    def stacked_q(i):
        return _stack_heads_on_rows(q_ref[query_rows(i), :], A_GROUP)

    def scores(qs, c):
        k = k_ref[c * tk:(c + 1) * tk, :]
        s_t = lax.dot_general(k, qs, (((1,), (1,)), ((), ())), preferred_element_type=F32)
        s_buf[c % 2] = s_t
        return jnp.max(s_t, axis=0, keepdims=True)

    def softmax(m, c, chunk_max):
        m_new = jnp.maximum(m, chunk_max)
        p_buf[c % 2] = jnp.exp2(s_buf[c % 2] - m_new).astype(BF16)
        return m_new, jnp.exp2(m - m_new)

    fresh_m = jnp.full((1, rows), NEG, F32)
    first_q = stacked_q(0)
    first_max = [scores(first_q, 0), scores(first_q, 1)]

    def block(i, m, alpha, max0, max1):
        chunk_max = [max0, max1]
        qs = stacked_q(i)
        qs_next = stacked_q(jnp.minimum(i + 1, blocks - 1))
        acc = jnp.zeros((HEAD_DIM + ONES_ROWS, rows), F32)
        for t in range(n):
            if t + 2 < n:
                scored = scores(qs, t + 2)
            else:
                scored = scores(qs_next, t + 2 - n)
            if t + 1 < n:
                m, alpha_next = softmax(m, t + 1, chunk_max[(t + 1) % 2])
            else:
                m, alpha_next = softmax(fresh_m, 0, chunk_max[0])
            chunk_max[t % 2] = scored
            pv = jnp.dot(_with_ones_rows(vt_ref[t]), p_buf[t % 2], preferred_element_type=F32)
            acc = alpha * acc + pv
            alpha = alpha_next
        o_t = acc[:HEAD_DIM] / acc[HEAD_DIM:HEAD_DIM + 1]
        o = _unstack_transposed(o_t, A_GROUP, tq)
        o_ref[query_rows(i), :] = (o * sg_ref[query_rows(i), :].astype(F32)).astype(BF16)
        return m, alpha, chunk_max[0], chunk_max[1]

    per_body = max(1, ATT_BODY_CHUNKS // n)

    def body(step, carry):
        for u in range(per_body):
            carry = block(step * per_body + u, *carry)
        return carry

    lax.fori_loop(0, blocks // per_body, body, softmax(fresh_m, 0, first_max[0]) + tuple(first_max))


def _dense_attn(qa, ka, vat, sga, batch, seq):
    gw = A_GROUP * HEAD_DIM
    q3 = qa.reshape(batch, seq, A_W)
    sg3 = sga.reshape(batch, seq, A_W)
    k4 = ka.reshape(A_KV, batch, seq, HEAD_DIM)
    tk = _att_tk(seq)
    vt4 = vat.reshape(batch, seq // tk, A_KVW, tk)
    qspec = pl.BlockSpec((None, seq, gw), lambda b, g: (b, 0, g))
    kspec = pl.BlockSpec((None, None, seq, HEAD_DIM), lambda b, g: (g, b, 0, 0))
    vtspec = pl.BlockSpec((None, seq // tk, HEAD_DIM, tk), lambda b, g: (b, 0, g, 0))
    out = pl.pallas_call(
        functools.partial(_dense_attn_kernel, seq=seq), grid=(batch, A_KV),
        in_specs=[qspec, kspec, vtspec, qspec], out_specs=qspec,
        out_shape=jax.ShapeDtypeStruct((batch, seq, A_W), BF16),
        scratch_shapes=[pltpu.VMEM((2, tk, A_GROUP * ATT_TQ), F32),
                        pltpu.VMEM((2, tk, A_GROUP * ATT_TQ), BF16)],
        compiler_params=_params(("parallel", "parallel")), name="dense_attn",
    )(q3, k4, vt4, sg3)
    return out.reshape(batch * seq, A_W)


def _retention_kernel(lg_ref, q_ref, k_ref, v_ref, sg_ref, o_ref, st_ref, *, seq):
    c = CHUNK
    nc = seq // c
    pv = 2 * B_VAL
    pair = pl.program_id(1)
    lgf = [lg_ref[0, 2 * pair + hh] for hh in range(2)]
    lgb = [lg_ref[1, 2 * pair + hh] for hh in range(2)]
    lane = lax.broadcasted_iota(jnp.int32, (1, LANES), 1)
    first_qk_head = lane < HEAD_DIM
    lane_lgf = jnp.where(first_qk_head, lgf[0], lgf[1])
    lane_lgb = jnp.where(first_qk_head, lgb[0], lgb[1])
    t = lax.broadcasted_iota(jnp.int32, (c, 1), 0).astype(F32)
    kdec_f = jnp.exp((c - 1.0 - t) * lane_lgf)
    kdec_b = jnp.exp(t * lane_lgb)
    qdec_f = jnp.exp((t + 1.0) * lane_lgf)
    qdec_b = jnp.exp((c - t) * lane_lgb)
    row_first = lax.broadcasted_iota(jnp.int32, (LANES, 1), 0) < HEAD_DIM
    first_v_head = lax.broadcasted_iota(jnp.int32, (1, pv), 1) < B_VAL
    same_head = row_first == first_v_head
    cdec_f = jnp.exp(c * jnp.where(row_first, lgf[0], lgf[1]))
    cdec_b = jnp.exp(c * jnp.where(row_first, lgb[0], lgb[1]))
    diff = (lax.broadcasted_iota(jnp.int32, (c, c), 0) - lax.broadcasted_iota(jnp.int32, (c, c), 1)).astype(F32)
    intra_decay = jnp.concatenate(
        [jnp.where(diff >= 0, jnp.exp(lgf[hh] * jnp.maximum(diff, 0.0)), jnp.exp(lgb[hh] * jnp.maximum(-diff, 0.0)))
         for hh in range(2)], axis=1)

    def chunk_rows(n):
        return pl.ds(pl.multiple_of(n * c, c), c)

    def kv_body(n, carry):
        rows = chunk_rows(n)
        k = k_ref[rows, :].astype(F32)
        v = v_ref[rows, :]
        dn = (((0,), (0,)), ((), ()))
        kvf = lax.dot_general((k * kdec_f).astype(BF16), v, dn, preferred_element_type=F32)
        kvb = lax.dot_general((k * kdec_b).astype(BF16), v, dn, preferred_element_type=F32)
        st_ref[n, 0:LANES, :] = jnp.where(same_head, kvf, 0.0)
        st_ref[n, LANES:2 * LANES, :] = jnp.where(same_head, kvb, 0.0)
        return carry

    lax.fori_loop(0, nc, kv_body, 0, unroll=RET_UNROLL)

    def scan_fwd(n, state):
        kv = st_ref[n, 0:LANES, :]
        st_ref[n, 0:LANES, :] = state
        return state * cdec_f + kv

    lax.fori_loop(0, nc, scan_fwd, jnp.zeros((LANES, pv), F32))

    def scan_bwd(i, state):
        n = nc - 1 - i
        kv = st_ref[n, LANES:2 * LANES, :]
        st_ref[n, LANES:2 * LANES, :] = state
        return state * cdec_b + kv

    lax.fori_loop(0, nc, scan_bwd, jnp.zeros((LANES, pv), F32))

    def out_body(n, carry):
        rows = chunk_rows(n)
        q = q_ref[rows, :].astype(F32)
        k = k_ref[rows, :]
        v = v_ref[rows, :]
        zero = jnp.zeros_like(k)
        k_by_head = jnp.concatenate([jnp.where(first_qk_head, k, zero), jnp.where(first_qk_head, zero, k)], axis=0)
        s = lax.dot_general(q.astype(BF16), k_by_head, (((1,), (1,)), ((), ())),
                            preferred_element_type=F32)
        zero_v = jnp.zeros_like(v)
        v_by_head = jnp.concatenate([jnp.where(first_v_head, v, zero_v), jnp.where(first_v_head, zero_v, v)], axis=0)
        lhs = jnp.concatenate([s * intra_decay, q * qdec_f, q * qdec_b], axis=1).astype(BF16)
        rhs = jnp.concatenate([v_by_head, st_ref[n].astype(BF16)], axis=0)
        o_pair = jnp.dot(lhs, rhs, preferred_element_type=F32)
        for hh in range(2):
            o = o_pair[:, hh * B_VAL:(hh + 1) * B_VAL]
            o = o * lax.rsqrt(jnp.mean(o * o, axis=-1, keepdims=True) + EPS)
            o = o * sg_ref[rows, hh * B_VAL:(hh + 1) * B_VAL].astype(F32)
            o_ref[rows, hh * B_VAL:(hh + 1) * B_VAL] = o.astype(BF16)
        return carry

    lax.fori_loop(0, nc, out_body, 0, unroll=RET_UNROLL)


def _retention(log_gamma, qb, kb, vb, sgb, batch, seq):
    pairs = B_HEADS // 2
    qk_spec = pl.BlockSpec((None, seq, LANES), lambda b, j: (b, 0, j))
    v_spec = pl.BlockSpec((None, seq, 2 * B_VAL), lambda b, j: (b, 0, j))
    out = pl.pallas_call(
        functools.partial(_retention_kernel, seq=seq), grid=(batch, pairs),
        in_specs=[pl.BlockSpec(memory_space=pltpu.SMEM), qk_spec, qk_spec, v_spec, v_spec],
        out_specs=v_spec, out_shape=jax.ShapeDtypeStruct((batch, seq, B_VW), BF16),
        scratch_shapes=[pltpu.VMEM((seq // CHUNK, 2 * LANES, 2 * B_VAL), F32)],
        compiler_params=_params(("parallel", "parallel")), name="retention",
    )(log_gamma, qb.reshape(batch, seq, B_QKW), kb.reshape(batch, seq, B_QKW),
      vb.reshape(batch, seq, B_VW), sgb.reshape(batch, seq, B_VW))
    return out.reshape(batch * seq, B_VW)


def _mid_proj_kernel(x_ref, ma_ref, mb_ref, wo_ref, g_ref, wc_ref, x1_ref, q_ref, k_ref, v_ref, sg_ref):
    y = jnp.dot(ma_ref[...], wo_ref[0:A_W, :], preferred_element_type=F32)
    y += jnp.dot(mb_ref[...], wo_ref[A_W:A_W + B_VW, :], preferred_element_type=F32)
    x1 = x_ref[...] + y
    x1_ref[...] = x1
    xn = _rmsnorm_rows(x1, g_ref[...]).astype(BF16)

    def proj(lo, hi):
        return jnp.dot(xn, wc_ref[:, lo:hi], preferred_element_type=F32)

    zq = proj(C_Q, C_K) * (HEAD_DIM ** -0.5 * LOG2E)
    gw = C_GROUP * HEAD_DIM
    for blk in range(q_ref.shape[0]):
        for g in range(C_KV):
            by_head = zq[blk * BLOCK:(blk + 1) * BLOCK, g * gw:(g + 1) * gw].T
            q_ref[blk, g] = jnp.concatenate(
                [by_head[h * HEAD_DIM:(h + 1) * HEAD_DIM] for h in range(C_GROUP)], axis=1).astype(BF16)
    zkv = proj(C_K, C_G)
    _split_heads_store(k_ref, zkv[:, :LANES])
    _transposed_chunks_store(v_ref, zkv[:, LANES:])
    sg_ref[...] = _silu(proj(C_G, C_IN)).astype(BF16)


def _mid_proj(x2d, ma, mb, wo, g, wc):
    rows = x2d.shape[0]
    tm = ROW_TILE
    row = lambda i: (i, 0)
    const = lambda i: (0, 0)
    kvrow = lambda i: (0, i, 0)
    out_shape = (
        jax.ShapeDtypeStruct((rows, D_MODEL), F32),
        jax.ShapeDtypeStruct((rows // BLOCK, C_KV, HEAD_DIM, C_GROUP * BLOCK), BF16),
        jax.ShapeDtypeStruct((C_KV, rows, HEAD_DIM), BF16), jax.ShapeDtypeStruct((rows // BLOCK, C_KVW, BLOCK), BF16),
        jax.ShapeDtypeStruct((rows, C_W), BF16),
    )
    out_specs = (
        pl.BlockSpec((tm, D_MODEL), row),
        pl.BlockSpec((tm // BLOCK, C_KV, HEAD_DIM, C_GROUP * BLOCK), lambda i: (i, 0, 0, 0)),
        pl.BlockSpec((C_KV, tm, HEAD_DIM), kvrow), pl.BlockSpec((tm // BLOCK, C_KVW, BLOCK), lambda i: (i, 0, 0)),
        pl.BlockSpec((tm, C_W), row),
    )
    in_specs = [
        pl.BlockSpec((tm, D_MODEL), row), pl.BlockSpec((tm, A_W), row), pl.BlockSpec((tm, B_VW), row),
        pl.BlockSpec((A_W + B_VW, D_MODEL), const), pl.BlockSpec((1, D_MODEL), const),
        pl.BlockSpec((D_MODEL, C_IN), const),
    ]
    return pl.pallas_call(
        _mid_proj_kernel, grid=(rows // tm,), in_specs=in_specs, out_specs=out_specs,
        out_shape=out_shape, compiler_params=_params(("parallel",)), name="mid_proj",
    )(x2d, ma, mb, wo, g, wc)


def _win_attn_kernel(q_ref, kp_ref, k_ref, kn_ref, vp_ref, v_ref, vn_ref, bias_ref, sink_ref, sg_ref, o_ref,
                     kw_all, vt_all, s_buf, p_buf, *, blocks_per_seq):
    j = pl.program_id(1)
    nblk = q_ref.shape[0]
    gw = C_GROUP * HEAD_DIM
    for g in range(C_KV):
        kw_all[g, 0:BLOCK] = kp_ref[g]
        kw_all[g, BLOCK:(nblk + 1) * BLOCK] = k_ref[g]
        kw_all[g, (nblk + 1) * BLOCK:(nblk + 2) * BLOCK] = kn_ref[g]
    vt_all[0] = vp_ref[0]
    vt_all[1:nblk + 1] = v_ref[...]
    vt_all[nblk + 1] = vn_ref[0]

    def block_rows(i):
        return pl.ds(pl.multiple_of(i * BLOCK, BLOCK), BLOCK)

    def scores(i, g):
        kw = kw_all[g, pl.ds(pl.multiple_of(i * BLOCK, BLOCK), 3 * BLOCK), :]
        s_buf[g] = jnp.dot(kw, q_ref[i, g], preferred_element_type=F32)

    def softmax(i, g):
        block_in_seq = j * nblk + i
        prev_bias = jnp.where(block_in_seq == 0, 3, 0)
        next_bias = jnp.where(block_in_seq == blocks_per_seq - 1, 3, 2)
        s_t = jnp.concatenate([s_buf[g, 0:BLOCK] + bias_ref[prev_bias, g],
                               s_buf[g, BLOCK:2 * BLOCK] + bias_ref[1, g],
                               s_buf[g, 2 * BLOCK:3 * BLOCK] + bias_ref[next_bias, g]], axis=0)
        sink = sink_ref[g]
        mx = jnp.maximum(jnp.max(s_t, axis=0, keepdims=True), sink)
        p_buf[g] = jnp.exp2(s_t - mx).astype(BF16)
        return jnp.exp2(sink - mx)

    def finish(i, g, sink_term):
        heads = slice(g * HEAD_DIM, (g + 1) * HEAD_DIM)
        vw_t = jnp.concatenate([vt_all[i, heads, :], vt_all[i + 1, heads, :], vt_all[i + 2, heads, :]], axis=1)
        acc = jnp.dot(_with_ones_rows(vw_t), p_buf[g], preferred_element_type=F32)
        denom = acc[HEAD_DIM:HEAD_DIM + 1] + sink_term
        o = _unstack_transposed(acc[:HEAD_DIM] / denom, C_GROUP, BLOCK)
        gate = sg_ref[block_rows(i), g * gw:(g + 1) * gw].astype(F32)
        o_ref[block_rows(i), g * gw:(g + 1) * gw] = (o * gate).astype(BF16)

    scores(0, 0)
    scores(0, 1)

    def body(step, sink_term0):
        for u in range(WIN_UNROLL):
            i = step * WIN_UNROLL + u
            nxt = jnp.minimum(i + 1, nblk - 1)
            scores(nxt, 0)
            sink_term1 = softmax(i, 1)
            finish(i, 0, sink_term0)
            scores(nxt, 1)
            sink_term0 = softmax(nxt, 0)
            finish(i, 1, sink_term1)
        return sink_term0

    lax.fori_loop(0, nblk // WIN_UNROLL, body, softmax(0, 0))


def _win_attn(q, k, vt, bias, sink, sg, batch, seq):
    nb = seq // BLOCK
    sup = WIN_SUPER
    nblk = sup // BLOCK
    sg3 = sg.reshape(batch, seq, C_W)
    k4 = k.reshape(C_KV, batch, seq, HEAD_DIM)
    rowspec = pl.BlockSpec((None, sup, C_W), lambda b, j: (b, j, 0))
    qspec = pl.BlockSpec((nblk, C_KV, HEAD_DIM, C_GROUP * BLOCK), lambda b, j: (b * (nb // nblk) + j, 0, 0, 0))
    before = lambda j: jnp.maximum(j * nblk - 1, 0)
    after = lambda j: jnp.minimum((j + 1) * nblk, nb - 1)
    halo_k = (C_KV, None, BLOCK, HEAD_DIM)
    halo_v = (1, C_KVW, BLOCK)
    out = pl.pallas_call(
        functools.partial(_win_attn_kernel, blocks_per_seq=nb), grid=(batch, seq // sup),
        in_specs=[qspec,
                  pl.BlockSpec(halo_k, lambda b, j: (0, b, before(j), 0)),
                  pl.BlockSpec((C_KV, None, sup, HEAD_DIM), lambda b, j: (0, b, j, 0)),
                  pl.BlockSpec(halo_k, lambda b, j: (0, b, after(j), 0)),
                  pl.BlockSpec(halo_v, lambda b, j: (b * nb + before(j), 0, 0)),
                  pl.BlockSpec((nblk, C_KVW, BLOCK), lambda b, j: (b * (nb // nblk) + j, 0, 0)),
                  pl.BlockSpec(halo_v, lambda b, j: (b * nb + after(j), 0, 0)),
                  pl.BlockSpec((4, C_KV, BLOCK, C_GROUP * BLOCK), lambda b, j: (0, 0, 0, 0)),
                  pl.BlockSpec((C_KV, 1, C_GROUP * BLOCK), lambda b, j: (0, 0, 0)), rowspec],
        out_specs=rowspec, out_shape=jax.ShapeDtypeStruct((batch, seq, C_W), BF16),
        scratch_shapes=[pltpu.VMEM((C_KV, sup + 2 * BLOCK, HEAD_DIM), BF16),
                        pltpu.VMEM((nblk + 2, C_KVW, BLOCK), BF16),
                        pltpu.VMEM((C_KV, 3 * BLOCK, C_GROUP * BLOCK), F32),
                        pltpu.VMEM((C_KV, 3 * BLOCK, C_GROUP * BLOCK), BF16)],
        compiler_params=_params(("parallel", "parallel")), name="win_attn",
    )(q, k4, k4, k4, vt, vt, vt, bias, sink, sg3)
    return out.reshape(batch * seq, C_W)


def _out_proj_kernel(x_ref, m_ref, wo_ref, g_ref, y_ref):
    x2 = x_ref[...] + jnp.dot(m_ref[...], wo_ref[...], preferred_element_type=F32)
    y_ref[...] = _rmsnorm_rows(x2, g_ref[...])


def _out_proj(x2d, m, wo, g):
    rows = x2d.shape[0]
    tm = ROW_TILE
    row = lambda i: (i, 0)
    const = lambda i: (0, 0)
    return pl.pallas_call(
        _out_proj_kernel, grid=(rows // tm,),
        in_specs=[pl.BlockSpec((tm, D_MODEL), row), pl.BlockSpec((tm, C_W), row),
                  pl.BlockSpec((C_W, D_MODEL), const), pl.BlockSpec((1, D_MODEL), const)],
        out_specs=pl.BlockSpec((tm, D_MODEL), row), out_shape=jax.ShapeDtypeStruct((rows, D_MODEL), F32),
        compiler_params=_params(("parallel",)), name="out_proj",
    )(x2d, m, wo, g)


def _trunk(x, prm):
    batch, seq, _ = x.shape
    x2d = x.reshape(batch * seq, D_MODEL)
    qa, ka, va, sga, qb, kb, vb, sgb = _inproj_ab(
        x2d, seq, prm["g0"], prm["w_in_ab"], prm["gain"], prm["ones"], prm["tabs"])
    ma = _dense_attn(qa, ka, va, sga, batch, seq)
    mb = _retention(prm["log_gamma"], qb, kb, vb, sgb, batch, seq)
    x1, qc, kc, vc, sgc = _mid_proj(x2d, ma, mb, prm["w_out_ab"], prm["g1"], prm["w_in_c"])
    mc = _win_attn(qc, kc, vc, prm["bias"], prm["sink"], sgc, batch, seq)
    y = _out_proj(x1, mc, prm["w_out_c"], prm["gf"])
    return y.reshape(batch, seq, D_MODEL)


def kernel(x_prompt, x_sample, norm_g, w_in_ab, qk_norm_a, ret_decay, w_out_ab, w_in_c, sink_c, w_out_c,
           rel_bias, final_norm):
    assert w_in_ab.shape[0] == 1 and w_in_c.shape[0] == 1, "two-layer trunk: one layer of each kind"
    max_seq = max(x_prompt.shape[1], x_sample.shape[1])

    def deinterleaved(w_cols):
        d_in, width = w_cols.shape
        pairs = w_cols.reshape(d_in, width // HEAD_DIM, HALF, 2)
        return pairs.transpose(0, 1, 3, 2).reshape(d_in, width)

    w_ab = w_in_ab[0].astype(BF16)
    w_ab = jnp.concatenate([
        deinterleaved(w_ab[:, AB_QA:AB_KA]), deinterleaved(w_ab[:, AB_KA:AB_VA]), w_ab[:, AB_VA:AB_QB],
        deinterleaved(w_ab[:, AB_QB:AB_KB]), deinterleaved(w_ab[:, AB_KB:AB_VB]), w_ab[:, AB_VB:]], axis=1)
    head_perm = _deinterleave_perm(HEAD_DIM)
    gain = jnp.stack([jnp.tile(qk_norm_a[0, 0][head_perm], 2) * (HEAD_DIM ** -0.5 * LOG2E),
                      jnp.tile(qk_norm_a[0, 1][head_perm], 2)]).astype(F32)
    ones = np.kron(np.eye(2), np.ones((HEAD_DIM, HEAD_DIM)))

    rel = np.arange(3 * BLOCK)[:, None] - BLOCK - np.arange(BLOCK)[None, :]
    bucket_onehot = jax.nn.one_hot(jnp.asarray(_t5_bucket(rel)), REL_BUCKETS, dtype=F32)
    bias = jnp.einsum("kqb,bh->kqh", bucket_onehot, rel_bias.astype(F32) * LOG2E,
                      precision=lax.Precision.HIGHEST)
    bias = jnp.where(jnp.asarray(np.abs(rel) <= WINDOW)[:, :, None], bias, NEG)
    bias = bias.reshape(3, BLOCK, BLOCK, C_KV, C_GROUP).transpose(0, 3, 1, 4, 2)
    bias = bias.reshape(3, C_KV, BLOCK, C_GROUP * BLOCK)
    bias = jnp.concatenate([bias, jnp.full((1,) + bias.shape[1:], NEG, F32)], axis=0)
    sink = jnp.repeat(sink_c[0].astype(F32) * LOG2E, BLOCK).reshape(C_KV, 1, C_GROUP * BLOCK)

    prm = {
        "g0": norm_g[0].reshape(1, D_MODEL), "g1": norm_g[1].reshape(1, D_MODEL),
        "gf": final_norm.reshape(1, D_MODEL),
        "w_in_ab": w_ab, "w_out_ab": w_out_ab[0].astype(BF16),
        "w_in_c": w_in_c[0].astype(BF16), "w_out_c": w_out_c[0].astype(BF16),
        "gain": gain, "ones": jnp.asarray(ones, BF16),
        "tabs": _rope_tables(_axial_angles(max_seq)) + _rope_tables(_linear_angles(max_seq)),
        "log_gamma": -jnp.exp(ret_decay[0].astype(F32)),
        "bias": bias, "sink": sink,
    }
    return _trunk(x_prompt, prm), _trunk(x_sample, prm)
```

```python
import functools

import jax
import jax.numpy as jnp
import numpy as np
from jax import lax
from jax.experimental import pallas as pl
from jax.experimental.pallas import tpu as pltpu

F32 = jnp.float32
BF16 = jnp.bfloat16

D_MODEL = 1024
GRID_W = 64
EPS = 1e-6
ROPE_THETA = 10000.0
HEAD_DIM = 64
HALF = HEAD_DIM // 2
LANES = 128

A_HEADS, A_KV = 8, 2
A_GROUP = A_HEADS // A_KV
A_W, A_KVW = A_HEADS * HEAD_DIM, A_KV * HEAD_DIM
B_HEADS, B_VAL = 4, 128
B_QKW, B_VW = B_HEADS * HEAD_DIM, B_HEADS * B_VAL
C_HEADS, C_KV = 16, 2
C_GROUP = C_HEADS // C_KV
C_W, C_KVW = C_HEADS * HEAD_DIM, C_KV * HEAD_DIM
BLOCK = 128
WINDOW = 128
REL_BUCKETS, REL_MAX_DIST = 32, 128
CHUNK = 128
RET_UNROLL = 8
NEG = -1e30
LOG2E = float(np.log2(np.e))
ONES_ROWS = 16

AB_QA, AB_KA, AB_VA, AB_GA = 0, A_W, A_W + A_KVW, A_W + 2 * A_KVW
AB_QB = AB_GA + A_W
AB_KB, AB_VB, AB_GB = AB_QB + B_QKW, AB_QB + 2 * B_QKW, AB_QB + 2 * B_QKW + B_VW
AB_IN = AB_GB + B_VW
C_Q, C_K, C_V, C_G = 0, C_W, C_W + C_KVW, C_W + 2 * C_KVW
C_IN = C_G + C_W

ROW_TILE = 1024
ATT_TQ = 256
ATT_TK_MAX = 1024
ATT_SEQ_CHUNKS = 2
ATT_BODY_CHUNKS = 16
WIN_SUPER = 2048
WIN_UNROLL = 4
VMEM_LIMIT = 48 * 1024 * 1024


def _att_tk(seq):
    return min(ATT_TK_MAX, seq // ATT_SEQ_CHUNKS)


def _params(sem, flags=None):
    return pltpu.CompilerParams(dimension_semantics=sem, vmem_limit_bytes=VMEM_LIMIT, flags=flags)


def _deinterleave_perm(width):
    heads = width // HEAD_DIM
    within = np.concatenate([np.arange(0, HEAD_DIM, 2), np.arange(1, HEAD_DIM, 2)])
    return (np.arange(heads)[:, None] * HEAD_DIM + within[None, :]).reshape(-1)


def _rope_tables(ang):
    c, s = np.cos(ang), np.sin(ang)
    cos_t = np.concatenate([c, c, c, c], axis=-1)
    sin_t = np.concatenate([-s, s, -s, s], axis=-1)
    return jnp.asarray(cos_t, F32), jnp.asarray(sin_t, F32)


def _axial_angles(n):
    t = np.arange(n)
    quarter = HEAD_DIM // 4
    freqs = ROPE_THETA ** (-np.arange(quarter, dtype=np.float64) / quarter)
    row = (t // GRID_W).astype(np.float64)
    col = (t % GRID_W).astype(np.float64)
    return np.concatenate([row[:, None] * freqs, col[:, None] * freqs], axis=-1)


def _linear_angles(n):
    freqs = ROPE_THETA ** (-np.arange(HALF, dtype=np.float64) / HALF)
    return np.arange(n, dtype=np.float64)[:, None] * freqs


def _t5_bucket(rel):
    half = REL_BUCKETS // 2
    max_exact = half // 2
    ret = (rel > 0).astype(np.int32) * half
    dist = np.abs(rel)
    large = max_exact + (np.log(np.maximum(dist, 1) / max_exact) / np.log(REL_MAX_DIST / max_exact)
                         * (half - max_exact)).astype(np.int32)
    large = np.minimum(large, half - 1)
    return ret + np.where(dist < max_exact, dist, large)


def _rmsnorm_rows(x, gain):
    return x * lax.rsqrt(jnp.mean(x * x, axis=-1, keepdims=True) + EPS) * gain


def _silu(z):
    return z * jax.nn.sigmoid(z)


def _rotate_halves(x, cos_t, sin_t):
    lane = lax.broadcasted_iota(jnp.int32, (1, LANES), 1)
    first_half = (lane % HEAD_DIM) < HALF
    partner = jnp.where(first_half, pltpu.roll(x, LANES - HALF, 1), pltpu.roll(x, HALF, 1))
    return x * cos_t + partner * sin_t


def _split_heads_store(ref, tile):
    ref[0] = tile[:, :HEAD_DIM].astype(ref.dtype)
    ref[1] = tile[:, HEAD_DIM:].astype(ref.dtype)


def _transposed_chunks_store(ref, tile):
    chunk = ref.shape[2]
    tile_t = tile.T
    for c in range(ref.shape[0]):
        ref[c] = tile_t[:, c * chunk:(c + 1) * chunk].astype(ref.dtype)


def _stack_heads_on_rows(q, heads):
    return jnp.concatenate([q[:, h * HEAD_DIM:(h + 1) * HEAD_DIM] for h in range(heads)], axis=0)


def _unstack_transposed(o_t, heads, tokens):
    by_head = jnp.concatenate([o_t[:, h * tokens:(h + 1) * tokens] for h in range(heads)], axis=0)
    return by_head.T


def _with_ones_rows(v_t):
    return jnp.concatenate([v_t, jnp.ones((ONES_ROWS, v_t.shape[1]), v_t.dtype)], axis=0)


def _inproj_ab_kernel(x_ref, g_ref, w_ref, gain_ref, ones_ref, cosa_ref, sina_ref, cosb_ref, sinb_ref,
                      qa_ref, ka_ref, va_ref, sga_ref, qb_ref, kb_ref, vb_ref, sgb_ref):
    xn = _rmsnorm_rows(x_ref[...], g_ref[...]).astype(BF16)

    def proj(lo, hi):
        return jnp.dot(xn, w_ref[:, lo:hi], preferred_element_type=F32)

    cosa, sina = cosa_ref[...], sina_ref[...]
    cosb, sinb = cosb_ref[...], sinb_ref[...]

    def head_norm_rope(tile, gain):
        ss = jnp.dot((tile * tile).astype(BF16), ones_ref[...], preferred_element_type=F32)
        tile = tile * lax.rsqrt(ss * (1.0 / HEAD_DIM) + EPS) * gain
        return _rotate_halves(tile, cosa, sina)

    zq = proj(AB_QA, AB_KA)
    for c in range(A_W // LANES):
        tile = head_norm_rope(zq[:, c * LANES:(c + 1) * LANES], gain_ref[0:1, :])
        qa_ref[:, c * LANES:(c + 1) * LANES] = tile.astype(BF16)
    zkv = proj(AB_KA, AB_GA)
    _split_heads_store(ka_ref, head_norm_rope(zkv[:, :LANES], gain_ref[1:2, :]))
    _transposed_chunks_store(va_ref, zkv[:, LANES:])
    sga_ref[...] = _silu(proj(AB_GA, AB_QB)).astype(BF16)

    zqk = proj(AB_QB, AB_VB)
    for c in range(B_QKW // LANES):
        qb_ref[:, c * LANES:(c + 1) * LANES] = _rotate_halves(
            zqk[:, c * LANES:(c + 1) * LANES], cosb, sinb).astype(BF16)
        kt = zqk[:, B_QKW + c * LANES:B_QKW + (c + 1) * LANES] * (HEAD_DIM ** -0.5)
        kb_ref[:, c * LANES:(c + 1) * LANES] = _rotate_halves(kt, cosb, sinb).astype(BF16)
    vb_ref[...] = proj(AB_VB, AB_GB).astype(BF16)
    sgb_ref[...] = _silu(proj(AB_GB, AB_IN)).astype(BF16)


def _inproj_ab(x2d, seq, g, w, gain, ones, tabs):
    rows = x2d.shape[0]
    tm = ROW_TILE
    per_seq = seq // tm
    row = lambda i: (i, 0)
    const = lambda i: (0, 0)
    tab = lambda i: (i % per_seq, 0)
    kvrow = lambda i: (0, i, 0)
    tk = _att_tk(seq)
    vt_chunks = tm // tk
    out_shape = (
        jax.ShapeDtypeStruct((rows, A_W), BF16), jax.ShapeDtypeStruct((A_KV, rows, HEAD_DIM), BF16),
        jax.ShapeDtypeStruct((rows // tk, A_KVW, tk), BF16), jax.ShapeDtypeStruct((rows, A_W), BF16),
        jax.ShapeDtypeStruct((rows, B_QKW), BF16), jax.ShapeDtypeStruct((rows, B_QKW), BF16),
        jax.ShapeDtypeStruct((rows, B_VW), BF16), jax.ShapeDtypeStruct((rows, B_VW), BF16),
    )
    out_specs = (
        pl.BlockSpec((tm, A_W), row), pl.BlockSpec((A_KV, tm, HEAD_DIM), kvrow),
        pl.BlockSpec((vt_chunks, A_KVW, tk), lambda i: (i, 0, 0)), pl.BlockSpec((tm, A_W), row),
        pl.BlockSpec((tm, B_QKW), row), pl.BlockSpec((tm, B_QKW), row),
        pl.BlockSpec((tm, B_VW), row), pl.BlockSpec((tm, B_VW), row),
    )
    in_specs = [
        pl.BlockSpec((tm, D_MODEL), row), pl.BlockSpec((1, D_MODEL), const),
        pl.BlockSpec((D_MODEL, AB_IN), const), pl.BlockSpec((2, LANES), const),
        pl.BlockSpec((LANES, LANES), const),
    ] + [pl.BlockSpec((tm, LANES), tab)] * 4
    return pl.pallas_call(
        _inproj_ab_kernel, grid=(rows // tm,), in_specs=in_specs, out_specs=out_specs,
        out_shape=out_shape, compiler_params=_params(("parallel",)), name="inproj_ab",
    )(x2d, g, w, gain, ones, *tabs)


def _dense_attn_kernel(q_ref, k_ref, vt_ref, sg_ref, o_ref, s_buf, p_buf, *, seq):
    tq, tk = ATT_TQ, _att_tk(seq)
    rows = A_GROUP * tq
    n = seq // tk
    blocks = seq // tq
    assert n % 2 == 0 and n >= 2

    def query_rows(i):
        return pl.ds(pl.multiple_of(i * tq, tq), tq)

    def stacked_q(i):
        return _stack_heads_on_rows(q_ref[query_rows(i), :], A_GROUP)

    def scores(qs, c):
        k = k_ref[c * tk:(c + 1) * tk, :]
        s_t = lax.dot_general(k, qs, (((1,), (1,)), ((), ())), preferred_element_type=F32)
        s_buf[c % 2] = s_t
        return jnp.max(s_t, axis=0, keepdims=True)

    def softmax(m, c, chunk_max):
        m_new = jnp.maximum(m, chunk_max)
        p_buf[c % 2] = jnp.exp2(s_buf[c % 2] - m_new).astype(BF16)
        return m_new, jnp.exp2(m - m_new)

    fresh_m = jnp.full((1, rows), NEG, F32)
    first_q = stacked_q(0)
    first_max = [scores(first_q, 0), scores(first_q, 1)]

    def block(i, m, alpha, max0, max1):
        chunk_max = [max0, max1]
        qs = stacked_q(i)
        qs_next = stacked_q(jnp.minimum(i + 1, blocks - 1))
        acc = jnp.zeros((HEAD_DIM + ONES_ROWS, rows), F32)
        for t in range(n):
            if t + 2 < n:
                scored = scores(qs, t + 2)
            else:
                scored = scores(qs_next, t + 2 - n)
            if t + 1 < n:
                m, alpha_next = softmax(m, t + 1, chunk_max[(t + 1) % 2])
            else:
                m, alpha_next = softmax(fresh_m, 0, chunk_max[0])
            chunk_max[t % 2] = scored
            pv = jnp.dot(_with_ones_rows(vt_ref[t]), p_buf[t % 2], preferred_element_type=F32)
            acc = alpha * acc + pv
            alpha = alpha_next
        o_t = acc[:HEAD_DIM] / acc[HEAD_DIM:HEAD_DIM + 1]
        o = _unstack_transposed(o_t, A_GROUP, tq)
        o_ref[query_rows(i), :] = (o * sg_ref[query_rows(i), :].astype(F32)).astype(BF16)
        return m, alpha, chunk_max[0], chunk_max[1]

    per_body = max(1, ATT_BODY_CHUNKS // n)

    def body(step, carry):
        for u in range(per_body):
            carry = block(step * per_body + u, *carry)
        return carry

    lax.fori_loop(0, blocks // per_body, body, softmax(fresh_m, 0, first_max[0]) + tuple(first_max))


def _dense_attn(qa, ka, vat, sga, batch, seq):
    gw = A_GROUP * HEAD_DIM
    q3 = qa.reshape(batch, seq, A_W)
    sg3 = sga.reshape(batch, seq, A_W)
    k4 = ka.reshape(A_KV, batch, seq, HEAD_DIM)
    tk = _att_tk(seq)
    vt4 = vat.reshape(batch, seq // tk, A_KVW, tk)
    qspec = pl.BlockSpec((None, seq, gw), lambda b, g: (b, 0, g))
    kspec = pl.BlockSpec((None, None, seq, HEAD_DIM), lambda b, g: (g, b, 0, 0))
    vtspec = pl.BlockSpec((None, seq // tk, HEAD_DIM, tk), lambda b, g: (b, 0, g, 0))
    out = pl.pallas_call(
        functools.partial(_dense_attn_kernel, seq=seq), grid=(batch, A_KV),
        in_specs=[qspec, kspec, vtspec, qspec], out_specs=qspec,
        out_shape=jax.ShapeDtypeStruct((batch, seq, A_W), BF16),
        scratch_shapes=[pltpu.VMEM((2, tk, A_GROUP * ATT_TQ), F32),
                        pltpu.VMEM((2, tk, A_GROUP * ATT_TQ), BF16)],
        compiler_params=_params(("parallel", "parallel")), name="dense_attn",
    )(q3, k4, vt4, sg3)
    return out.reshape(batch * seq, A_W)


def _retention_kernel(lg_ref, q_ref, k_ref, v_ref, sg_ref, o_ref, st_ref, *, seq):
    c = CHUNK
    nc = seq // c
    pv = 2 * B_VAL
    pair = pl.program_id(1)
    lgf = [lg_ref[0, 2 * pair + hh] for hh in range(2)]
    lgb = [lg_ref[1, 2 * pair + hh] for hh in range(2)]
    lane = lax.broadcasted_iota(jnp.int32, (1, LANES), 1)
    first_qk_head = lane < HEAD_DIM
    lane_lgf = jnp.where(first_qk_head, lgf[0], lgf[1])
    lane_lgb = jnp.where(first_qk_head, lgb[0], lgb[1])
    t = lax.broadcasted_iota(jnp.int32, (c, 1), 0).astype(F32)
    kdec_f = jnp.exp((c - 1.0 - t) * lane_lgf)
    kdec_b = jnp.exp(t * lane_lgb)
    qdec_f = jnp.exp((t + 1.0) * lane_lgf)
    qdec_b = jnp.exp((c - t) * lane_lgb)
    row_first = lax.broadcasted_iota(jnp.int32, (LANES, 1), 0) < HEAD_DIM
    first_v_head = lax.broadcasted_iota(jnp.int32, (1, pv), 1) < B_VAL
    same_head = row_first == first_v_head
    cdec_f = jnp.exp(c * jnp.where(row_first, lgf[0], lgf[1]))
    cdec_b = jnp.exp(c * jnp.where(row_first, lgb[0], lgb[1]))
    diff = (lax.broadcasted_iota(jnp.int32, (c, c), 0) - lax.broadcasted_iota(jnp.int32, (c, c), 1)).astype(F32)
    intra_decay = jnp.concatenate(
        [jnp.where(diff >= 0, jnp.exp(lgf[hh] * jnp.maximum(diff, 0.0)), jnp.exp(lgb[hh] * jnp.maximum(-diff, 0.0)))
         for hh in range(2)], axis=1)

    def chunk_rows(n):
        return pl.ds(pl.multiple_of(n * c, c), c)

    def kv_body(n, carry):
        rows = chunk_rows(n)
        k = k_ref[rows, :].astype(F32)
        v = v_ref[rows, :]
        dn = (((0,), (0,)), ((), ()))
        kvf = lax.dot_general((k * kdec_f).astype(BF16), v, dn, preferred_element_type=F32)
        kvb = lax.dot_general((k * kdec_b).astype(BF16), v, dn, preferred_element_type=F32)
        st_ref[n, 0:LANES, :] = jnp.where(same_head, kvf, 0.0)
        st_ref[n, LANES:2 * LANES, :] = jnp.where(same_head, kvb, 0.0)
        return carry

    lax.fori_loop(0, nc, kv_body, 0, unroll=RET_UNROLL)

    def scan_fwd(n, state):
        kv = st_ref[n, 0:LANES, :]
        st_ref[n, 0:LANES, :] = state
        return state * cdec_f + kv

    lax.fori_loop(0, nc, scan_fwd, jnp.zeros((LANES, pv), F32))

    def scan_bwd(i, state):
        n = nc - 1 - i
        kv = st_ref[n, LANES:2 * LANES, :]
        st_ref[n, LANES:2 * LANES, :] = state
        return state * cdec_b + kv

    lax.fori_loop(0, nc, scan_bwd, jnp.zeros((LANES, pv), F32))

    def out_body(n, carry):
        rows = chunk_rows(n)
        q = q_ref[rows, :].astype(F32)
        k = k_ref[rows, :]
        v = v_ref[rows, :]
        zero = jnp.zeros_like(k)
        k_by_head = jnp.concatenate([jnp.where(first_qk_head, k, zero), jnp.where(first_qk_head, zero, k)], axis=0)
        s = lax.dot_general(q.astype(BF16), k_by_head, (((1,), (1,)), ((), ())),
                            preferred_element_type=F32)
        zero_v = jnp.zeros_like(v)
        v_by_head = jnp.concatenate([jnp.where(first_v_head, v, zero_v), jnp.where(first_v_head, zero_v, v)], axis=0)
        lhs = jnp.concatenate([s * intra_decay, q * qdec_f, q * qdec_b], axis=1).astype(BF16)
        rhs = jnp.concatenate([v_by_head, st_ref[n].astype(BF16)], axis=0)
        o_pair = jnp.dot(lhs, rhs, preferred_element_type=F32)
        for hh in range(2):
            o = o_pair[:, hh * B_VAL:(hh + 1) * B_VAL]
            o = o * lax.rsqrt(jnp.mean(o * o, axis=-1, keepdims=True) + EPS)
            o = o * sg_ref[rows, hh * B_VAL:(hh + 1) * B_VAL].astype(F32)
            o_ref[rows, hh * B_VAL:(hh + 1) * B_VAL] = o.astype(BF16)
        return carry

    lax.fori_loop(0, nc, out_body, 0, unroll=RET_UNROLL)


def _retention(log_gamma, qb, kb, vb, sgb, batch, seq):
    pairs = B_HEADS // 2
    qk_spec = pl.BlockSpec((None, seq, LANES), lambda b, j: (b, 0, j))
    v_spec = pl.BlockSpec((None, seq, 2 * B_VAL), lambda b, j: (b, 0, j))
    out = pl.pallas_call(
        functools.partial(_retention_kernel, seq=seq), grid=(batch, pairs),
        in_specs=[pl.BlockSpec(memory_space=pltpu.SMEM), qk_spec, qk_spec, v_spec, v_spec],
        out_specs=v_spec, out_shape=jax.ShapeDtypeStruct((batch, seq, B_VW), BF16),
        scratch_shapes=[pltpu.VMEM((seq // CHUNK, 2 * LANES, 2 * B_VAL), F32)],
        compiler_params=_params(("parallel", "parallel")), name="retention",
    )(log_gamma, qb.reshape(batch, seq, B_QKW), kb.reshape(batch, seq, B_QKW),
      vb.reshape(batch, seq, B_VW), sgb.reshape(batch, seq, B_VW))
    return out.reshape(batch * seq, B_VW)


def _mid_proj_kernel(x_ref, ma_ref, mb_ref, wo_ref, g_ref, wc_ref, x1_ref, q_ref, k_ref, v_ref, sg_ref):
    y = jnp.dot(ma_ref[...], wo_ref[0:A_W, :], preferred_element_type=F32)
    y += jnp.dot(mb_ref[...], wo_ref[A_W:A_W + B_VW, :], preferred_element_type=F32)
    x1 = x_ref[...] + y
    x1_ref[...] = x1
    xn = _rmsnorm_rows(x1, g_ref[...]).astype(BF16)

    def proj(lo, hi):
        return jnp.dot(xn, wc_ref[:, lo:hi], preferred_element_type=F32)

    q_ref[...] = (proj(C_Q, C_K) * (HEAD_DIM ** -0.5 * LOG2E)).astype(BF16)
    zkv = proj(C_K, C_G)
    _split_heads_store(k_ref, zkv[:, :LANES])
    _transposed_chunks_store(v_ref, zkv[:, LANES:])
    sg_ref[...] = _silu(proj(C_G, C_IN)).astype(BF16)


def _mid_proj(x2d, ma, mb, wo, g, wc):
    rows = x2d.shape[0]
    tm = ROW_TILE
    row = lambda i: (i, 0)
    const = lambda i: (0, 0)
    kvrow = lambda i: (0, i, 0)
    out_shape = (
        jax.ShapeDtypeStruct((rows, D_MODEL), F32), jax.ShapeDtypeStruct((rows, C_W), BF16),
        jax.ShapeDtypeStruct((C_KV, rows, HEAD_DIM), BF16), jax.ShapeDtypeStruct((rows // BLOCK, C_KVW, BLOCK), BF16),
        jax.ShapeDtypeStruct((rows, C_W), BF16),
    )
    out_specs = (
        pl.BlockSpec((tm, D_MODEL), row), pl.BlockSpec((tm, C_W), row),
        pl.BlockSpec((C_KV, tm, HEAD_DIM), kvrow), pl.BlockSpec((tm // BLOCK, C_KVW, BLOCK), lambda i: (i, 0, 0)),
        pl.BlockSpec((tm, C_W), row),
    )
    in_specs = [
        pl.BlockSpec((tm, D_MODEL), row), pl.BlockSpec((tm, A_W), row), pl.BlockSpec((tm, B_VW), row),
        pl.BlockSpec((A_W + B_VW, D_MODEL), const), pl.BlockSpec((1, D_MODEL), const),
        pl.BlockSpec((D_MODEL, C_IN), const),
    ]
    return pl.pallas_call(
        _mid_proj_kernel, grid=(rows // tm,), in_specs=in_specs, out_specs=out_specs,
        out_shape=out_shape, compiler_params=_params(("parallel",)), name="mid_proj",
    )(x2d, ma, mb, wo, g, wc)


def _win_attn_kernel(q_ref, kp_ref, k_ref, kn_ref, vp_ref, v_ref, vn_ref, bias_ref, sink_ref, sg_ref, o_ref,
                     kw_all, vt_all, s_buf, p_buf, *, blocks_per_seq):
    j = pl.program_id(1)
    nblk = q_ref.shape[0] // BLOCK
    gw = C_GROUP * HEAD_DIM
    for g in range(C_KV):
        kw_all[g, 0:BLOCK] = kp_ref[g]
        kw_all[g, BLOCK:(nblk + 1) * BLOCK] = k_ref[g]
        kw_all[g, (nblk + 1) * BLOCK:(nblk + 2) * BLOCK] = kn_ref[g]
    vt_all[0] = vp_ref[0]
    vt_all[1:nblk + 1] = v_ref[...]
    vt_all[nblk + 1] = vn_ref[0]

    def block_rows(i):
        return pl.ds(pl.multiple_of(i * BLOCK, BLOCK), BLOCK)

    def scores(i, g):
        qs = _stack_heads_on_rows(q_ref[block_rows(i), g * gw:(g + 1) * gw], C_GROUP)
        kw = kw_all[g, pl.ds(pl.multiple_of(i * BLOCK, BLOCK), 3 * BLOCK), :]
        s_buf[g] = lax.dot_general(kw, qs, (((1,), (1,)), ((), ())), preferred_element_type=F32)

    def softmax(i, g):
        block_in_seq = j * nblk + i
        prev_bias = jnp.where(block_in_seq == 0, 3, 0)
        next_bias = jnp.where(block_in_seq == blocks_per_seq - 1, 3, 2)
        s_t = jnp.concatenate([s_buf[g, 0:BLOCK] + bias_ref[prev_bias, g],
                               s_buf[g, BLOCK:2 * BLOCK] + bias_ref[1, g],
                               s_buf[g, 2 * BLOCK:3 * BLOCK] + bias_ref[next_bias, g]], axis=0)
        sink = sink_ref[g]
        mx = jnp.maximum(jnp.max(s_t, axis=0, keepdims=True), sink)
        p_buf[g] = jnp.exp2(s_t - mx).astype(BF16)
        return jnp.exp2(sink - mx)

    def finish(i, g, sink_term):
        heads = slice(g * HEAD_DIM, (g + 1) * HEAD_DIM)
        vw_t = jnp.concatenate([vt_all[i, heads, :], vt_all[i + 1, heads, :], vt_all[i + 2, heads, :]], axis=1)
        acc = jnp.dot(_with_ones_rows(vw_t), p_buf[g], preferred_element_type=F32)
        denom = acc[HEAD_DIM:HEAD_DIM + 1] + sink_term
        o = _unstack_transposed(acc[:HEAD_DIM] / denom, C_GROUP, BLOCK)
        gate = sg_ref[block_rows(i), g * gw:(g + 1) * gw].astype(F32)
        o_ref[block_rows(i), g * gw:(g + 1) * gw] = (o * gate).astype(BF16)

    scores(0, 0)
    scores(0, 1)

    def body(step, sink_term0):
        for u in range(WIN_UNROLL):
            i = step * WIN_UNROLL + u
            nxt = jnp.minimum(i + 1, nblk - 1)
            scores(nxt, 0)
            sink_term1 = softmax(i, 1)
            finish(i, 0, sink_term0)
            scores(nxt, 1)
            sink_term0 = softmax(nxt, 0)
            finish(i, 1, sink_term1)
        return sink_term0

    lax.fori_loop(0, nblk // WIN_UNROLL, body, softmax(0, 0))


def _win_attn(q, k, vt, bias, sink, sg, batch, seq):
    nb = seq // BLOCK
    sup = WIN_SUPER
    nblk = sup // BLOCK
    q3 = q.reshape(batch, seq, C_W)
    sg3 = sg.reshape(batch, seq, C_W)
    k4 = k.reshape(C_KV, batch, seq, HEAD_DIM)
    rowspec = pl.BlockSpec((None, sup, C_W), lambda b, j: (b, j, 0))
    before = lambda j: jnp.maximum(j * nblk - 1, 0)
    after = lambda j: jnp.minimum((j + 1) * nblk, nb - 1)
    halo_k = (C_KV, None, BLOCK, HEAD_DIM)
    halo_v = (1, C_KVW, BLOCK)
    out = pl.pallas_call(
        functools.partial(_win_attn_kernel, blocks_per_seq=nb), grid=(batch, seq // sup),
        in_specs=[rowspec,
                  pl.BlockSpec(halo_k, lambda b, j: (0, b, before(j), 0)),
                  pl.BlockSpec((C_KV, None, sup, HEAD_DIM), lambda b, j: (0, b, j, 0)),
                  pl.BlockSpec(halo_k, lambda b, j: (0, b, after(j), 0)),
                  pl.BlockSpec(halo_v, lambda b, j: (b * nb + before(j), 0, 0)),
                  pl.BlockSpec((nblk, C_KVW, BLOCK), lambda b, j: (b * (nb // nblk) + j, 0, 0)),
                  pl.BlockSpec(halo_v, lambda b, j: (b * nb + after(j), 0, 0)),
                  pl.BlockSpec((4, C_KV, BLOCK, C_GROUP * BLOCK), lambda b, j: (0, 0, 0, 0)),
                  pl.BlockSpec((C_KV, 1, C_GROUP * BLOCK), lambda b, j: (0, 0, 0)), rowspec],
        out_specs=rowspec, out_shape=jax.ShapeDtypeStruct((batch, seq, C_W), BF16),
        scratch_shapes=[pltpu.VMEM((C_KV, sup + 2 * BLOCK, HEAD_DIM), BF16),
                        pltpu.VMEM((nblk + 2, C_KVW, BLOCK), BF16),
                        pltpu.VMEM((C_KV, 3 * BLOCK, C_GROUP * BLOCK), F32),
                        pltpu.VMEM((C_KV, 3 * BLOCK, C_GROUP * BLOCK), BF16)],
        compiler_params=_params(("parallel", "parallel")), name="win_attn",
    )(q3, k4, k4, k4, vt, vt, vt, bias, sink, sg3)
    return out.reshape(batch * seq, C_W)


def _out_proj_kernel(x_ref, m_ref, wo_ref, g_ref, y_ref):
    x2 = x_ref[...] + jnp.dot(m_ref[...], wo_ref[...], preferred_element_type=F32)
    y_ref[...] = _rmsnorm_rows(x2, g_ref[...])


def _out_proj(x2d, m, wo, g):
    rows = x2d.shape[0]
    tm = ROW_TILE
    row = lambda i: (i, 0)
    const = lambda i: (0, 0)
    return pl.pallas_call(
        _out_proj_kernel, grid=(rows // tm,),
        in_specs=[pl.BlockSpec((tm, D_MODEL), row), pl.BlockSpec((tm, C_W), row),
                  pl.BlockSpec((C_W, D_MODEL), const), pl.BlockSpec((1, D_MODEL), const)],
        out_specs=pl.BlockSpec((tm, D_MODEL), row), out_shape=jax.ShapeDtypeStruct((rows, D_MODEL), F32),
        compiler_params=_params(("parallel",)), name="out_proj",
    )(x2d, m, wo, g)


def _trunk(x, prm):
    batch, seq, _ = x.shape
    x2d = x.reshape(batch * seq, D_MODEL)
    qa, ka, va, sga, qb, kb, vb, sgb = _inproj_ab(
        x2d, seq, prm["g0"], prm["w_in_ab"], prm["gain"], prm["ones"], prm["tabs"])
    ma = _dense_attn(qa, ka, va, sga, batch, seq)
    mb = _retention(prm["log_gamma"], qb, kb, vb, sgb, batch, seq)
    x1, qc, kc, vc, sgc = _mid_proj(x2d, ma, mb, prm["w_out_ab"], prm["g1"], prm["w_in_c"])
    mc = _win_attn(qc, kc, vc, prm["bias"], prm["sink"], sgc, batch, seq)
    y = _out_proj(x1, mc, prm["w_out_c"], prm["gf"])
    return y.reshape(batch, seq, D_MODEL)


def kernel(x_prompt, x_sample, norm_g, w_in_ab, qk_norm_a, ret_decay, w_out_ab, w_in_c, sink_c, w_out_c,
           rel_bias, final_norm):
    assert w_in_ab.shape[0] == 1 and w_in_c.shape[0] == 1, "two-layer trunk: one layer of each kind"
    max_seq = max(x_prompt.shape[1], x_sample.shape[1])

    def deinterleaved(w_cols):
        d_in, width = w_cols.shape
        pairs = w_cols.reshape(d_in, width // HEAD_DIM, HALF, 2)
        return pairs.transpose(0, 1, 3, 2).reshape(d_in, width)

    w_ab = w_in_ab[0].astype(BF16)
    w_ab = jnp.concatenate([
        deinterleaved(w_ab[:, AB_QA:AB_KA]), deinterleaved(w_ab[:, AB_KA:AB_VA]), w_ab[:, AB_VA:AB_QB],
        deinterleaved(w_ab[:, AB_QB:AB_KB]), deinterleaved(w_ab[:, AB_KB:AB_VB]), w_ab[:, AB_VB:]], axis=1)
    head_perm = _deinterleave_perm(HEAD_DIM)
    gain = jnp.stack([jnp.tile(qk_norm_a[0, 0][head_perm], 2) * (HEAD_DIM ** -0.5 * LOG2E),
                      jnp.tile(qk_norm_a[0, 1][head_perm], 2)]).astype(F32)
    ones = np.kron(np.eye(2), np.ones((HEAD_DIM, HEAD_DIM)))

    rel = np.arange(3 * BLOCK)[:, None] - BLOCK - np.arange(BLOCK)[None, :]
    bucket_onehot = jax.nn.one_hot(jnp.asarray(_t5_bucket(rel)), REL_BUCKETS, dtype=F32)
    bias = jnp.einsum("kqb,bh->kqh", bucket_onehot, rel_bias.astype(F32) * LOG2E,
                      precision=lax.Precision.HIGHEST)
    bias = jnp.where(jnp.asarray(np.abs(rel) <= WINDOW)[:, :, None], bias, NEG)
    bias = bias.reshape(3, BLOCK, BLOCK, C_KV, C_GROUP).transpose(0, 3, 1, 4, 2)
    bias = bias.reshape(3, C_KV, BLOCK, C_GROUP * BLOCK)
    bias = jnp.concatenate([bias, jnp.full((1,) + bias.shape[1:], NEG, F32)], axis=0)
    sink = jnp.repeat(sink_c[0].astype(F32) * LOG2E, BLOCK).reshape(C_KV, 1, C_GROUP * BLOCK)

    prm = {
        "g0": norm_g[0].reshape(1, D_MODEL), "g1": norm_g[1].reshape(1, D_MODEL),
        "gf": final_norm.reshape(1, D_MODEL),
        "w_in_ab": w_ab, "w_out_ab": w_out_ab[0].astype(BF16),
        "w_in_c": w_in_c[0].astype(BF16), "w_out_c": w_out_c[0].astype(BF16),
        "gain": gain, "ones": jnp.asarray(ones, BF16),
        "tabs": _rope_tables(_axial_angles(max_seq)) + _rope_tables(_linear_angles(max_seq)),
        "log_gamma": -jnp.exp(ret_decay[0].astype(F32)),
        "bias": bias, "sink": sink,
    }
    return _trunk(x_prompt, prm), _trunk(x_sample, prm)
```

```python
import functools

import jax
import jax.numpy as jnp
import numpy as np
from jax import lax
from jax.experimental import pallas as pl
from jax.experimental.pallas import tpu as pltpu

F32 = jnp.float32
BF16 = jnp.bfloat16

D_MODEL = 1024
GRID_W = 64
EPS = 1e-6
ROPE_THETA = 10000.0
HEAD_DIM = 64
HALF = HEAD_DIM // 2
LANES = 128

A_HEADS, A_KV = 8, 2
A_GROUP = A_HEADS // A_KV
A_W, A_KVW = A_HEADS * HEAD_DIM, A_KV * HEAD_DIM
B_HEADS, B_VAL = 4, 128
B_QKW, B_VW = B_HEADS * HEAD_DIM, B_HEADS * B_VAL
C_HEADS, C_KV = 16, 2
C_GROUP = C_HEADS // C_KV
C_W, C_KVW = C_HEADS * HEAD_DIM, C_KV * HEAD_DIM
BLOCK = 128
WINDOW = 128
REL_BUCKETS, REL_MAX_DIST = 32, 128
CHUNK = 128
RET_UNROLL = 8
NEG = -1e30
LOG2E = float(np.log2(np.e))
ONES_ROWS = 16

AB_QA, AB_KA, AB_VA, AB_GA = 0, A_W, A_W + A_KVW, A_W + 2 * A_KVW
AB_QB = AB_GA + A_W
AB_KB, AB_VB, AB_GB = AB_QB + B_QKW, AB_QB + 2 * B_QKW, AB_QB + 2 * B_QKW + B_VW
AB_IN = AB_GB + B_VW
C_Q, C_K, C_V, C_G = 0, C_W, C_W + C_KVW, C_W + 2 * C_KVW
C_IN = C_G + C_W

PROJ_SPLIT = 2
ROW_TILE = 1024
ATT_TQ = 256
ATT_TK_MAX = 1024
ATT_SEQ_CHUNKS = 4
ATT_BODY_CHUNKS = 16
WIN_SUPER = 2048
WIN_UNROLL = 4
VMEM_LIMIT = 48 * 1024 * 1024


def _att_tk(seq):
    return min(ATT_TK_MAX, seq // ATT_SEQ_CHUNKS)


def _att_tq(seq):
    del seq
    return ATT_TQ


def _params(sem, flags=None):
    return pltpu.CompilerParams(dimension_semantics=sem, vmem_limit_bytes=VMEM_LIMIT, flags=flags)


def _deinterleave_perm(width):
    heads = width // HEAD_DIM
    within = np.concatenate([np.arange(0, HEAD_DIM, 2), np.arange(1, HEAD_DIM, 2)])
    return (np.arange(heads)[:, None] * HEAD_DIM + within[None, :]).reshape(-1)


def _rope_tables(ang):
    c, s = np.cos(ang), np.sin(ang)
    cos_t = np.concatenate([c, c, c, c], axis=-1)
    sin_t = np.concatenate([-s, s, -s, s], axis=-1)
    return jnp.asarray(cos_t, F32), jnp.asarray(sin_t, F32)


def _axial_angles(n):
    t = np.arange(n)
    quarter = HEAD_DIM // 4
    freqs = ROPE_THETA ** (-np.arange(quarter, dtype=np.float64) / quarter)
    row = (t // GRID_W).astype(np.float64)
    col = (t % GRID_W).astype(np.float64)
    return np.concatenate([row[:, None] * freqs, col[:, None] * freqs], axis=-1)


def _linear_angles(n):
    freqs = ROPE_THETA ** (-np.arange(HALF, dtype=np.float64) / HALF)
    return np.arange(n, dtype=np.float64)[:, None] * freqs


def _t5_bucket(rel):
    half = REL_BUCKETS // 2
    max_exact = half // 2
    ret = (rel > 0).astype(np.int32) * half
    dist = np.abs(rel)
    large = max_exact + (np.log(np.maximum(dist, 1) / max_exact) / np.log(REL_MAX_DIST / max_exact)
                         * (half - max_exact)).astype(np.int32)
    large = np.minimum(large, half - 1)
    return ret + np.where(dist < max_exact, dist, large)


def _rmsnorm_rows(x, gain):
    return x * lax.rsqrt(jnp.mean(x * x, axis=-1, keepdims=True) + EPS) * gain


def _silu(z):
    return z * jax.nn.sigmoid(z)


def _rotate_halves(x, cos_t, sin_t):
    lane = lax.broadcasted_iota(jnp.int32, (1, LANES), 1)
    first_half = (lane % HEAD_DIM) < HALF
    partner = jnp.where(first_half, pltpu.roll(x, LANES - HALF, 1), pltpu.roll(x, HALF, 1))
    return x * cos_t + partner * sin_t


def _transposed_chunks_store(ref, tile, row0=0):
    chunk = ref.shape[2]
    rows = tile.shape[0]
    tile_t = tile.T
    step = min(chunk, rows)
    for s in range(0, rows, step):
        c, off = divmod(row0 + s, chunk)
        ref[c, :, off:off + step] = tile_t[:, s:s + step].astype(ref.dtype)


def _stack_heads_on_rows(q, heads):
    return jnp.concatenate([q[:, h * HEAD_DIM:(h + 1) * HEAD_DIM] for h in range(heads)], axis=0)


def _unstack_transposed(o_t, heads, tokens):
    by_head = jnp.concatenate([o_t[:, h * tokens:(h + 1) * tokens] for h in range(heads)], axis=0)
    return by_head.T


def _with_ones_rows(v_t):
    return jnp.concatenate([v_t, jnp.ones((ONES_ROWS, v_t.shape[1]), v_t.dtype)], axis=0)


def _inproj_ab_kernel(x_ref, g_ref, w_ref, gain_ref, ones_ref, cosa_ref, sina_ref, cosb_ref, sinb_ref,
                      qa_ref, ka_ref, va_ref, sga_ref, qb_ref, kb_ref, vb_ref, sgb_ref):
    part_rows = x_ref.shape[0] // PROJ_SPLIT
    for part in range(PROJ_SPLIT):
        rows = slice(part * part_rows, (part + 1) * part_rows)
        xn = _rmsnorm_rows(x_ref[rows, :], g_ref[...]).astype(BF16)

        def proj(lo, hi):
            return jnp.dot(xn, w_ref[:, lo:hi], preferred_element_type=F32)

        cosa, sina = cosa_ref[rows, :], sina_ref[rows, :]
        cosb, sinb = cosb_ref[rows, :], sinb_ref[rows, :]

        def head_norm_rope(z, gain):
            width = z.shape[1]
            ss = jnp.dot((z * z).astype(BF16), ones_ref[0:width, 0:width], preferred_element_type=F32)
            z = z * lax.rsqrt(ss * (1.0 / HEAD_DIM) + EPS)
            return [_rotate_halves(z[:, c * LANES:(c + 1) * LANES] * gain, cosa, sina)
                    for c in range(width // LANES)]

        zq = proj(AB_QA, AB_KA)
        for pair in range(A_W // (2 * LANES)):
            tiles = head_norm_rope(zq[:, pair * 2 * LANES:(pair + 1) * 2 * LANES], gain_ref[0:1, :])
            for c, tile in enumerate(tiles):
                lo = (2 * pair + c) * LANES
                qa_ref[rows, lo:lo + LANES] = tile.astype(BF16)
        zkv = proj(AB_KA, AB_GA)
        k_tile, = head_norm_rope(zkv[:, :LANES], gain_ref[1:2, :])
        ka_ref[0, rows, :] = k_tile[:, :HEAD_DIM].astype(BF16)
        ka_ref[1, rows, :] = k_tile[:, HEAD_DIM:].astype(BF16)
        _transposed_chunks_store(va_ref, zkv[:, LANES:], part * part_rows)
        sga_ref[rows, :] = _silu(proj(AB_GA, AB_QB)).astype(BF16)
        sgb_ref[rows, :] = _silu(proj(AB_GB, AB_IN)).astype(BF16)

        zqk = proj(AB_QB, AB_VB)
        for c in range(B_QKW // LANES):
            qb_ref[rows, c * LANES:(c + 1) * LANES] = _rotate_halves(
                zqk[:, c * LANES:(c + 1) * LANES], cosb, sinb).astype(BF16)
            kt = zqk[:, B_QKW + c * LANES:B_QKW + (c + 1) * LANES] * (HEAD_DIM ** -0.5)
            kb_ref[rows, c * LANES:(c + 1) * LANES] = _rotate_halves(kt, cosb, sinb).astype(BF16)
        vb_ref[rows, :] = proj(AB_VB, AB_GB).astype(BF16)


def _inproj_ab(x2d, seq, g, w, gain, ones, tabs):
    rows = x2d.shape[0]
    tm = ROW_TILE
    per_seq = seq // tm
    row = lambda i: (i, 0)
    const = lambda i: (0, 0)
    tab = lambda i: (i % per_seq, 0)
    kvrow = lambda i: (0, i, 0)
    tk = _att_tk(seq)
    vt_chunks = tm // tk
    out_shape = (
        jax.ShapeDtypeStruct((rows, A_W), BF16), jax.ShapeDtypeStruct((A_KV, rows, HEAD_DIM), BF16),
        jax.ShapeDtypeStruct((rows // tk, A_KVW, tk), BF16), jax.ShapeDtypeStruct((rows, A_W), BF16),
        jax.ShapeDtypeStruct((rows, B_QKW), BF16), jax.ShapeDtypeStruct((rows, B_QKW), BF16),
        jax.ShapeDtypeStruct((rows, B_VW), BF16), jax.ShapeDtypeStruct((rows, B_VW), BF16),
    )
    out_specs = (
        pl.BlockSpec((tm, A_W), row), pl.BlockSpec((A_KV, tm, HEAD_DIM), kvrow),
        pl.BlockSpec((vt_chunks, A_KVW, tk), lambda i: (i, 0, 0)), pl.BlockSpec((tm, A_W), row),
        pl.BlockSpec((tm, B_QKW), row), pl.BlockSpec((tm, B_QKW), row),
        pl.BlockSpec((tm, B_VW), row), pl.BlockSpec((tm, B_VW), row),
    )
    in_specs = [
        pl.BlockSpec((tm, D_MODEL), row), pl.BlockSpec((1, D_MODEL), const),
        pl.BlockSpec((D_MODEL, AB_IN), const), pl.BlockSpec((2, LANES), const),
        pl.BlockSpec((2 * LANES, 2 * LANES), const),
    ] + [pl.BlockSpec((tm, LANES), tab)] * 4
    return pl.pallas_call(
        _inproj_ab_kernel, grid=(rows // tm,), in_specs=in_specs, out_specs=out_specs,
        out_shape=out_shape, compiler_params=_params(("parallel",)), name="inproj_ab",
    )(x2d, g, w, gain, ones, *tabs)


def _dense_attn_kernel(q_ref, k_ref, vt_ref, sg_ref, o_ref, s_buf, p_buf, *, seq):
    tq, tk = _att_tq(seq), _att_tk(seq)
    rows = A_GROUP * tq
    n = seq // tk
    blocks = seq // tq
    assert n % 2 == 0 and n >= 2

    def query_rows(i):
        return pl.ds(pl.multiple_of(i * tq, tq), tq)

    def stacked_q(i):
        return _stack_heads_on_rows(q_ref[query_rows(i), :], A_GROUP)

    def scores(qs, c):
        k = k_ref[c * tk:(c + 1) * tk, :]
        s_t = lax.dot_general(k, qs, (((1,), (1,)), ((), ())), preferred_element_type=F32)
        s_buf[c % 2] = s_t
        return jnp.max(s_t, axis=0, keepdims=True)

    def softmax(m, c, chunk_max):
        m_new = jnp.maximum(m, chunk_max)
        p_buf[c % 2] = jnp.exp2(s_buf[c % 2] - m_new).astype(BF16)
        return m_new, jnp.exp2(m - m_new)

    fresh_m = jnp.full((1, rows), NEG, F32)
    first_q = stacked_q(0)
    first_max = [scores(first_q, 0), scores(first_q, 1)]

    def block(i, m, alpha, max0, max1):
        chunk_max = [max0, max1]
        qs = stacked_q(i)
        qs_next = stacked_q(jnp.minimum(i + 1, blocks - 1))
        acc = jnp.zeros((HEAD_DIM + ONES_ROWS, rows), F32)
        for t in range(n):
            if t + 2 < n:
                scored = scores(qs, t + 2)
            else:
                scored = scores(qs_next, t + 2 - n)
            if t + 1 < n:
                m, alpha_next = softmax(m, t + 1, chunk_max[(t + 1) % 2])
            else:
                m, alpha_next = softmax(fresh_m, 0, chunk_max[0])
            chunk_max[t % 2] = scored
            pv = jnp.dot(_with_ones_rows(vt_ref[t]), p_buf[t % 2], preferred_element_type=F32)
            acc = alpha * acc + pv
            alpha = alpha_next
        o_t = acc[:HEAD_DIM] / acc[HEAD_DIM:HEAD_DIM + 1]
        o = _unstack_transposed(o_t, A_GROUP, tq)
        o_ref[query_rows(i), :] = (o * sg_ref[query_rows(i), :].astype(F32)).astype(BF16)
        return m, alpha, chunk_max[0], chunk_max[1]

    per_body = max(1, ATT_BODY_CHUNKS // n)

    def body(step, carry):
        for u in range(per_body):
            carry = block(step * per_body + u, *carry)
        return carry

    lax.fori_loop(0, blocks // per_body, body, softmax(fresh_m, 0, first_max[0]) + tuple(first_max))


def _dense_attn(qa, ka, vat, sga, batch, seq):
    gw = A_GROUP * HEAD_DIM
    q3 = qa.reshape(batch, seq, A_W)
    sg3 = sga.reshape(batch, seq, A_W)
    k4 = ka.reshape(A_KV, batch, seq, HEAD_DIM)
    tk = _att_tk(seq)
    vt4 = vat.reshape(batch, seq // tk, A_KVW, tk)
    qspec = pl.BlockSpec((None, seq, gw), lambda b, g: (b, 0, g))
    kspec = pl.BlockSpec((None, None, seq, HEAD_DIM), lambda b, g: (g, b, 0, 0))
    vtspec = pl.BlockSpec((None, seq // tk, HEAD_DIM, tk), lambda b, g: (b, 0, g, 0))
    out = pl.pallas_call(
        functools.partial(_dense_attn_kernel, seq=seq), grid=(batch, A_KV),
        in_specs=[qspec, kspec, vtspec, qspec], out_specs=qspec,
        out_shape=jax.ShapeDtypeStruct((batch, seq, A_W), BF16),
        scratch_shapes=[pltpu.VMEM((2, tk, A_GROUP * _att_tq(seq)), F32),
                        pltpu.VMEM((2, tk, A_GROUP * _att_tq(seq)), BF16)],
        compiler_params=_params(("parallel", "parallel")), name="dense_attn",
    )(q3, k4, vt4, sg3)
    return out.reshape(batch * seq, A_W)


def _retention_kernel(lg_ref, q_ref, k_ref, v_ref, sg_ref, o_ref, st_ref, *, seq):
    c = CHUNK
    nc = seq // c
    pv = 2 * B_VAL
    pair = pl.program_id(1)
    lgf = [lg_ref[0, 2 * pair + hh] for hh in range(2)]
    lgb = [lg_ref[1, 2 * pair + hh] for hh in range(2)]
    lane = lax.broadcasted_iota(jnp.int32, (1, LANES), 1)
    first_qk_head = lane < HEAD_DIM
    lane_lgf = jnp.where(first_qk_head, lgf[0], lgf[1])
    lane_lgb = jnp.where(first_qk_head, lgb[0], lgb[1])
    t = lax.broadcasted_iota(jnp.int32, (c, 1), 0).astype(F32)
    kdec_f = jnp.exp((c - 1.0 - t) * lane_lgf)
    kdec_b = jnp.exp(t * lane_lgb)
    qdec_f = jnp.exp((t + 1.0) * lane_lgf)
    qdec_b = jnp.exp((c - t) * lane_lgb)
    row_first = lax.broadcasted_iota(jnp.int32, (LANES, 1), 0) < HEAD_DIM
    first_v_head = lax.broadcasted_iota(jnp.int32, (1, pv), 1) < B_VAL
    same_head = row_first == first_v_head
    cdec_f = jnp.exp(c * jnp.where(row_first, lgf[0], lgf[1]))
    cdec_b = jnp.exp(c * jnp.where(row_first, lgb[0], lgb[1]))
    diff = (lax.broadcasted_iota(jnp.int32, (c, c), 0) - lax.broadcasted_iota(jnp.int32, (c, c), 1)).astype(F32)
    intra_decay = jnp.concatenate(
        [jnp.where(diff >= 0, jnp.exp(lgf[hh] * jnp.maximum(diff, 0.0)), jnp.exp(lgb[hh] * jnp.maximum(-diff, 0.0)))
         for hh in range(2)], axis=1)

    def chunk_rows(n):
        return pl.ds(pl.multiple_of(n * c, c), c)

    def kv_body(n, carry):
        rows = chunk_rows(n)
        k = k_ref[rows, :].astype(F32)
        v = v_ref[rows, :]
        dn = (((0,), (0,)), ((), ()))
        kvf = lax.dot_general((k * kdec_f).astype(BF16), v, dn, preferred_element_type=F32)
        kvb = lax.dot_general((k * kdec_b).astype(BF16), v, dn, preferred_element_type=F32)
        st_ref[n, 0:LANES, :] = jnp.where(same_head, kvf, 0.0)
        st_ref[n, LANES:2 * LANES, :] = jnp.where(same_head, kvb, 0.0)
        return carry

    lax.fori_loop(0, nc, kv_body, 0, unroll=RET_UNROLL)

    def scan_fwd(n, state):
        kv = st_ref[n, 0:LANES, :]
        st_ref[n, 0:LANES, :] = state
        return state * cdec_f + kv

    lax.fori_loop(0, nc, scan_fwd, jnp.zeros((LANES, pv), F32))

    def scan_bwd(i, state):
        n = nc - 1 - i
        kv = st_ref[n, LANES:2 * LANES, :]
        st_ref[n, LANES:2 * LANES, :] = state
        return state * cdec_b + kv

    lax.fori_loop(0, nc, scan_bwd, jnp.zeros((LANES, pv), F32))

    def out_body(n, carry):
        rows = chunk_rows(n)
        q = q_ref[rows, :].astype(F32)
        k = k_ref[rows, :]
        v = v_ref[rows, :]
        zero = jnp.zeros_like(k)
        k_by_head = jnp.concatenate([jnp.where(first_qk_head, k, zero), jnp.where(first_qk_head, zero, k)], axis=0)
        s = lax.dot_general(q.astype(BF16), k_by_head, (((1,), (1,)), ((), ())),
                            preferred_element_type=F32)
        zero_v = jnp.zeros_like(v)
        v_by_head = jnp.concatenate([jnp.where(first_v_head, v, zero_v), jnp.where(first_v_head, zero_v, v)], axis=0)
        lhs = jnp.concatenate([s * intra_decay, q * qdec_f, q * qdec_b], axis=1).astype(BF16)
        rhs = jnp.concatenate([v_by_head, st_ref[n].astype(BF16)], axis=0)
        o_pair = jnp.dot(lhs, rhs, preferred_element_type=F32)
        for hh in range(2):
            o = o_pair[:, hh * B_VAL:(hh + 1) * B_VAL]
            o = o * lax.rsqrt(jnp.mean(o * o, axis=-1, keepdims=True) + EPS)
            o = o * sg_ref[rows, hh * B_VAL:(hh + 1) * B_VAL].astype(F32)
            o_ref[rows, hh * B_VAL:(hh + 1) * B_VAL] = o.astype(BF16)
        return carry

    lax.fori_loop(0, nc, out_body, 0, unroll=RET_UNROLL)


def _retention(log_gamma, qb, kb, vb, sgb, batch, seq):
    pairs = B_HEADS // 2
    qk_spec = pl.BlockSpec((None, seq, LANES), lambda b, j: (b, 0, j))
    v_spec = pl.BlockSpec((None, seq, 2 * B_VAL), lambda b, j: (b, 0, j))
    out = pl.pallas_call(
        functools.partial(_retention_kernel, seq=seq), grid=(batch, pairs),
        in_specs=[pl.BlockSpec(memory_space=pltpu.SMEM), qk_spec, qk_spec, v_spec, v_spec],
        out_specs=v_spec, out_shape=jax.ShapeDtypeStruct((batch, seq, B_VW), BF16),
        scratch_shapes=[pltpu.VMEM((seq // CHUNK, 2 * LANES, 2 * B_VAL), F32)],
        compiler_params=_params(("parallel", "parallel")), name="retention",
    )(log_gamma, qb.reshape(batch, seq, B_QKW), kb.reshape(batch, seq, B_QKW),
      vb.reshape(batch, seq, B_VW), sgb.reshape(batch, seq, B_VW))
    return out.reshape(batch * seq, B_VW)


def _mid_proj_kernel(x_ref, ma_ref, mb_ref, wo_ref, g_ref, wc_ref, x1_ref, q_ref, k_ref, v_ref, sg_ref):
    part_rows = x_ref.shape[0] // PROJ_SPLIT
    for part in range(PROJ_SPLIT):
        rows = slice(part * part_rows, (part + 1) * part_rows)
        y = jnp.dot(ma_ref[rows, :], wo_ref[0:A_W, :], preferred_element_type=F32)
        y += jnp.dot(mb_ref[rows, :], wo_ref[A_W:A_W + B_VW, :], preferred_element_type=F32)
        x1 = x_ref[rows, :] + y
        x1_ref[rows, :] = x1
        xn = _rmsnorm_rows(x1, g_ref[...]).astype(BF16)

        def proj(lo, hi):
            return jnp.dot(xn, wc_ref[:, lo:hi], preferred_element_type=F32)

        sg_ref[rows, :] = _silu(proj(C_G, C_IN)).astype(BF16)
        zkv = proj(C_K, C_G)
        k_ref[0, rows, :] = zkv[:, :HEAD_DIM].astype(BF16)
        k_ref[1, rows, :] = zkv[:, HEAD_DIM:LANES].astype(BF16)
        _transposed_chunks_store(v_ref, zkv[:, LANES:], part * part_rows)
        q_ref[rows, :] = (proj(C_Q, C_K) * (HEAD_DIM ** -0.5 * LOG2E)).astype(BF16)


def _mid_proj(x2d, ma, mb, wo, g, wc):
    rows = x2d.shape[0]
    tm = ROW_TILE
    row = lambda i: (i, 0)
    const = lambda i: (0, 0)
    kvrow = lambda i: (0, i, 0)
    out_shape = (
        jax.ShapeDtypeStruct((rows, D_MODEL), F32), jax.ShapeDtypeStruct((rows, C_W), BF16),
        jax.ShapeDtypeStruct((C_KV, rows, HEAD_DIM), BF16), jax.ShapeDtypeStruct((rows // BLOCK, C_KVW, BLOCK), BF16),
        jax.ShapeDtypeStruct((rows, C_W), BF16),
    )
    out_specs = (
        pl.BlockSpec((tm, D_MODEL), row), pl.BlockSpec((tm, C_W), row),
        pl.BlockSpec((C_KV, tm, HEAD_DIM), kvrow), pl.BlockSpec((tm // BLOCK, C_KVW, BLOCK), lambda i: (i, 0, 0)),
        pl.BlockSpec((tm, C_W), row),
    )
    in_specs = [
        pl.BlockSpec((tm, D_MODEL), row), pl.BlockSpec((tm, A_W), row), pl.BlockSpec((tm, B_VW), row),
        pl.BlockSpec((A_W + B_VW, D_MODEL), const), pl.BlockSpec((1, D_MODEL), const),
        pl.BlockSpec((D_MODEL, C_IN), const),
    ]
    return pl.pallas_call(
        _mid_proj_kernel, grid=(rows // tm,), in_specs=in_specs, out_specs=out_specs,
        out_shape=out_shape, compiler_params=_params(("parallel",)), name="mid_proj",
    )(x2d, ma, mb, wo, g, wc)


def _win_attn_kernel(q_ref, kp_ref, k_ref, kn_ref, vp_ref, v_ref, vn_ref, bias_ref, sink_ref, sg_ref, o_ref,
                     kw_all, vt_all, s_buf, p_buf, *, blocks_per_seq):
    j = pl.program_id(1)
    nblk = q_ref.shape[0] // BLOCK
    gw = C_GROUP * HEAD_DIM
    for g in range(C_KV):
        kw_all[g, 0:BLOCK] = kp_ref[g]
        kw_all[g, BLOCK:(nblk + 1) * BLOCK] = k_ref[g]
        kw_all[g, (nblk + 1) * BLOCK:(nblk + 2) * BLOCK] = kn_ref[g]
    vt_all[0] = vp_ref[0]
    vt_all[1:nblk + 1] = v_ref[...]
    vt_all[nblk + 1] = vn_ref[0]

    def block_rows(i):
        return pl.ds(pl.multiple_of(i * BLOCK, BLOCK), BLOCK)

    def scores(i, g):
        qs = _stack_heads_on_rows(q_ref[block_rows(i), g * gw:(g + 1) * gw], C_GROUP)
        kw = kw_all[g, pl.ds(pl.multiple_of(i * BLOCK, BLOCK), 3 * BLOCK), :]
        s_buf[g] = lax.dot_general(kw, qs, (((1,), (1,)), ((), ())), preferred_element_type=F32)

    def softmax(i, g):
        block_in_seq = j * nblk + i
        prev_bias = jnp.where(block_in_seq == 0, 3, 0)
        next_bias = jnp.where(block_in_seq == blocks_per_seq - 1, 3, 2)
        s_t = jnp.concatenate([s_buf[g, 0:BLOCK] + bias_ref[prev_bias, g],
                               s_buf[g, BLOCK:2 * BLOCK] + bias_ref[1, g],
                               s_buf[g, 2 * BLOCK:3 * BLOCK] + bias_ref[next_bias, g]], axis=0)
        sink = sink_ref[g]
        mx = jnp.maximum(jnp.max(s_t, axis=0, keepdims=True), sink)
        p_buf[g] = jnp.exp2(s_t - mx).astype(BF16)
        return jnp.exp2(sink - mx)

    def finish(i, g, sink_term):
        heads = slice(g * HEAD_DIM, (g + 1) * HEAD_DIM)
        vw_t = jnp.concatenate([vt_all[i, heads, :], vt_all[i + 1, heads, :], vt_all[i + 2, heads, :]], axis=1)
        acc = jnp.dot(_with_ones_rows(vw_t), p_buf[g], preferred_element_type=F32)
        denom = acc[HEAD_DIM:HEAD_DIM + 1] + sink_term
        o = _unstack_transposed(acc[:HEAD_DIM] / denom, C_GROUP, BLOCK)
        gate = sg_ref[block_rows(i), g * gw:(g + 1) * gw].astype(F32)
        o_ref[block_rows(i), g * gw:(g + 1) * gw] = (o * gate).astype(BF16)

    scores(0, 0)
    scores(0, 1)

    def body(step, sink_term0):
        for u in range(WIN_UNROLL):
            i = step * WIN_UNROLL + u
            nxt = jnp.minimum(i + 1, nblk - 1)
            scores(nxt, 0)
            sink_term1 = softmax(i, 1)
            finish(i, 0, sink_term0)
            scores(nxt, 1)
            sink_term0 = softmax(nxt, 0)
            finish(i, 1, sink_term1)
        return sink_term0

    lax.fori_loop(0, nblk // WIN_UNROLL, body, softmax(0, 0))


def _win_attn(q, k, vt, bias, sink, sg, batch, seq):
    nb = seq // BLOCK
    sup = WIN_SUPER
    nblk = sup // BLOCK
    q3 = q.reshape(batch, seq, C_W)
    sg3 = sg.reshape(batch, seq, C_W)
    k4 = k.reshape(C_KV, batch, seq, HEAD_DIM)
    rowspec = pl.BlockSpec((None, sup, C_W), lambda b, j: (b, j, 0))
    before = lambda j: jnp.maximum(j * nblk - 1, 0)
    after = lambda j: jnp.minimum((j + 1) * nblk, nb - 1)
    halo_k = (C_KV, None, BLOCK, HEAD_DIM)
    halo_v = (1, C_KVW, BLOCK)
    out = pl.pallas_call(
        functools.partial(_win_attn_kernel, blocks_per_seq=nb), grid=(batch, seq // sup),
        in_specs=[rowspec,
                  pl.BlockSpec(halo_k, lambda b, j: (0, b, before(j), 0)),
                  pl.BlockSpec((C_KV, None, sup, HEAD_DIM), lambda b, j: (0, b, j, 0)),
                  pl.BlockSpec(halo_k, lambda b, j: (0, b, after(j), 0)),
                  pl.BlockSpec(halo_v, lambda b, j: (b * nb + before(j), 0, 0)),
                  pl.BlockSpec((nblk, C_KVW, BLOCK), lambda b, j: (b * (nb // nblk) + j, 0, 0)),
                  pl.BlockSpec(halo_v, lambda b, j: (b * nb + after(j), 0, 0)),
                  pl.BlockSpec((4, C_KV, BLOCK, C_GROUP * BLOCK), lambda b, j: (0, 0, 0, 0)),
                  pl.BlockSpec((C_KV, 1, C_GROUP * BLOCK), lambda b, j: (0, 0, 0)), rowspec],
        out_specs=rowspec, out_shape=jax.ShapeDtypeStruct((batch, seq, C_W), BF16),
        scratch_shapes=[pltpu.VMEM((C_KV, sup + 2 * BLOCK, HEAD_DIM), BF16),
                        pltpu.VMEM((nblk + 2, C_KVW, BLOCK), BF16),
                        pltpu.VMEM((C_KV, 3 * BLOCK, C_GROUP * BLOCK), F32),
                        pltpu.VMEM((C_KV, 3 * BLOCK, C_GROUP * BLOCK), BF16)],
        compiler_params=_params(("parallel", "parallel")), name="win_attn",
    )(q3, k4, k4, k4, vt, vt, vt, bias, sink, sg3)
    return out.reshape(batch * seq, C_W)


def _out_proj_kernel(x_ref, m_ref, wo_ref, g_ref, y_ref):
    part_rows = x_ref.shape[0] // PROJ_SPLIT
    for part in range(PROJ_SPLIT):
        rows = slice(part * part_rows, (part + 1) * part_rows)
        x2 = x_ref[rows, :] + jnp.dot(m_ref[rows, :], wo_ref[...], preferred_element_type=F32)
        y_ref[rows, :] = _rmsnorm_rows(x2, g_ref[...])


def _out_proj(x2d, m, wo, g):
    rows = x2d.shape[0]
    tm = ROW_TILE
    row = lambda i: (i, 0)
    const = lambda i: (0, 0)
    return pl.pallas_call(
        _out_proj_kernel, grid=(rows // tm,),
        in_specs=[pl.BlockSpec((tm, D_MODEL), row), pl.BlockSpec((tm, C_W), row),
                  pl.BlockSpec((C_W, D_MODEL), const), pl.BlockSpec((1, D_MODEL), const)],
        out_specs=pl.BlockSpec((tm, D_MODEL), row), out_shape=jax.ShapeDtypeStruct((rows, D_MODEL), F32),
        compiler_params=_params(("parallel",)), name="out_proj",
    )(x2d, m, wo, g)


def _trunk(x, prm):
    batch, seq, _ = x.shape
    x2d = x.reshape(batch * seq, D_MODEL)
    qa, ka, va, sga, qb, kb, vb, sgb = _inproj_ab(
        x2d, seq, prm["g0"], prm["w_in_ab"], prm["gain"], prm["ones"], prm["tabs"])
    ma = _dense_attn(qa, ka, va, sga, batch, seq)
    mb = _retention(prm["log_gamma"], qb, kb, vb, sgb, batch, seq)
    x1, qc, kc, vc, sgc = _mid_proj(x2d, ma, mb, prm["w_out_ab"], prm["g1"], prm["w_in_c"])
    mc = _win_attn(qc, kc, vc, prm["bias"], prm["sink"], sgc, batch, seq)
    y = _out_proj(x1, mc, prm["w_out_c"], prm["gf"])
    return y.reshape(batch, seq, D_MODEL)


def kernel(x_prompt, x_sample, norm_g, w_in_ab, qk_norm_a, ret_decay, w_out_ab, w_in_c, sink_c, w_out_c,
           rel_bias, final_norm):
    assert w_in_ab.shape[0] == 1 and w_in_c.shape[0] == 1, "two-layer trunk: one layer of each kind"
    max_seq = max(x_prompt.shape[1], x_sample.shape[1])

    def deinterleaved(w_cols):
        d_in, width = w_cols.shape
        pairs = w_cols.reshape(d_in, width // HEAD_DIM, HALF, 2)
        return pairs.transpose(0, 1, 3, 2).reshape(d_in, width)

    w_ab = w_in_ab[0].astype(BF16)
    w_ab = jnp.concatenate([
        deinterleaved(w_ab[:, AB_QA:AB_KA]), deinterleaved(w_ab[:, AB_KA:AB_VA]), w_ab[:, AB_VA:AB_QB],
        deinterleaved(w_ab[:, AB_QB:AB_KB]), deinterleaved(w_ab[:, AB_KB:AB_VB]), w_ab[:, AB_VB:]], axis=1)
    head_perm = _deinterleave_perm(HEAD_DIM)
    gain = jnp.stack([jnp.tile(qk_norm_a[0, 0][head_perm], 2) * (HEAD_DIM ** -0.5 * LOG2E),
                      jnp.tile(qk_norm_a[0, 1][head_perm], 2)]).astype(F32)
    ones = np.kron(np.eye(2 * LANES // HEAD_DIM), np.ones((HEAD_DIM, HEAD_DIM)))

    rel = np.arange(3 * BLOCK)[:, None] - BLOCK - np.arange(BLOCK)[None, :]
    bucket_onehot = jax.nn.one_hot(jnp.asarray(_t5_bucket(rel)), REL_BUCKETS, dtype=F32)
    bias = jnp.einsum("kqb,bh->kqh", bucket_onehot, rel_bias.astype(F32) * LOG2E,
                      precision=lax.Precision.HIGHEST)
    bias = jnp.where(jnp.asarray(np.abs(rel) <= WINDOW)[:, :, None], bias, NEG)
    bias = bias.reshape(3, BLOCK, BLOCK, C_KV, C_GROUP).transpose(0, 3, 1, 4, 2)
    bias = bias.reshape(3, C_KV, BLOCK, C_GROUP * BLOCK)
    bias = jnp.concatenate([bias, jnp.full((1,) + bias.shape[1:], NEG, F32)], axis=0)
    sink = jnp.repeat(sink_c[0].astype(F32) * LOG2E, BLOCK).reshape(C_KV, 1, C_GROUP * BLOCK)

    prm = {
        "g0": norm_g[0].reshape(1, D_MODEL), "g1": norm_g[1].reshape(1, D_MODEL),
        "gf": final_norm.reshape(1, D_MODEL),
        "w_in_ab": w_ab, "w_out_ab": w_out_ab[0].astype(BF16),
        "w_in_c": w_in_c[0].astype(BF16), "w_out_c": w_out_c[0].astype(BF16),
        "gain": gain, "ones": jnp.asarray(ones, BF16),
        "tabs": _rope_tables(_axial_angles(max_seq)) + _rope_tables(_linear_angles(max_seq)),
        "log_gamma": -jnp.exp(ret_decay[0].astype(F32)),
        "bias": bias, "sink": sink,
    }
    return _trunk(x_prompt, prm), _trunk(x_sample, prm)
```

```python
import functools

import jax
import jax.numpy as jnp
import numpy as np
from jax import lax
from jax.experimental import pallas as pl
from jax.experimental.pallas import tpu as pltpu

F32 = jnp.float32
BF16 = jnp.bfloat16

D_MODEL = 1024
GRID_W = 64
EPS = 1e-6
ROPE_THETA = 10000.0
HEAD_DIM = 64
HALF = HEAD_DIM // 2
LANES = 128

A_HEADS, A_KV = 8, 2
A_GROUP = A_HEADS // A_KV
A_W, A_KVW = A_HEADS * HEAD_DIM, A_KV * HEAD_DIM
B_HEADS, B_VAL = 4, 128
B_QKW, B_VW = B_HEADS * HEAD_DIM, B_HEADS * B_VAL
C_HEADS, C_KV = 16, 2
C_GROUP = C_HEADS // C_KV
C_W, C_KVW = C_HEADS * HEAD_DIM, C_KV * HEAD_DIM
BLOCK = 128
WINDOW = 128
REL_BUCKETS, REL_MAX_DIST = 32, 128
CHUNK = 128
RET_UNROLL = 8
NEG = -1e30
LOG2E = float(np.log2(np.e))
ONES_ROWS = 16

AB_QA, AB_KA, AB_VA, AB_GA = 0, A_W, A_W + A_KVW, A_W + 2 * A_KVW
AB_QB = AB_GA + A_W
AB_KB, AB_VB, AB_GB = AB_QB + B_QKW, AB_QB + 2 * B_QKW, AB_QB + 2 * B_QKW + B_VW
AB_IN = AB_GB + B_VW
C_Q, C_K, C_V, C_G = 0, C_W, C_W + C_KVW, C_W + 2 * C_KVW
C_IN = C_G + C_W

PROJ_SPLIT = 2
ROW_TILE = 1024
ATT_TQ = 256
ATT_TK_MAX = 1024
ATT_SEQ_CHUNKS = 4
ATT_BODY_CHUNKS = 16
WIN_SUPER = 2048
WIN_UNROLL = 4
VMEM_LIMIT = 48 * 1024 * 1024


def _att_tk(seq):
    return min(ATT_TK_MAX, seq // ATT_SEQ_CHUNKS)


def _att_tq(seq):
    del seq
    return ATT_TQ


def _params(sem, flags=None):
    return pltpu.CompilerParams(dimension_semantics=sem, vmem_limit_bytes=VMEM_LIMIT, flags=flags)


def _rope_tables(ang):
    c, s = np.repeat(np.cos(ang), 2, axis=-1), np.repeat(np.sin(ang), 2, axis=-1)
    s[:, 0::2] *= -1.0
    return jnp.asarray(np.tile(c, (1, 2)), F32), jnp.asarray(np.tile(s, (1, 2)), F32)


def _axial_angles(n):
    t = np.arange(n)
    quarter = HEAD_DIM // 4
    freqs = ROPE_THETA ** (-np.arange(quarter, dtype=np.float64) / quarter)
    row = (t // GRID_W).astype(np.float64)
    col = (t % GRID_W).astype(np.float64)
    return np.concatenate([row[:, None] * freqs, col[:, None] * freqs], axis=-1)


def _linear_angles(n):
    freqs = ROPE_THETA ** (-np.arange(HALF, dtype=np.float64) / HALF)
    return np.arange(n, dtype=np.float64)[:, None] * freqs


def _t5_bucket(rel):
    half = REL_BUCKETS // 2
    max_exact = half // 2
    ret = (rel > 0).astype(np.int32) * half
    dist = np.abs(rel)
    large = max_exact + (np.log(np.maximum(dist, 1) / max_exact) / np.log(REL_MAX_DIST / max_exact)
                         * (half - max_exact)).astype(np.int32)
    large = np.minimum(large, half - 1)
    return ret + np.where(dist < max_exact, dist, large)


def _rmsnorm_rows(x, gain):
    return x * lax.rsqrt(jnp.mean(x * x, axis=-1, keepdims=True) + EPS) * gain


def _silu(z):
    return z * jax.nn.sigmoid(z)


def _rotate_pairs(x, cos_t, sin_t):
    lane = lax.broadcasted_iota(jnp.int32, (1, LANES), 1)
    partner = jnp.where(lane % 2 == 0, pltpu.roll(x, LANES - 1, 1), pltpu.roll(x, 1, 1))
    return x * cos_t + partner * sin_t


def _transposed_chunks_store(ref, tile, row0=0):
    chunk = ref.shape[2]
    rows = tile.shape[0]
    tile_t = tile.T
    step = min(chunk, rows)
    for s in range(0, rows, step):
        c, off = divmod(row0 + s, chunk)
        ref[c, :, off:off + step] = tile_t[:, s:s + step].astype(ref.dtype)


def _stack_heads_on_rows(q, heads):
    return jnp.concatenate([q[:, h * HEAD_DIM:(h + 1) * HEAD_DIM] for h in range(heads)], axis=0)


def _unstack_transposed(o_t, heads, tokens):
    by_head = jnp.concatenate([o_t[:, h * tokens:(h + 1) * tokens] for h in range(heads)], axis=0)
    return by_head.T


def _with_ones_rows(v_t):
    return jnp.concatenate([v_t, jnp.ones((ONES_ROWS, v_t.shape[1]), v_t.dtype)], axis=0)


def _inproj_ab_kernel(x_ref, g_ref, w_ref, gain_ref, ones_ref, cosa_ref, sina_ref, cosb_ref, sinb_ref,
                      qa_ref, ka_ref, va_ref, sga_ref, qb_ref, kb_ref, vb_ref, sgb_ref):
    part_rows = x_ref.shape[0] // PROJ_SPLIT
    for part in range(PROJ_SPLIT):
        rows = slice(part * part_rows, (part + 1) * part_rows)
        xn = _rmsnorm_rows(x_ref[rows, :], g_ref[...]).astype(BF16)

        def proj(lo, hi):
            return jnp.dot(xn, w_ref[:, lo:hi], preferred_element_type=F32)

        cosa, sina = cosa_ref[rows, :], sina_ref[rows, :]
        cosb, sinb = cosb_ref[rows, :], sinb_ref[rows, :]

        def head_norm_rope(z, gain):
            width = z.shape[1]
            ss = jnp.dot((z * z).astype(BF16), ones_ref[0:width, 0:width], preferred_element_type=F32)
            z = z * lax.rsqrt(ss * (1.0 / HEAD_DIM) + EPS)
            return [_rotate_pairs(z[:, c * LANES:(c + 1) * LANES] * gain, cosa, sina)
                    for c in range(width // LANES)]

        zq = proj(AB_QA, AB_KA)
        for pair in range(A_W // (2 * LANES)):
            tiles = head_norm_rope(zq[:, pair * 2 * LANES:(pair + 1) * 2 * LANES], gain_ref[0:1, :])
            for c, tile in enumerate(tiles):
                lo = (2 * pair + c) * LANES
                qa_ref[rows, lo:lo + LANES] = tile.astype(BF16)
        zkv = proj(AB_KA, AB_GA)
        k_tile, = head_norm_rope(zkv[:, :LANES], gain_ref[1:2, :])
        ka_ref[0, rows, :] = k_tile[:, :HEAD_DIM].astype(BF16)
        ka_ref[1, rows, :] = k_tile[:, HEAD_DIM:].astype(BF16)
        _transposed_chunks_store(va_ref, zkv[:, LANES:], part * part_rows)
        sga_ref[rows, :] = _silu(proj(AB_GA, AB_QB)).astype(BF16)
        sgb_ref[rows, :] = _silu(proj(AB_GB, AB_IN)).astype(BF16)

        zqk = proj(AB_QB, AB_VB)
        for c in range(B_QKW // LANES):
            qb_ref[rows, c * LANES:(c + 1) * LANES] = _rotate_pairs(
                zqk[:, c * LANES:(c + 1) * LANES], cosb, sinb).astype(BF16)
            kt = zqk[:, B_QKW + c * LANES:B_QKW + (c + 1) * LANES] * (HEAD_DIM ** -0.5)
            kb_ref[rows, c * LANES:(c + 1) * LANES] = _rotate_pairs(kt, cosb, sinb).astype(BF16)
        vb_ref[rows, :] = proj(AB_VB, AB_GB).astype(BF16)


def _inproj_ab(x2d, seq, g, w, gain, ones, tabs):
    rows = x2d.shape[0]
    tm = ROW_TILE
    per_seq = seq // tm
    row = lambda i: (i, 0)
    const = lambda i: (0, 0)
    tab = lambda i: (i % per_seq, 0)
    kvrow = lambda i: (0, i, 0)
    tk = _att_tk(seq)
    vt_chunks = tm // tk
    out_shape = (
        jax.ShapeDtypeStruct((rows, A_W), BF16), jax.ShapeDtypeStruct((A_KV, rows, HEAD_DIM), BF16),
        jax.ShapeDtypeStruct((rows // tk, A_KVW, tk), BF16), jax.ShapeDtypeStruct((rows, A_W), BF16),
        jax.ShapeDtypeStruct((rows, B_QKW), BF16), jax.ShapeDtypeStruct((rows, B_QKW), BF16),
        jax.ShapeDtypeStruct((rows, B_VW), BF16), jax.ShapeDtypeStruct((rows, B_VW), BF16),
    )
    out_specs = (
        pl.BlockSpec((tm, A_W), row), pl.BlockSpec((A_KV, tm, HEAD_DIM), kvrow),
        pl.BlockSpec((vt_chunks, A_KVW, tk), lambda i: (i, 0, 0)), pl.BlockSpec((tm, A_W), row),
        pl.BlockSpec((tm, B_QKW), row), pl.BlockSpec((tm, B_QKW), row),
        pl.BlockSpec((tm, B_VW), row), pl.BlockSpec((tm, B_VW), row),
    )
    in_specs = [
        pl.BlockSpec((tm, D_MODEL), row), pl.BlockSpec((1, D_MODEL), const),
        pl.BlockSpec((D_MODEL, AB_IN), const), pl.BlockSpec((2, LANES), const),
        pl.BlockSpec((2 * LANES, 2 * LANES), const),
    ] + [pl.BlockSpec((tm, LANES), tab)] * 4
    return pl.pallas_call(
        _inproj_ab_kernel, grid=(rows // tm,), in_specs=in_specs, out_specs=out_specs,
        out_shape=out_shape, compiler_params=_params(("parallel",)), name="inproj_ab",
    )(x2d, g, w, gain, ones, *tabs)


def _dense_attn_kernel(q_ref, k_ref, vt_ref, sg_ref, o_ref, s_buf, p_buf, *, seq):
    tq, tk = _att_tq(seq), _att_tk(seq)
    rows = A_GROUP * tq
    n = seq // tk
    blocks = seq // tq
    assert n % 2 == 0 and n >= 2

    def query_rows(i):
        return pl.ds(pl.multiple_of(i * tq, tq), tq)

    def stacked_q(i):
        return _stack_heads_on_rows(q_ref[query_rows(i), :], A_GROUP)

    def scores(qs, c):
        k = k_ref[c * tk:(c + 1) * tk, :]
        s_t = lax.dot_general(k, qs, (((1,), (1,)), ((), ())), preferred_element_type=F32)
        s_buf[c % 2] = s_t
        return jnp.max(s_t, axis=0, keepdims=True)

    def softmax(m, c, chunk_max):
        m_new = jnp.maximum(m, chunk_max)
        p_buf[c % 2] = jnp.exp2(s_buf[c % 2] - m_new).astype(BF16)
        return m_new, jnp.exp2(m - m_new)

    fresh_m = jnp.full((1, rows), NEG, F32)
    first_q = stacked_q(0)
    first_max = [scores(first_q, 0), scores(first_q, 1)]

    def block(i, m, alpha, max0, max1):
        chunk_max = [max0, max1]
        qs = stacked_q(i)
        qs_next = stacked_q(jnp.minimum(i + 1, blocks - 1))
        acc = jnp.zeros((HEAD_DIM + ONES_ROWS, rows), F32)
        for t in range(n):
            if t + 2 < n:
                scored = scores(qs, t + 2)
            else:
                scored = scores(qs_next, t + 2 - n)
            if t + 1 < n:
                m, alpha_next = softmax(m, t + 1, chunk_max[(t + 1) % 2])
            else:
                m, alpha_next = softmax(fresh_m, 0, chunk_max[0])
            chunk_max[t % 2] = scored
            pv = jnp.dot(_with_ones_rows(vt_ref[t]), p_buf[t % 2], preferred_element_type=F32)
            acc = alpha * acc + pv
            alpha = alpha_next
        o_t = acc[:HEAD_DIM] / acc[HEAD_DIM:HEAD_DIM + 1]
        o = _unstack_transposed(o_t, A_GROUP, tq)
        o_ref[query_rows(i), :] = (o * sg_ref[query_rows(i), :].astype(F32)).astype(BF16)
        return m, alpha, chunk_max[0], chunk_max[1]

    per_body = max(1, ATT_BODY_CHUNKS // n)

    def body(step, carry):
        for u in range(per_body):
            carry = block(step * per_body + u, *carry)
        return carry

    lax.fori_loop(0, blocks // per_body, body, softmax(fresh_m, 0, first_max[0]) + tuple(first_max))


def _dense_attn(qa, ka, vat, sga, batch, seq):
    gw = A_GROUP * HEAD_DIM
    q3 = qa.reshape(batch, seq, A_W)
    sg3 = sga.reshape(batch, seq, A_W)
    k4 = ka.reshape(A_KV, batch, seq, HEAD_DIM)
    tk = _att_tk(seq)
    vt4 = vat.reshape(batch, seq // tk, A_KVW, tk)
    qspec = pl.BlockSpec((None, seq, gw), lambda b, g: (b, 0, g))
    kspec = pl.BlockSpec((None, None, seq, HEAD_DIM), lambda b, g: (g, b, 0, 0))
    vtspec = pl.BlockSpec((None, seq // tk, HEAD_DIM, tk), lambda b, g: (b, 0, g, 0))
    out = pl.pallas_call(
        functools.partial(_dense_attn_kernel, seq=seq), grid=(batch, A_KV),
        in_specs=[qspec, kspec, vtspec, qspec], out_specs=qspec,
        out_shape=jax.ShapeDtypeStruct((batch, seq, A_W), BF16),
        scratch_shapes=[pltpu.VMEM((2, tk, A_GROUP * _att_tq(seq)), F32),
                        pltpu.VMEM((2, tk, A_GROUP * _att_tq(seq)), BF16)],
        compiler_params=_params(("parallel", "parallel")), name="dense_attn",
    )(q3, k4, vt4, sg3)
    return out.reshape(batch * seq, A_W)


def _retention_kernel(lg_ref, q_ref, k_ref, v_ref, sg_ref, o_ref, st_ref, *, seq):
    c = CHUNK
    nc = seq // c
    pv = 2 * B_VAL
    pair = pl.program_id(1)
    lgf = [lg_ref[0, 2 * pair + hh] for hh in range(2)]
    lgb = [lg_ref[1, 2 * pair + hh] for hh in range(2)]
    lane = lax.broadcasted_iota(jnp.int32, (1, LANES), 1)
    first_qk_head = lane < HEAD_DIM
    lane_lgf = jnp.where(first_qk_head, lgf[0], lgf[1])
    lane_lgb = jnp.where(first_qk_head, lgb[0], lgb[1])
    t = lax.broadcasted_iota(jnp.int32, (c, 1), 0).astype(F32)
    kdec_f = jnp.exp((c - 1.0 - t) * lane_lgf)
    kdec_b = jnp.exp(t * lane_lgb)
    qdec_f = jnp.exp((t + 1.0) * lane_lgf)
    qdec_b = jnp.exp((c - t) * lane_lgb)
    row_first = lax.broadcasted_iota(jnp.int32, (LANES, 1), 0) < HEAD_DIM
    first_v_head = lax.broadcasted_iota(jnp.int32, (1, pv), 1) < B_VAL
    same_head = row_first == first_v_head
    cdec_f = jnp.exp(c * jnp.where(row_first, lgf[0], lgf[1]))
    cdec_b = jnp.exp(c * jnp.where(row_first, lgb[0], lgb[1]))
    diff = (lax.broadcasted_iota(jnp.int32, (c, c), 0) - lax.broadcasted_iota(jnp.int32, (c, c), 1)).astype(F32)
    intra_decay = jnp.concatenate(
        [jnp.where(diff >= 0, jnp.exp(lgf[hh] * jnp.maximum(diff, 0.0)), jnp.exp(lgb[hh] * jnp.maximum(-diff, 0.0)))
         for hh in range(2)], axis=1)

    def chunk_rows(n):
        return pl.ds(pl.multiple_of(n * c, c), c)

    def kv_body(n, carry):
        rows = chunk_rows(n)
        k = k_ref[rows, :].astype(F32)
        v = v_ref[rows, :]
        dn = (((0,), (0,)), ((), ()))
        kvf = lax.dot_general((k * kdec_f).astype(BF16), v, dn, preferred_element_type=F32)
        kvb = lax.dot_general((k * kdec_b).astype(BF16), v, dn, preferred_element_type=F32)
        st_ref[n, 0:LANES, :] = jnp.where(same_head, kvf, 0.0)
        st_ref[n, LANES:2 * LANES, :] = jnp.where(same_head, kvb, 0.0)
        return carry

    lax.fori_loop(0, nc, kv_body, 0, unroll=RET_UNROLL)

    def scan_fwd(n, state):
        kv = st_ref[n, 0:LANES, :]
        st_ref[n, 0:LANES, :] = state
        return state * cdec_f + kv

    lax.fori_loop(0, nc, scan_fwd, jnp.zeros((LANES, pv), F32))

    def scan_bwd(i, state):
        n = nc - 1 - i
        kv = st_ref[n, LANES:2 * LANES, :]
        st_ref[n, LANES:2 * LANES, :] = state
        return state * cdec_b + kv

    lax.fori_loop(0, nc, scan_bwd, jnp.zeros((LANES, pv), F32))

    def out_body(n, carry):
        rows = chunk_rows(n)
        q = q_ref[rows, :].astype(F32)
        k = k_ref[rows, :]
        v = v_ref[rows, :]
        zero = jnp.zeros_like(k)
        k_by_head = jnp.concatenate([jnp.where(first_qk_head, k, zero), jnp.where(first_qk_head, zero, k)], axis=0)
        s = lax.dot_general(q.astype(BF16), k_by_head, (((1,), (1,)), ((), ())),
                            preferred_element_type=F32)
        zero_v = jnp.zeros_like(v)
        v_by_head = jnp.concatenate([jnp.where(first_v_head, v, zero_v), jnp.where(first_v_head, zero_v, v)], axis=0)
        lhs = jnp.concatenate([s * intra_decay, q * qdec_f, q * qdec_b], axis=1).astype(BF16)
        rhs = jnp.concatenate([v_by_head, st_ref[n].astype(BF16)], axis=0)
        o_pair = jnp.dot(lhs, rhs, preferred_element_type=F32)
        for hh in range(2):
            o = o_pair[:, hh * B_VAL:(hh + 1) * B_VAL]
            o = o * lax.rsqrt(jnp.mean(o * o, axis=-1, keepdims=True) + EPS)
            o = o * sg_ref[rows, hh * B_VAL:(hh + 1) * B_VAL].astype(F32)
            o_ref[rows, hh * B_VAL:(hh + 1) * B_VAL] = o.astype(BF16)
        return carry

    lax.fori_loop(0, nc, out_body, 0, unroll=RET_UNROLL)


def _retention(log_gamma, qb, kb, vb, sgb, batch, seq):
    pairs = B_HEADS // 2
    qk_spec = pl.BlockSpec((None, seq, LANES), lambda b, j: (b, 0, j))
    v_spec = pl.BlockSpec((None, seq, 2 * B_VAL), lambda b, j: (b, 0, j))
    out = pl.pallas_call(
        functools.partial(_retention_kernel, seq=seq), grid=(batch, pairs),
        in_specs=[pl.BlockSpec(memory_space=pltpu.SMEM), qk_spec, qk_spec, v_spec, v_spec],
        out_specs=v_spec, out_shape=jax.ShapeDtypeStruct((batch, seq, B_VW), BF16),
        scratch_shapes=[pltpu.VMEM((seq // CHUNK, 2 * LANES, 2 * B_VAL), F32)],
        compiler_params=_params(("parallel", "parallel")), name="retention",
    )(log_gamma, qb.reshape(batch, seq, B_QKW), kb.reshape(batch, seq, B_QKW),
      vb.reshape(batch, seq, B_VW), sgb.reshape(batch, seq, B_VW))
    return out.reshape(batch * seq, B_VW)


def _mid_proj_kernel(x_ref, ma_ref, mb_ref, wo_ref, g_ref, wc_ref, x1_ref, q_ref, k_ref, v_ref, sg_ref):
    part_rows = x_ref.shape[0] // PROJ_SPLIT
    for part in range(PROJ_SPLIT):
        rows = slice(part * part_rows, (part + 1) * part_rows)
        y = jnp.dot(ma_ref[rows, :], wo_ref[0:A_W, :], preferred_element_type=F32)
        y += jnp.dot(mb_ref[rows, :], wo_ref[A_W:A_W + B_VW, :], preferred_element_type=F32)
        x1 = x_ref[rows, :] + y
        x1_ref[rows, :] = x1
        xn = _rmsnorm_rows(x1, g_ref[...]).astype(BF16)

        def proj(lo, hi):
            return jnp.dot(xn, wc_ref[:, lo:hi], preferred_element_type=F32)

        sg_ref[rows, :] = _silu(proj(C_G, C_IN)).astype(BF16)
        zkv = proj(C_K, C_G)
        k_ref[0, rows, :] = zkv[:, :HEAD_DIM].astype(BF16)
        k_ref[1, rows, :] = zkv[:, HEAD_DIM:LANES].astype(BF16)
        _transposed_chunks_store(v_ref, zkv[:, LANES:], part * part_rows)
        q_ref[rows, :] = (proj(C_Q, C_K) * (HEAD_DIM ** -0.5 * LOG2E)).astype(BF16)


def _mid_proj(x2d, ma, mb, wo, g, wc):
    rows = x2d.shape[0]
    tm = ROW_TILE
    row = lambda i: (i, 0)
    const = lambda i: (0, 0)
    kvrow = lambda i: (0, i, 0)
    out_shape = (
        jax.ShapeDtypeStruct((rows, D_MODEL), F32), jax.ShapeDtypeStruct((rows, C_W), BF16),
        jax.ShapeDtypeStruct((C_KV, rows, HEAD_DIM), BF16), jax.ShapeDtypeStruct((rows // BLOCK, C_KVW, BLOCK), BF16),
        jax.ShapeDtypeStruct((rows, C_W), BF16),
    )
    out_specs = (
        pl.BlockSpec((tm, D_MODEL), row), pl.BlockSpec((tm, C_W), row),
        pl.BlockSpec((C_KV, tm, HEAD_DIM), kvrow), pl.BlockSpec((tm // BLOCK, C_KVW, BLOCK), lambda i: (i, 0, 0)),
        pl.BlockSpec((tm, C_W), row),
    )
    in_specs = [
        pl.BlockSpec((tm, D_MODEL), row), pl.BlockSpec((tm, A_W), row), pl.BlockSpec((tm, B_VW), row),
        pl.BlockSpec((A_W + B_VW, D_MODEL), const), pl.BlockSpec((1, D_MODEL), const),
        pl.BlockSpec((D_MODEL, C_IN), const),
    ]
    return pl.pallas_call(
        _mid_proj_kernel, grid=(rows // tm,), in_specs=in_specs, out_specs=out_specs,
        out_shape=out_shape, compiler_params=_params(("parallel",)), name="mid_proj",
    )(x2d, ma, mb, wo, g, wc)


def _win_attn_kernel(q_ref, kp_ref, k_ref, kn_ref, vp_ref, v_ref, vn_ref, bias_ref, sink_ref, sg_ref, o_ref,
                     kw_all, vt_all, s_buf, p_buf, *, blocks_per_seq):
    j = pl.program_id(1)
    nblk = q_ref.shape[0] // BLOCK
    gw = C_GROUP * HEAD_DIM
    for g in range(C_KV):
        kw_all[g, 0:BLOCK] = kp_ref[g]
        kw_all[g, BLOCK:(nblk + 1) * BLOCK] = k_ref[g]
        kw_all[g, (nblk + 1) * BLOCK:(nblk + 2) * BLOCK] = kn_ref[g]
    vt_all[0] = vp_ref[0]
    vt_all[1:nblk + 1] = v_ref[...]
    vt_all[nblk + 1] = vn_ref[0]

    def block_rows(i):
        return pl.ds(pl.multiple_of(i * BLOCK, BLOCK), BLOCK)

    def scores(i, g):
        qs = _stack_heads_on_rows(q_ref[block_rows(i), g * gw:(g + 1) * gw], C_GROUP)
        kw = kw_all[g, pl.ds(pl.multiple_of(i * BLOCK, BLOCK), 3 * BLOCK), :]
        s_buf[g] = lax.dot_general(kw, qs, (((1,), (1,)), ((), ())), preferred_element_type=F32)

    def softmax(i, g):
        block_in_seq = j * nblk + i
        prev_bias = jnp.where(block_in_seq == 0, 3, 0)
        next_bias = jnp.where(block_in_seq == blocks_per_seq - 1, 3, 2)
        s_t = jnp.concatenate([s_buf[g, 0:BLOCK] + bias_ref[prev_bias, g],
                               s_buf[g, BLOCK:2 * BLOCK] + bias_ref[1, g],
                               s_buf[g, 2 * BLOCK:3 * BLOCK] + bias_ref[next_bias, g]], axis=0)
        sink = sink_ref[g]
        mx = jnp.maximum(jnp.max(s_t, axis=0, keepdims=True), sink)
        p_buf[g] = jnp.exp2(s_t - mx).astype(BF16)
        return jnp.exp2(sink - mx)

    def finish(i, g, sink_term):
        heads = slice(g * HEAD_DIM, (g + 1) * HEAD_DIM)
        vw_t = jnp.concatenate([vt_all[i, heads, :], vt_all[i + 1, heads, :], vt_all[i + 2, heads, :]], axis=1)
        acc = jnp.dot(_with_ones_rows(vw_t), p_buf[g], preferred_element_type=F32)
        denom = acc[HEAD_DIM:HEAD_DIM + 1] + sink_term
        o = _unstack_transposed(acc[:HEAD_DIM] / denom, C_GROUP, BLOCK)
        gate = sg_ref[block_rows(i), g * gw:(g + 1) * gw].astype(F32)
        o_ref[block_rows(i), g * gw:(g + 1) * gw] = (o * gate).astype(BF16)

    scores(0, 0)
    scores(0, 1)

    def body(step, sink_term0):
        for u in range(WIN_UNROLL):
            i = step * WIN_UNROLL + u
            nxt = jnp.minimum(i + 1, nblk - 1)
            scores(nxt, 0)
            sink_term1 = softmax(i, 1)
            finish(i, 0, sink_term0)
            scores(nxt, 1)
            sink_term0 = softmax(nxt, 0)
            finish(i, 1, sink_term1)
        return sink_term0

    lax.fori_loop(0, nblk // WIN_UNROLL, body, softmax(0, 0))


def _win_attn(q, k, vt, bias, sink, sg, batch, seq):
    nb = seq // BLOCK
    sup = WIN_SUPER
    nblk = sup // BLOCK
    q3 = q.reshape(batch, seq, C_W)
    sg3 = sg.reshape(batch, seq, C_W)
    k4 = k.reshape(C_KV, batch, seq, HEAD_DIM)
    rowspec = pl.BlockSpec((None, sup, C_W), lambda b, j: (b, j, 0))
    before = lambda j: jnp.maximum(j * nblk - 1, 0)
    after = lambda j: jnp.minimum((j + 1) * nblk, nb - 1)
    halo_k = (C_KV, None, BLOCK, HEAD_DIM)
    halo_v = (1, C_KVW, BLOCK)
    out = pl.pallas_call(
        functools.partial(_win_attn_kernel, blocks_per_seq=nb), grid=(batch, seq // sup),
        in_specs=[rowspec,
                  pl.BlockSpec(halo_k, lambda b, j: (0, b, before(j), 0)),
                  pl.BlockSpec((C_KV, None, sup, HEAD_DIM), lambda b, j: (0, b, j, 0)),
                  pl.BlockSpec(halo_k, lambda b, j: (0, b, after(j), 0)),
                  pl.BlockSpec(halo_v, lambda b, j: (b * nb + before(j), 0, 0)),
                  pl.BlockSpec((nblk, C_KVW, BLOCK), lambda b, j: (b * (nb // nblk) + j, 0, 0)),
                  pl.BlockSpec(halo_v, lambda b, j: (b * nb + after(j), 0, 0)),
                  pl.BlockSpec((4, C_KV, BLOCK, C_GROUP * BLOCK), lambda b, j: (0, 0, 0, 0)),
                  pl.BlockSpec((C_KV, 1, C_GROUP * BLOCK), lambda b, j: (0, 0, 0)), rowspec],
        out_specs=rowspec, out_shape=jax.ShapeDtypeStruct((batch, seq, C_W), BF16),
        scratch_shapes=[pltpu.VMEM((C_KV, sup + 2 * BLOCK, HEAD_DIM), BF16),
                        pltpu.VMEM((nblk + 2, C_KVW, BLOCK), BF16),
                        pltpu.VMEM((C_KV, 3 * BLOCK, C_GROUP * BLOCK), F32),
                        pltpu.VMEM((C_KV, 3 * BLOCK, C_GROUP * BLOCK), BF16)],
        compiler_params=_params(("parallel", "parallel")), name="win_attn",
    )(q3, k4, k4, k4, vt, vt, vt, bias, sink, sg3)
    return out.reshape(batch * seq, C_W)


def _out_proj_kernel(x_ref, m_ref, wo_ref, g_ref, y_ref):
    part_rows = x_ref.shape[0] // PROJ_SPLIT
    for part in range(PROJ_SPLIT):
        rows = slice(part * part_rows, (part + 1) * part_rows)
        x2 = x_ref[rows, :] + jnp.dot(m_ref[rows, :], wo_ref[...], preferred_element_type=F32)
        y_ref[rows, :] = _rmsnorm_rows(x2, g_ref[...])


def _out_proj(x2d, m, wo, g):
    rows = x2d.shape[0]
    tm = ROW_TILE
    row = lambda i: (i, 0)
    const = lambda i: (0, 0)
    return pl.pallas_call(
        _out_proj_kernel, grid=(rows // tm,),
        in_specs=[pl.BlockSpec((tm, D_MODEL), row), pl.BlockSpec((tm, C_W), row),
                  pl.BlockSpec((C_W, D_MODEL), const), pl.BlockSpec((1, D_MODEL), const)],
        out_specs=pl.BlockSpec((tm, D_MODEL), row), out_shape=jax.ShapeDtypeStruct((rows, D_MODEL), F32),
        compiler_params=_params(("parallel",)), name="out_proj",
    )(x2d, m, wo, g)


def _trunk(x, prm):
    batch, seq, _ = x.shape
    x2d = x.reshape(batch * seq, D_MODEL)
    qa, ka, va, sga, qb, kb, vb, sgb = _inproj_ab(
        x2d, seq, prm["g0"], prm["w_in_ab"], prm["gain"], prm["ones"], prm["tabs"])
    ma = _dense_attn(qa, ka, va, sga, batch, seq)
    mb = _retention(prm["log_gamma"], qb, kb, vb, sgb, batch, seq)
    x1, qc, kc, vc, sgc = _mid_proj(x2d, ma, mb, prm["w_out_ab"], prm["g1"], prm["w_in_c"])
    mc = _win_attn(qc, kc, vc, prm["bias"], prm["sink"], sgc, batch, seq)
    y = _out_proj(x1, mc, prm["w_out_c"], prm["gf"])
    return y.reshape(batch, seq, D_MODEL)


def kernel(x_prompt, x_sample, norm_g, w_in_ab, qk_norm_a, ret_decay, w_out_ab, w_in_c, sink_c, w_out_c,
           rel_bias, final_norm):
    assert w_in_ab.shape[0] == 1 and w_in_c.shape[0] == 1, "two-layer trunk: one layer of each kind"
    max_seq = max(x_prompt.shape[1], x_sample.shape[1])

    w_ab = w_in_ab[0].astype(BF16)
    gain = jnp.stack([jnp.tile(qk_norm_a[0, 0], 2) * (HEAD_DIM ** -0.5 * LOG2E),
                      jnp.tile(qk_norm_a[0, 1], 2)]).astype(F32)
    ones = np.kron(np.eye(2 * LANES // HEAD_DIM), np.ones((HEAD_DIM, HEAD_DIM)))

    rel = np.arange(3 * BLOCK)[:, None] - BLOCK - np.arange(BLOCK)[None, :]
    bucket_onehot = jax.nn.one_hot(jnp.asarray(_t5_bucket(rel)), REL_BUCKETS, dtype=F32)
    bias = jnp.einsum("kqb,bh->kqh", bucket_onehot, rel_bias.astype(F32) * LOG2E,
                      precision=lax.Precision.HIGHEST)
    bias = jnp.where(jnp.asarray(np.abs(rel) <= WINDOW)[:, :, None], bias, NEG)
    bias = bias.reshape(3, BLOCK, BLOCK, C_KV, C_GROUP).transpose(0, 3, 1, 4, 2)
    bias = bias.reshape(3, C_KV, BLOCK, C_GROUP * BLOCK)
    bias = jnp.concatenate([bias, jnp.full((1,) + bias.shape[1:], NEG, F32)], axis=0)
    sink = jnp.repeat(sink_c[0].astype(F32) * LOG2E, BLOCK).reshape(C_KV, 1, C_GROUP * BLOCK)

    prm = {
        "g0": norm_g[0].reshape(1, D_MODEL), "g1": norm_g[1].reshape(1, D_MODEL),
        "gf": final_norm.reshape(1, D_MODEL),
        "w_in_ab": w_ab, "w_out_ab": w_out_ab[0].astype(BF16),
        "w_in_c": w_in_c[0].astype(BF16), "w_out_c": w_out_c[0].astype(BF16),
        "gain": gain, "ones": jnp.asarray(ones, BF16),
        "tabs": _rope_tables(_axial_angles(max_seq)) + _rope_tables(_linear_angles(max_seq)),
        "log_gamma": -jnp.exp(ret_decay[0].astype(F32)),
        "bias": bias, "sink": sink,
    }
    return _trunk(x_prompt, prm), _trunk(x_sample, prm)
```

```python
import functools

import jax
import jax.numpy as jnp
import numpy as np
from jax import lax
from jax.experimental import pallas as pl
from jax.experimental.pallas import tpu as pltpu

F32 = jnp.float32
BF16 = jnp.bfloat16

D_MODEL = 1024
GRID_W = 64
EPS = 1e-6
ROPE_THETA = 10000.0
HEAD_DIM = 64
HALF = HEAD_DIM // 2
LANES = 128

A_HEADS, A_KV = 8, 2
A_GROUP = A_HEADS // A_KV
A_W, A_KVW = A_HEADS * HEAD_DIM, A_KV * HEAD_DIM
B_HEADS, B_VAL = 4, 128
B_QKW, B_VW = B_HEADS * HEAD_DIM, B_HEADS * B_VAL
C_HEADS, C_KV = 16, 2
C_GROUP = C_HEADS // C_KV
C_W, C_KVW = C_HEADS * HEAD_DIM, C_KV * HEAD_DIM
BLOCK = 128
WINDOW = 128
REL_BUCKETS, REL_MAX_DIST = 32, 128
CHUNK = 128
RET_UNROLL = 8
NEG = -1e30
LOG2E = float(np.log2(np.e))
ONES_ROWS = 16

AB_QA, AB_KA, AB_VA, AB_GA = 0, A_W, A_W + A_KVW, A_W + 2 * A_KVW
AB_QB = AB_GA + A_W
AB_KB, AB_VB, AB_GB = AB_QB + B_QKW, AB_QB + 2 * B_QKW, AB_QB + 2 * B_QKW + B_VW
AB_IN = AB_GB + B_VW
C_Q, C_K, C_V, C_G = 0, C_W, C_W + C_KVW, C_W + 2 * C_KVW
C_IN = C_G + C_W

PROJ_SPLIT = 2
ROW_TILE = 1024
ATT_TQ = 256
ATT_TK_MAX = 1024
ATT_SEQ_CHUNKS = 4
ATT_BODY_CHUNKS = 16
WIN_SUPER = 2048
WIN_UNROLL = 8
VMEM_LIMIT = 48 * 1024 * 1024


def _att_tk(seq):
    return min(ATT_TK_MAX, seq // ATT_SEQ_CHUNKS)


def _att_tq(seq):
    del seq
    return ATT_TQ


def _params(sem, flags=None):
    return pltpu.CompilerParams(dimension_semantics=sem, vmem_limit_bytes=VMEM_LIMIT, flags=flags)


def _rope_tables(ang):
    c, s = np.repeat(np.cos(ang), 2, axis=-1), np.repeat(np.sin(ang), 2, axis=-1)
    s[:, 0::2] *= -1.0
    return jnp.asarray(np.tile(c, (1, 2)), F32), jnp.asarray(np.tile(s, (1, 2)), F32)


def _axial_angles(n):
    t = np.arange(n)
    quarter = HEAD_DIM // 4
    freqs = ROPE_THETA ** (-np.arange(quarter, dtype=np.float64) / quarter)
    row = (t // GRID_W).astype(np.float64)
    col = (t % GRID_W).astype(np.float64)
    return np.concatenate([row[:, None] * freqs, col[:, None] * freqs], axis=-1)


def _linear_angles(n):
    freqs = ROPE_THETA ** (-np.arange(HALF, dtype=np.float64) / HALF)
    return np.arange(n, dtype=np.float64)[:, None] * freqs


def _t5_bucket(rel):
    half = REL_BUCKETS // 2
    max_exact = half // 2
    ret = (rel > 0).astype(np.int32) * half
    dist = np.abs(rel)
    large = max_exact + (np.log(np.maximum(dist, 1) / max_exact) / np.log(REL_MAX_DIST / max_exact)
                         * (half - max_exact)).astype(np.int32)
    large = np.minimum(large, half - 1)
    return ret + np.where(dist < max_exact, dist, large)


def _rmsnorm_rows(x, gain):
    return x * lax.rsqrt(jnp.mean(x * x, axis=-1, keepdims=True) + EPS) * gain


def _silu(z):
    return z * jax.nn.sigmoid(z)


def _rotate_pairs(x, cos_t, sin_t):
    lane = lax.broadcasted_iota(jnp.int32, (1, LANES), 1)
    partner = jnp.where(lane % 2 == 0, pltpu.roll(x, LANES - 1, 1), pltpu.roll(x, 1, 1))
    return x * cos_t + partner * sin_t


def _transposed_chunks_store(ref, tile, row0=0):
    chunk = ref.shape[2]
    rows = tile.shape[0]
    tile_t = tile.T
    step = min(chunk, rows)
    for s in range(0, rows, step):
        c, off = divmod(row0 + s, chunk)
        ref[c, :, off:off + step] = tile_t[:, s:s + step].astype(ref.dtype)


def _stack_heads_on_rows(q, heads):
    return jnp.concatenate([q[:, h * HEAD_DIM:(h + 1) * HEAD_DIM] for h in range(heads)], axis=0)


def _unstack_transposed(o_t, heads, tokens):
    by_head = jnp.concatenate([o_t[:, h * tokens:(h + 1) * tokens] for h in range(heads)], axis=0)
    return by_head.T


def _with_ones_rows(v_t):
    return jnp.concatenate([v_t, jnp.ones((ONES_ROWS, v_t.shape[1]), v_t.dtype)], axis=0)


def _inproj_ab_kernel(x_ref, g_ref, w_ref, gain_ref, ones_ref, cosa_ref, sina_ref, cosb_ref, sinb_ref,
                      qa_ref, ka_ref, va_ref, sga_ref, qb_ref, kb_ref, vb_ref, sgb_ref):
    part_rows = x_ref.shape[0] // PROJ_SPLIT
    for part in range(PROJ_SPLIT):
        rows = slice(part * part_rows, (part + 1) * part_rows)
        xn = _rmsnorm_rows(x_ref[rows, :], g_ref[...]).astype(BF16)

        def proj(lo, hi):
            return jnp.dot(xn, w_ref[:, lo:hi], preferred_element_type=F32)

        cosa, sina = cosa_ref[rows, :], sina_ref[rows, :]
        cosb, sinb = cosb_ref[rows, :], sinb_ref[rows, :]

        def head_norm_rope(z, gain):
            width = z.shape[1]
            ss = jnp.dot((z * z).astype(BF16), ones_ref[0:width, 0:width], preferred_element_type=F32)
            z = z * lax.rsqrt(ss * (1.0 / HEAD_DIM) + EPS)
            return [_rotate_pairs(z[:, c * LANES:(c + 1) * LANES] * gain, cosa, sina)
                    for c in range(width // LANES)]

        zq = proj(AB_QA, AB_KA)
        for pair in range(A_W // (2 * LANES)):
            tiles = head_norm_rope(zq[:, pair * 2 * LANES:(pair + 1) * 2 * LANES], gain_ref[0:1, :])
            for c, tile in enumerate(tiles):
                lo = (2 * pair + c) * LANES
                qa_ref[rows, lo:lo + LANES] = tile.astype(BF16)
        zkv = proj(AB_KA, AB_GA)
        k_tile, = head_norm_rope(zkv[:, :LANES], gain_ref[1:2, :])
        ka_ref[0, rows, :] = k_tile[:, :HEAD_DIM].astype(BF16)
        ka_ref[1, rows, :] = k_tile[:, HEAD_DIM:].astype(BF16)
        _transposed_chunks_store(va_ref, zkv[:, LANES:], part * part_rows)
        sga_ref[rows, :] = _silu(proj(AB_GA, AB_QB)).astype(BF16)
        sgb_ref[rows, :] = _silu(proj(AB_GB, AB_IN)).astype(BF16)

        zqk = proj(AB_QB, AB_VB)
        for c in range(B_QKW // LANES):
            qb_ref[rows, c * LANES:(c + 1) * LANES] = _rotate_pairs(
                zqk[:, c * LANES:(c + 1) * LANES], cosb, sinb).astype(BF16)
            kt = zqk[:, B_QKW + c * LANES:B_QKW + (c + 1) * LANES] * (HEAD_DIM ** -0.5)
            kb_ref[rows, c * LANES:(c + 1) * LANES] = _rotate_pairs(kt, cosb, sinb).astype(BF16)
        vb_ref[rows, :] = proj(AB_VB, AB_GB).astype(BF16)


def _inproj_ab(x2d, seq, g, w, gain, ones, tabs):
    rows = x2d.shape[0]
    tm = ROW_TILE
    per_seq = seq // tm
    row = lambda i: (i, 0)
    const = lambda i: (0, 0)
    tab = lambda i: (i % per_seq, 0)
    kvrow = lambda i: (0, i, 0)
    tk = _att_tk(seq)
    vt_chunks = tm // tk
    out_shape = (
        jax.ShapeDtypeStruct((rows, A_W), BF16), jax.ShapeDtypeStruct((A_KV, rows, HEAD_DIM), BF16),
        jax.ShapeDtypeStruct((rows // tk, A_KVW, tk), BF16), jax.ShapeDtypeStruct((rows, A_W), BF16),
        jax.ShapeDtypeStruct((rows, B_QKW), BF16), jax.ShapeDtypeStruct((rows, B_QKW), BF16),
        jax.ShapeDtypeStruct((rows, B_VW), BF16), jax.ShapeDtypeStruct((rows, B_VW), BF16),
    )
    out_specs = (
        pl.BlockSpec((tm, A_W), row), pl.BlockSpec((A_KV, tm, HEAD_DIM), kvrow),
        pl.BlockSpec((vt_chunks, A_KVW, tk), lambda i: (i, 0, 0)), pl.BlockSpec((tm, A_W), row),
        pl.BlockSpec((tm, B_QKW), row), pl.BlockSpec((tm, B_QKW), row),
        pl.BlockSpec((tm, B_VW), row), pl.BlockSpec((tm, B_VW), row),
    )
    in_specs = [
        pl.BlockSpec((tm, D_MODEL), row), pl.BlockSpec((1, D_MODEL), const),
        pl.BlockSpec((D_MODEL, AB_IN), const), pl.BlockSpec((2, LANES), const),
        pl.BlockSpec((2 * LANES, 2 * LANES), const),
    ] + [pl.BlockSpec((tm, LANES), tab)] * 4
    return pl.pallas_call(
        _inproj_ab_kernel, grid=(rows // tm,), in_specs=in_specs, out_specs=out_specs,
        out_shape=out_shape, compiler_params=_params(("parallel",)), name="inproj_ab",
    )(x2d, g, w, gain, ones, *tabs)


def _dense_attn_kernel(q_ref, k_ref, vt_ref, sg_ref, o_ref, s_buf, p_buf, *, seq):
    tq, tk = _att_tq(seq), _att_tk(seq)
    rows = A_GROUP * tq
    n = seq // tk
    blocks = seq // tq
    assert n % 2 == 0 and n >= 2

    def query_rows(i):
        return pl.ds(pl.multiple_of(i * tq, tq), tq)

    def stacked_q(i):
        return _stack_heads_on_rows(q_ref[query_rows(i), :], A_GROUP)

    def scores(qs, c):
        k = k_ref[c * tk:(c + 1) * tk, :]
        s_t = lax.dot_general(k, qs, (((1,), (1,)), ((), ())), preferred_element_type=F32)
        s_buf[c % 2] = s_t
        return jnp.max(s_t, axis=0, keepdims=True)

    def softmax(m, c, chunk_max):
        m_new = jnp.maximum(m, chunk_max)
        p_buf[c % 2] = jnp.exp2(s_buf[c % 2] - m_new).astype(BF16)
        return m_new, jnp.exp2(m - m_new)

    fresh_m = jnp.full((1, rows), NEG, F32)
    first_q = stacked_q(0)
    first_max = [scores(first_q, 0), scores(first_q, 1)]

    def block(i, m, alpha, max0, max1):
        chunk_max = [max0, max1]
        qs = stacked_q(i)
        qs_next = stacked_q(jnp.minimum(i + 1, blocks - 1))
        acc = jnp.zeros((HEAD_DIM + ONES_ROWS, rows), F32)
        for t in range(n):
            if t + 2 < n:
                scored = scores(qs, t + 2)
            else:
                scored = scores(qs_next, t + 2 - n)
            if t + 1 < n:
                m, alpha_next = softmax(m, t + 1, chunk_max[(t + 1) % 2])
            else:
                m, alpha_next = softmax(fresh_m, 0, chunk_max[0])
            chunk_max[t % 2] = scored
            pv = jnp.dot(_with_ones_rows(vt_ref[t]), p_buf[t % 2], preferred_element_type=F32)
            acc = alpha * acc + pv
            alpha = alpha_next
        o_t = acc[:HEAD_DIM] / acc[HEAD_DIM:HEAD_DIM + 1]
        o = _unstack_transposed(o_t, A_GROUP, tq)
        o_ref[query_rows(i), :] = (o * sg_ref[query_rows(i), :].astype(F32)).astype(BF16)
        return m, alpha, chunk_max[0], chunk_max[1]

    per_body = max(1, ATT_BODY_CHUNKS // n)

    def body(step, carry):
        for u in range(per_body):
            carry = block(step * per_body + u, *carry)
        return carry

    lax.fori_loop(0, blocks // per_body, body, softmax(fresh_m, 0, first_max[0]) + tuple(first_max))


def _dense_attn(qa, ka, vat, sga, batch, seq):
    gw = A_GROUP * HEAD_DIM
    q3 = qa.reshape(batch, seq, A_W)
    sg3 = sga.reshape(batch, seq, A_W)
    k4 = ka.reshape(A_KV, batch, seq, HEAD_DIM)
    tk = _att_tk(seq)
    vt4 = vat.reshape(batch, seq // tk, A_KVW, tk)
    qspec = pl.BlockSpec((None, seq, gw), lambda b, g: (b, 0, g))
    kspec = pl.BlockSpec((None, None, seq, HEAD_DIM), lambda b, g: (g, b, 0, 0))
    vtspec = pl.BlockSpec((None, seq // tk, HEAD_DIM, tk), lambda b, g: (b, 0, g, 0))
    out = pl.pallas_call(
        functools.partial(_dense_attn_kernel, seq=seq), grid=(batch, A_KV),
        in_specs=[qspec, kspec, vtspec, qspec], out_specs=qspec,
        out_shape=jax.ShapeDtypeStruct((batch, seq, A_W), BF16),
        scratch_shapes=[pltpu.VMEM((2, tk, A_GROUP * _att_tq(seq)), F32),
                        pltpu.VMEM((2, tk, A_GROUP * _att_tq(seq)), BF16)],
        compiler_params=_params(("parallel", "parallel")), name="dense_attn",
    )(q3, k4, vt4, sg3)
    return out.reshape(batch * seq, A_W)


def _retention_kernel(lg_ref, q_ref, k_ref, v_ref, sg_ref, o_ref, st_ref, *, seq):
    c = CHUNK
    nc = seq // c
    pv = 2 * B_VAL
    pair = pl.program_id(1)
    lgf = [lg_ref[0, 2 * pair + hh] for hh in range(2)]
    lgb = [lg_ref[1, 2 * pair + hh] for hh in range(2)]
    lane = lax.broadcasted_iota(jnp.int32, (1, LANES), 1)
    first_qk_head = lane < HEAD_DIM
    lane_lgf = jnp.where(first_qk_head, lgf[0], lgf[1])
    lane_lgb = jnp.where(first_qk_head, lgb[0], lgb[1])
    t = lax.broadcasted_iota(jnp.int32, (c, 1), 0).astype(F32)
    kdec_f = jnp.exp((c - 1.0 - t) * lane_lgf)
    kdec_b = jnp.exp(t * lane_lgb)
    qdec_f = jnp.exp((t + 1.0) * lane_lgf)
    qdec_b = jnp.exp((c - t) * lane_lgb)
    row_first = lax.broadcasted_iota(jnp.int32, (LANES, 1), 0) < HEAD_DIM
    first_v_head = lax.broadcasted_iota(jnp.int32, (1, pv), 1) < B_VAL
    same_head = row_first == first_v_head
    cdec_f = jnp.exp(c * jnp.where(row_first, lgf[0], lgf[1]))
    cdec_b = jnp.exp(c * jnp.where(row_first, lgb[0], lgb[1]))
    diff = (lax.broadcasted_iota(jnp.int32, (c, c), 0) - lax.broadcasted_iota(jnp.int32, (c, c), 1)).astype(F32)
    intra_decay = jnp.concatenate(
        [jnp.where(diff >= 0, jnp.exp(lgf[hh] * jnp.maximum(diff, 0.0)), jnp.exp(lgb[hh] * jnp.maximum(-diff, 0.0)))
         for hh in range(2)], axis=1)

    def chunk_rows(n):
        return pl.ds(pl.multiple_of(n * c, c), c)

    def kv_body(n, carry):
        rows = chunk_rows(n)
        k = k_ref[rows, :].astype(F32)
        v = v_ref[rows, :]
        dn = (((0,), (0,)), ((), ()))
        kvf = lax.dot_general((k * kdec_f).astype(BF16), v, dn, preferred_element_type=F32)
        kvb = lax.dot_general((k * kdec_b).astype(BF16), v, dn, preferred_element_type=F32)
        st_ref[n, 0:LANES, :] = jnp.where(same_head, kvf, 0.0)
        st_ref[n, LANES:2 * LANES, :] = jnp.where(same_head, kvb, 0.0)
        return carry

    lax.fori_loop(0, nc, kv_body, 0, unroll=RET_UNROLL)

    def scan_fwd(n, state):
        kv = st_ref[n, 0:LANES, :]
        st_ref[n, 0:LANES, :] = state
        return state * cdec_f + kv

    lax.fori_loop(0, nc, scan_fwd, jnp.zeros((LANES, pv), F32))

    def scan_bwd(i, state):
        n = nc - 1 - i
        kv = st_ref[n, LANES:2 * LANES, :]
        st_ref[n, LANES:2 * LANES, :] = state
        return state * cdec_b + kv

    lax.fori_loop(0, nc, scan_bwd, jnp.zeros((LANES, pv), F32))

    def out_body(n, carry):
        rows = chunk_rows(n)
        q = q_ref[rows, :].astype(F32)
        k = k_ref[rows, :]
        v = v_ref[rows, :]
        zero = jnp.zeros_like(k)
        k_by_head = jnp.concatenate([jnp.where(first_qk_head, k, zero), jnp.where(first_qk_head, zero, k)], axis=0)
        s = lax.dot_general(q.astype(BF16), k_by_head, (((1,), (1,)), ((), ())),
                            preferred_element_type=F32)
        zero_v = jnp.zeros_like(v)
        v_by_head = jnp.concatenate([jnp.where(first_v_head, v, zero_v), jnp.where(first_v_head, zero_v, v)], axis=0)
        lhs = jnp.concatenate([s * intra_decay, q * qdec_f, q * qdec_b], axis=1).astype(BF16)
        rhs = jnp.concatenate([v_by_head, st_ref[n].astype(BF16)], axis=0)
        o_pair = jnp.dot(lhs, rhs, preferred_element_type=F32)
        for hh in range(2):
            o = o_pair[:, hh * B_VAL:(hh + 1) * B_VAL]
            o = o * lax.rsqrt(jnp.mean(o * o, axis=-1, keepdims=True) + EPS)
            o = o * sg_ref[rows, hh * B_VAL:(hh + 1) * B_VAL].astype(F32)
            o_ref[rows, hh * B_VAL:(hh + 1) * B_VAL] = o.astype(BF16)
        return carry

    lax.fori_loop(0, nc, out_body, 0, unroll=RET_UNROLL)


def _retention(log_gamma, qb, kb, vb, sgb, batch, seq):
    pairs = B_HEADS // 2
    qk_spec = pl.BlockSpec((None, seq, LANES), lambda b, j: (b, 0, j))
    v_spec = pl.BlockSpec((None, seq, 2 * B_VAL), lambda b, j: (b, 0, j))
    out = pl.pallas_call(
        functools.partial(_retention_kernel, seq=seq), grid=(batch, pairs),
        in_specs=[pl.BlockSpec(memory_space=pltpu.SMEM), qk_spec, qk_spec, v_spec, v_spec],
        out_specs=v_spec, out_shape=jax.ShapeDtypeStruct((batch, seq, B_VW), BF16),
        scratch_shapes=[pltpu.VMEM((seq // CHUNK, 2 * LANES, 2 * B_VAL), F32)],
        compiler_params=_params(("parallel", "parallel")), name="retention",
    )(log_gamma, qb.reshape(batch, seq, B_QKW), kb.reshape(batch, seq, B_QKW),
      vb.reshape(batch, seq, B_VW), sgb.reshape(batch, seq, B_VW))
    return out.reshape(batch * seq, B_VW)


def _mid_proj_kernel(x_ref, ma_ref, mb_ref, wo_ref, g_ref, wc_ref, x1_ref, q_ref, k_ref, v_ref, sg_ref):
    part_rows = x_ref.shape[0] // PROJ_SPLIT
    for part in range(PROJ_SPLIT):
        rows = slice(part * part_rows, (part + 1) * part_rows)
        y = jnp.dot(ma_ref[rows, :], wo_ref[0:A_W, :], preferred_element_type=F32)
        y += jnp.dot(mb_ref[rows, :], wo_ref[A_W:A_W + B_VW, :], preferred_element_type=F32)
        x1 = x_ref[rows, :] + y
        x1_ref[rows, :] = x1
        xn = _rmsnorm_rows(x1, g_ref[...]).astype(BF16)

        def proj(lo, hi):
            return jnp.dot(xn, wc_ref[:, lo:hi], preferred_element_type=F32)

        sg_ref[rows, :] = _silu(proj(C_G, C_IN)).astype(BF16)
        zkv = proj(C_K, C_G)
        k_ref[0, rows, :] = zkv[:, :HEAD_DIM].astype(BF16)
        k_ref[1, rows, :] = zkv[:, HEAD_DIM:LANES].astype(BF16)
        _transposed_chunks_store(v_ref, zkv[:, LANES:], part * part_rows)
        q_ref[rows, :] = (proj(C_Q, C_K) * (HEAD_DIM ** -0.5 * LOG2E)).astype(BF16)


def _mid_proj(x2d, ma, mb, wo, g, wc):
    rows = x2d.shape[0]
    tm = ROW_TILE
    row = lambda i: (i, 0)
    const = lambda i: (0, 0)
    kvrow = lambda i: (0, i, 0)
    out_shape = (
        jax.ShapeDtypeStruct((rows, D_MODEL), F32), jax.ShapeDtypeStruct((rows, C_W), BF16),
        jax.ShapeDtypeStruct((C_KV, rows, HEAD_DIM), BF16), jax.ShapeDtypeStruct((rows // BLOCK, C_KVW, BLOCK), BF16),
        jax.ShapeDtypeStruct((rows, C_W), BF16),
    )
    out_specs = (
        pl.BlockSpec((tm, D_MODEL), row), pl.BlockSpec((tm, C_W), row),
        pl.BlockSpec((C_KV, tm, HEAD_DIM), kvrow), pl.BlockSpec((tm // BLOCK, C_KVW, BLOCK), lambda i: (i, 0, 0)),
        pl.BlockSpec((tm, C_W), row),
    )
    in_specs = [
        pl.BlockSpec((tm, D_MODEL), row), pl.BlockSpec((tm, A_W), row), pl.BlockSpec((tm, B_VW), row),
        pl.BlockSpec((A_W + B_VW, D_MODEL), const), pl.BlockSpec((1, D_MODEL), const),
        pl.BlockSpec((D_MODEL, C_IN), const),
    ]
    return pl.pallas_call(
        _mid_proj_kernel, grid=(rows // tm,), in_specs=in_specs, out_specs=out_specs,
        out_shape=out_shape, compiler_params=_params(("parallel",)), name="mid_proj",
    )(x2d, ma, mb, wo, g, wc)


def _win_attn_kernel(q_ref, kp_ref, k_ref, kn_ref, vp_ref, v_ref, vn_ref, bias_ref, sink_ref, sg_ref, o_ref,
                     kw_all, vt_all, s_buf, p_buf, *, blocks_per_seq):
    j = pl.program_id(1)
    nblk = q_ref.shape[0] // BLOCK
    gw = C_GROUP * HEAD_DIM
    for g in range(C_KV):
        kw_all[g, 0:BLOCK] = kp_ref[g]
        kw_all[g, BLOCK:(nblk + 1) * BLOCK] = k_ref[g]
        kw_all[g, (nblk + 1) * BLOCK:(nblk + 2) * BLOCK] = kn_ref[g]
    vt_all[0] = vp_ref[0]
    vt_all[1:nblk + 1] = v_ref[...]
    vt_all[nblk + 1] = vn_ref[0]

    def block_rows(i):
        return pl.ds(pl.multiple_of(i * BLOCK, BLOCK), BLOCK)

    def scores(i, g):
        qs = _stack_heads_on_rows(q_ref[block_rows(i), g * gw:(g + 1) * gw], C_GROUP)
        kw = kw_all[g, pl.ds(pl.multiple_of(i * BLOCK, BLOCK), 3 * BLOCK), :]
        s_buf[g] = lax.dot_general(kw, qs, (((1,), (1,)), ((), ())), preferred_element_type=F32)

    def softmax(i, g):
        block_in_seq = j * nblk + i
        prev_bias = jnp.where(block_in_seq == 0, 3, 0)
        next_bias = jnp.where(block_in_seq == blocks_per_seq - 1, 3, 2)
        s_t = jnp.concatenate([s_buf[g, 0:BLOCK] + bias_ref[prev_bias, g],
                               s_buf[g, BLOCK:2 * BLOCK] + bias_ref[1, g],
                               s_buf[g, 2 * BLOCK:3 * BLOCK] + bias_ref[next_bias, g]], axis=0)
        sink = sink_ref[g]
        mx = jnp.maximum(jnp.max(s_t, axis=0, keepdims=True), sink)
        p_buf[g] = jnp.exp2(s_t - mx).astype(BF16)
        return jnp.exp2(sink - mx)

    def finish(i, g, sink_term):
        heads = slice(g * HEAD_DIM, (g + 1) * HEAD_DIM)
        vw_t = jnp.concatenate([vt_all[i, heads, :], vt_all[i + 1, heads, :], vt_all[i + 2, heads, :]], axis=1)
        acc = jnp.dot(_with_ones_rows(vw_t), p_buf[g], preferred_element_type=F32)
        denom = acc[HEAD_DIM:HEAD_DIM + 1] + sink_term
        o = _unstack_transposed(acc[:HEAD_DIM] / denom, C_GROUP, BLOCK)
        gate = sg_ref[block_rows(i), g * gw:(g + 1) * gw].astype(F32)
        o_ref[block_rows(i), g * gw:(g + 1) * gw] = (o * gate).astype(BF16)

    scores(0, 0)
    scores(0, 1)

    def body(step, sink_term0):
        for u in range(WIN_UNROLL):
            i = step * WIN_UNROLL + u
            nxt = jnp.minimum(i + 1, nblk - 1)
            scores(nxt, 0)
            sink_term1 = softmax(i, 1)
            finish(i, 0, sink_term0)
            scores(nxt, 1)
            sink_term0 = softmax(nxt, 0)
            finish(i, 1, sink_term1)
        return sink_term0

    lax.fori_loop(0, nblk // WIN_UNROLL, body, softmax(0, 0))


def _win_attn(q, k, vt, bias, sink, sg, batch, seq):
    nb = seq // BLOCK
    sup = WIN_SUPER
    nblk = sup // BLOCK
    q3 = q.reshape(batch, seq, C_W)
    sg3 = sg.reshape(batch, seq, C_W)
    k4 = k.reshape(C_KV, batch, seq, HEAD_DIM)
    rowspec = pl.BlockSpec((None, sup, C_W), lambda b, j: (b, j, 0))
    before = lambda j: jnp.maximum(j * nblk - 1, 0)
    after = lambda j: jnp.minimum((j + 1) * nblk, nb - 1)
    halo_k = (C_KV, None, BLOCK, HEAD_DIM)
    halo_v = (1, C_KVW, BLOCK)
    out = pl.pallas_call(
        functools.partial(_win_attn_kernel, blocks_per_seq=nb), grid=(batch, seq // sup),
        in_specs=[rowspec,
                  pl.BlockSpec(halo_k, lambda b, j: (0, b, before(j), 0)),
                  pl.BlockSpec((C_KV, None, sup, HEAD_DIM), lambda b, j: (0, b, j, 0)),
                  pl.BlockSpec(halo_k, lambda b, j: (0, b, after(j), 0)),
                  pl.BlockSpec(halo_v, lambda b, j: (b * nb + before(j), 0, 0)),
                  pl.BlockSpec((nblk, C_KVW, BLOCK), lambda b, j: (b * (nb // nblk) + j, 0, 0)),
                  pl.BlockSpec(halo_v, lambda b, j: (b * nb + after(j), 0, 0)),
                  pl.BlockSpec((4, C_KV, BLOCK, C_GROUP * BLOCK), lambda b, j: (0, 0, 0, 0)),
                  pl.BlockSpec((C_KV, 1, C_GROUP * BLOCK), lambda b, j: (0, 0, 0)), rowspec],
        out_specs=rowspec, out_shape=jax.ShapeDtypeStruct((batch, seq, C_W), BF16),
        scratch_shapes=[pltpu.VMEM((C_KV, sup + 2 * BLOCK, HEAD_DIM), BF16),
                        pltpu.VMEM((nblk + 2, C_KVW, BLOCK), BF16),
                        pltpu.VMEM((C_KV, 3 * BLOCK, C_GROUP * BLOCK), F32),
                        pltpu.VMEM((C_KV, 3 * BLOCK, C_GROUP * BLOCK), BF16)],
        compiler_params=_params(("parallel", "parallel")), name="win_attn",
    )(q3, k4, k4, k4, vt, vt, vt, bias, sink, sg3)
    return out.reshape(batch * seq, C_W)


def _out_proj_kernel(x_ref, m_ref, wo_ref, g_ref, y_ref):
    part_rows = x_ref.shape[0] // PROJ_SPLIT
    for part in range(PROJ_SPLIT):
        rows = slice(part * part_rows, (part + 1) * part_rows)
        x2 = x_ref[rows, :] + jnp.dot(m_ref[rows, :], wo_ref[...], preferred_element_type=F32)
        y_ref[rows, :] = _rmsnorm_rows(x2, g_ref[...])


def _out_proj(x2d, m, wo, g):
    rows = x2d.shape[0]
    tm = ROW_TILE
    row = lambda i: (i, 0)
    const = lambda i: (0, 0)
    return pl.pallas_call(
        _out_proj_kernel, grid=(rows // tm,),
        in_specs=[pl.BlockSpec((tm, D_MODEL), row), pl.BlockSpec((tm, C_W), row),
                  pl.BlockSpec((C_W, D_MODEL), const), pl.BlockSpec((1, D_MODEL), const)],
        out_specs=pl.BlockSpec((tm, D_MODEL), row), out_shape=jax.ShapeDtypeStruct((rows, D_MODEL), F32),
        compiler_params=_params(("parallel",)), name="out_proj",
    )(x2d, m, wo, g)


def _trunk(x, prm):
    batch, seq, _ = x.shape
    x2d = x.reshape(batch * seq, D_MODEL)
    qa, ka, va, sga, qb, kb, vb, sgb = _inproj_ab(
        x2d, seq, prm["g0"], prm["w_in_ab"], prm["gain"], prm["ones"], prm["tabs"])
    ma = _dense_attn(qa, ka, va, sga, batch, seq)
    mb = _retention(prm["log_gamma"], qb, kb, vb, sgb, batch, seq)
    x1, qc, kc, vc, sgc = _mid_proj(x2d, ma, mb, prm["w_out_ab"], prm["g1"], prm["w_in_c"])
    mc = _win_attn(qc, kc, vc, prm["bias"], prm["sink"], sgc, batch, seq)
    y = _out_proj(x1, mc, prm["w_out_c"], prm["gf"])
    return y.reshape(batch, seq, D_MODEL)


def kernel(x_prompt, x_sample, norm_g, w_in_ab, qk_norm_a, ret_decay, w_out_ab, w_in_c, sink_c, w_out_c,
           rel_bias, final_norm):
    assert w_in_ab.shape[0] == 1 and w_in_c.shape[0] == 1, "two-layer trunk: one layer of each kind"
    max_seq = max(x_prompt.shape[1], x_sample.shape[1])

    w_ab = w_in_ab[0].astype(BF16)
    gain = jnp.stack([jnp.tile(qk_norm_a[0, 0], 2) * (HEAD_DIM ** -0.5 * LOG2E),
                      jnp.tile(qk_norm_a[0, 1], 2)]).astype(F32)
    ones = np.kron(np.eye(2 * LANES // HEAD_DIM), np.ones((HEAD_DIM, HEAD_DIM)))

    rel = np.arange(3 * BLOCK)[:, None] - BLOCK - np.arange(BLOCK)[None, :]
    bucket_onehot = jax.nn.one_hot(jnp.asarray(_t5_bucket(rel).reshape(-1)), REL_BUCKETS, dtype=F32, axis=0)
    bias = jnp.einsum("bh,bn->hn", rel_bias.astype(F32) * LOG2E, bucket_onehot,
                      precision=lax.Precision.HIGHEST)
    bias = jnp.where(jnp.asarray((np.abs(rel) <= WINDOW).reshape(1, -1)), bias, NEG)
    bias = bias.reshape(C_KV, C_GROUP, 3, BLOCK, BLOCK).transpose(2, 0, 3, 1, 4)
    bias = bias.reshape(3, C_KV, BLOCK, C_GROUP * BLOCK)
    bias = jnp.concatenate([bias, jnp.full((1,) + bias.shape[1:], NEG, F32)], axis=0)
    sink = jnp.repeat(sink_c[0].astype(F32) * LOG2E, BLOCK).reshape(C_KV, 1, C_GROUP * BLOCK)

    prm = {
        "g0": norm_g[0].reshape(1, D_MODEL), "g1": norm_g[1].reshape(1, D_MODEL),
        "gf": final_norm.reshape(1, D_MODEL),
        "w_in_ab": w_ab, "w_out_ab": w_out_ab[0].astype(BF16),
        "w_in_c": w_in_c[0].astype(BF16), "w_out_c": w_out_c[0].astype(BF16),
        "gain": gain, "ones": jnp.asarray(ones, BF16),
        "tabs": _rope_tables(_axial_angles(max_seq)) + _rope_tables(_linear_angles(max_seq)),
        "log_gamma": -jnp.exp(ret_decay[0].astype(F32)),
        "bias": bias, "sink": sink,
    }
    return _trunk(x_prompt, prm), _trunk(x_sample, prm)
```

```python
import functools

import jax
import jax.numpy as jnp
import numpy as np
from jax import lax
from jax.experimental import pallas as pl
from jax.experimental.pallas import tpu as pltpu

F32 = jnp.float32
BF16 = jnp.bfloat16

D_MODEL = 1024
GRID_W = 64
EPS = 1e-6
ROPE_THETA = 10000.0
HEAD_DIM = 64
HALF = HEAD_DIM // 2
LANES = 128

A_HEADS, A_KV = 8, 2
A_GROUP = A_HEADS // A_KV
A_W, A_KVW = A_HEADS * HEAD_DIM, A_KV * HEAD_DIM
B_HEADS, B_VAL = 4, 128
B_QKW, B_VW = B_HEADS * HEAD_DIM, B_HEADS * B_VAL
C_HEADS, C_KV = 16, 2
C_GROUP = C_HEADS // C_KV
C_W, C_KVW = C_HEADS * HEAD_DIM, C_KV * HEAD_DIM
BLOCK = 128
WINDOW = 128
REL_BUCKETS, REL_MAX_DIST = 32, 128
CHUNK = 128
RET_UNROLL = 16
NEG = -1e30
LOG2E = float(np.log2(np.e))
ONES_ROWS = 16

AB_QA, AB_KA, AB_VA, AB_GA = 0, A_W, A_W + A_KVW, A_W + 2 * A_KVW
AB_QB = AB_GA + A_W
AB_KB, AB_VB, AB_GB = AB_QB + B_QKW, AB_QB + 2 * B_QKW, AB_QB + 2 * B_QKW + B_VW
AB_IN = AB_GB + B_VW
C_Q, C_K, C_V, C_G = 0, C_W, C_W + C_KVW, C_W + 2 * C_KVW
C_IN = C_G + C_W

PROJ_SPLIT = 2
ROW_TILE = 1024
ATT_TQ = 256
ATT_TK_MAX = 1024
ATT_SEQ_CHUNKS = 4
ATT_BODY_CHUNKS = 16
WIN_SUPER = 2048
WIN_UNROLL = 16
VMEM_LIMIT = 48 * 1024 * 1024


def _att_tk(seq):
    return min(ATT_TK_MAX, seq // ATT_SEQ_CHUNKS)


def _att_tq(seq):
    del seq
    return ATT_TQ


def _params(sem, flags=None):
    return pltpu.CompilerParams(dimension_semantics=sem, vmem_limit_bytes=VMEM_LIMIT, flags=flags)


def _rope_tables(ang):
    c, s = np.repeat(np.cos(ang), 2, axis=-1), np.repeat(np.sin(ang), 2, axis=-1)
    s[:, 0::2] *= -1.0
    return jnp.asarray(np.tile(c, (1, 2)), F32), jnp.asarray(np.tile(s, (1, 2)), F32)


def _axial_angles(n):
    t = np.arange(n)
    quarter = HEAD_DIM // 4
    freqs = ROPE_THETA ** (-np.arange(quarter, dtype=np.float64) / quarter)
    row = (t // GRID_W).astype(np.float64)
    col = (t % GRID_W).astype(np.float64)
    return np.concatenate([row[:, None] * freqs, col[:, None] * freqs], axis=-1)


def _linear_angles(n):
    freqs = ROPE_THETA ** (-np.arange(HALF, dtype=np.float64) / HALF)
    return np.arange(n, dtype=np.float64)[:, None] * freqs


def _t5_bucket(rel):
    half = REL_BUCKETS // 2
    max_exact = half // 2
    ret = (rel > 0).astype(np.int32) * half
    dist = np.abs(rel)
    large = max_exact + (np.log(np.maximum(dist, 1) / max_exact) / np.log(REL_MAX_DIST / max_exact)
                         * (half - max_exact)).astype(np.int32)
    large = np.minimum(large, half - 1)
    return ret + np.where(dist < max_exact, dist, large)


def _rmsnorm_rows(x, gain):
    return x * lax.rsqrt(jnp.mean(x * x, axis=-1, keepdims=True) + EPS) * gain


def _silu(z):
    return z * jax.nn.sigmoid(z)


def _rotate_pairs(x, cos_t, sin_t):
    lane = lax.broadcasted_iota(jnp.int32, (1, LANES), 1)
    partner = jnp.where(lane % 2 == 0, pltpu.roll(x, LANES - 1, 1), pltpu.roll(x, 1, 1))
    return x * cos_t + partner * sin_t


def _transposed_chunks_store(ref, tile, row0=0):
    chunk = ref.shape[2]
    rows = tile.shape[0]
    tile_t = tile.T
    step = min(chunk, rows)
    for s in range(0, rows, step):
        c, off = divmod(row0 + s, chunk)
        ref[c, :, off:off + step] = tile_t[:, s:s + step].astype(ref.dtype)


def _stack_heads_on_rows(q, heads):
    return jnp.concatenate([q[:, h * HEAD_DIM:(h + 1) * HEAD_DIM] for h in range(heads)], axis=0)


def _unstack_transposed(o_t, heads, tokens):
    by_head = jnp.concatenate([o_t[:, h * tokens:(h + 1) * tokens] for h in range(heads)], axis=0)
    return by_head.T


def _with_ones_rows(v_t):
    return jnp.concatenate([v_t, jnp.ones((ONES_ROWS, v_t.shape[1]), v_t.dtype)], axis=0)


def _inproj_ab_kernel(x_ref, g_ref, w_ref, gain_ref, ones_ref, cosa_ref, sina_ref, cosb_ref, sinb_ref,
                      qa_ref, ka_ref, va_ref, sga_ref, qb_ref, kb_ref, vb_ref, sgb_ref):
    part_rows = x_ref.shape[0] // PROJ_SPLIT
    for part in range(PROJ_SPLIT):
        rows = slice(part * part_rows, (part + 1) * part_rows)
        xn = _rmsnorm_rows(x_ref[rows, :], g_ref[...]).astype(BF16)

        def proj(lo, hi):
            return jnp.dot(xn, w_ref[:, lo:hi], preferred_element_type=F32)

        cosa, sina = cosa_ref[rows, :], sina_ref[rows, :]
        cosb, sinb = cosb_ref[rows, :], sinb_ref[rows, :]

        def head_norm_rope(z, gain):
            width = z.shape[1]
            ss = jnp.dot((z * z).astype(BF16), ones_ref[0:width, 0:width], preferred_element_type=F32)
            z = z * lax.rsqrt(ss * (1.0 / HEAD_DIM) + EPS)
            return [_rotate_pairs(z[:, c * LANES:(c + 1) * LANES] * gain, cosa, sina)
                    for c in range(width // LANES)]

        zq = proj(AB_QA, AB_KA)
        for pair in range(A_W // (2 * LANES)):
            tiles = head_norm_rope(zq[:, pair * 2 * LANES:(pair + 1) * 2 * LANES], gain_ref[0:1, :])
            for c, tile in enumerate(tiles):
                lo = (2 * pair + c) * LANES
                qa_ref[rows, lo:lo + LANES] = tile.astype(BF16)
        zkv = proj(AB_KA, AB_GA)
        k_tile, = head_norm_rope(zkv[:, :LANES], gain_ref[1:2, :])
        ka_ref[0, rows, :] = k_tile[:, :HEAD_DIM].astype(BF16)
        ka_ref[1, rows, :] = k_tile[:, HEAD_DIM:].astype(BF16)
        _transposed_chunks_store(va_ref, zkv[:, LANES:], part * part_rows)
        sga_ref[rows, :] = _silu(proj(AB_GA, AB_QB)).astype(BF16)
        sgb_ref[rows, :] = _silu(proj(AB_GB, AB_IN)).astype(BF16)

        zqk = proj(AB_QB, AB_VB)
        for c in range(B_QKW // LANES):
            qb_ref[rows, c * LANES:(c + 1) * LANES] = _rotate_pairs(
                zqk[:, c * LANES:(c + 1) * LANES], cosb, sinb).astype(BF16)
            kt = zqk[:, B_QKW + c * LANES:B_QKW + (c + 1) * LANES] * (HEAD_DIM ** -0.5)
            kb_ref[rows, c * LANES:(c + 1) * LANES] = _rotate_pairs(kt, cosb, sinb).astype(BF16)
        vb_ref[rows, :] = proj(AB_VB, AB_GB).astype(BF16)


def _inproj_ab(x2d, seq, g, w, gain, ones, tabs):
    rows = x2d.shape[0]
    tm = ROW_TILE
    per_seq = seq // tm
    row = lambda i: (i, 0)
    const = lambda i: (0, 0)
    tab = lambda i: (i % per_seq, 0)
    kvrow = lambda i: (0, i, 0)
    tk = _att_tk(seq)
    vt_chunks = tm // tk
    out_shape = (
        jax.ShapeDtypeStruct((rows, A_W), BF16), jax.ShapeDtypeStruct((A_KV, rows, HEAD_DIM), BF16),
        jax.ShapeDtypeStruct((rows // tk, A_KVW, tk), BF16), jax.ShapeDtypeStruct((rows, A_W), BF16),
        jax.ShapeDtypeStruct((rows, B_QKW), BF16), jax.ShapeDtypeStruct((rows, B_QKW), BF16),
        jax.ShapeDtypeStruct((rows, B_VW), BF16), jax.ShapeDtypeStruct((rows, B_VW), BF16),
    )
    out_specs = (
        pl.BlockSpec((tm, A_W), row), pl.BlockSpec((A_KV, tm, HEAD_DIM), kvrow),
        pl.BlockSpec((vt_chunks, A_KVW, tk), lambda i: (i, 0, 0)), pl.BlockSpec((tm, A_W), row),
        pl.BlockSpec((tm, B_QKW), row), pl.BlockSpec((tm, B_QKW), row),
        pl.BlockSpec((tm, B_VW), row), pl.BlockSpec((tm, B_VW), row),
    )
    in_specs = [
        pl.BlockSpec((tm, D_MODEL), row), pl.BlockSpec((1, D_MODEL), const),
        pl.BlockSpec((D_MODEL, AB_IN), const), pl.BlockSpec((2, LANES), const),
        pl.BlockSpec((2 * LANES, 2 * LANES), const),
    ] + [pl.BlockSpec((tm, LANES), tab)] * 4
    return pl.pallas_call(
        _inproj_ab_kernel, grid=(rows // tm,), in_specs=in_specs, out_specs=out_specs,
        out_shape=out_shape, compiler_params=_params(("parallel",)), name="inproj_ab",
    )(x2d, g, w, gain, ones, *tabs)


def _dense_attn_kernel(q_ref, k_ref, vt_ref, sg_ref, o_ref, s_buf, p_buf, *, seq):
    tq, tk = _att_tq(seq), _att_tk(seq)
    rows = A_GROUP * tq
    n = seq // tk
    blocks = seq // tq
    assert n % 2 == 0 and n >= 2

    def query_rows(i):
        return pl.ds(pl.multiple_of(i * tq, tq), tq)

    def stacked_q(i):
        return _stack_heads_on_rows(q_ref[query_rows(i), :], A_GROUP)

    def scores(qs, c):
        k = k_ref[c * tk:(c + 1) * tk, :]
        s_t = lax.dot_general(k, qs, (((1,), (1,)), ((), ())), preferred_element_type=F32)
        s_buf[c % 2] = s_t
        return jnp.max(s_t, axis=0, keepdims=True)

    def softmax(m, c, chunk_max):
        m_new = jnp.maximum(m, chunk_max)
        p_buf[c % 2] = jnp.exp2(s_buf[c % 2] - m_new).astype(BF16)
        return m_new, jnp.exp2(m - m_new)

    fresh_m = jnp.full((1, rows), NEG, F32)
    first_q = stacked_q(0)
    first_max = [scores(first_q, 0), scores(first_q, 1)]

    def block(i, m, alpha, max0, max1):
        chunk_max = [max0, max1]
        qs = stacked_q(i)
        qs_next = stacked_q(jnp.minimum(i + 1, blocks - 1))
        acc = jnp.zeros((HEAD_DIM + ONES_ROWS, rows), F32)
        for t in range(n):
            if t + 2 < n:
                scored = scores(qs, t + 2)
            else:
                scored = scores(qs_next, t + 2 - n)
            if t + 1 < n:
                m, alpha_next = softmax(m, t + 1, chunk_max[(t + 1) % 2])
            else:
                m, alpha_next = softmax(fresh_m, 0, chunk_max[0])
            chunk_max[t % 2] = scored
            pv = jnp.dot(_with_ones_rows(vt_ref[t]), p_buf[t % 2], preferred_element_type=F32)
            acc = alpha * acc + pv
            alpha = alpha_next
        o_t = acc[:HEAD_DIM] / acc[HEAD_DIM:HEAD_DIM + 1]
        o = _unstack_transposed(o_t, A_GROUP, tq)
        o_ref[query_rows(i), :] = (o * sg_ref[query_rows(i), :].astype(F32)).astype(BF16)
        return m, alpha, chunk_max[0], chunk_max[1]

    per_body = max(1, ATT_BODY_CHUNKS // n)

    def body(step, carry):
        for u in range(per_body):
            carry = block(step * per_body + u, *carry)
        return carry

    lax.fori_loop(0, blocks // per_body, body, softmax(fresh_m, 0, first_max[0]) + tuple(first_max))


def _dense_attn(qa, ka, vat, sga, batch, seq):
    gw = A_GROUP * HEAD_DIM
    q3 = qa.reshape(batch, seq, A_W)
    sg3 = sga.reshape(batch, seq, A_W)
    k4 = ka.reshape(A_KV, batch, seq, HEAD_DIM)
    tk = _att_tk(seq)
    vt4 = vat.reshape(batch, seq // tk, A_KVW, tk)
    qspec = pl.BlockSpec((None, seq, gw), lambda b, g: (b, 0, g))
    kspec = pl.BlockSpec((None, None, seq, HEAD_DIM), lambda b, g: (g, b, 0, 0))
    vtspec = pl.BlockSpec((None, seq // tk, HEAD_DIM, tk), lambda b, g: (b, 0, g, 0))
    out = pl.pallas_call(
        functools.partial(_dense_attn_kernel, seq=seq), grid=(batch, A_KV),
        in_specs=[qspec, kspec, vtspec, qspec], out_specs=qspec,
        out_shape=jax.ShapeDtypeStruct((batch, seq, A_W), BF16),
        scratch_shapes=[pltpu.VMEM((2, tk, A_GROUP * _att_tq(seq)), F32),
                        pltpu.VMEM((2, tk, A_GROUP * _att_tq(seq)), BF16)],
        compiler_params=_params(("parallel", "parallel")), name="dense_attn",
    )(q3, k4, vt4, sg3)
    return out.reshape(batch * seq, A_W)


def _retention_kernel(lg_ref, q_ref, k_ref, v_ref, sg_ref, o_ref, st_ref, *, seq):
    c = CHUNK
    nc = seq // c
    pv = 2 * B_VAL
    pair = pl.program_id(1)
    lgf = [lg_ref[0, 2 * pair + hh] for hh in range(2)]
    lgb = [lg_ref[1, 2 * pair + hh] for hh in range(2)]
    lane = lax.broadcasted_iota(jnp.int32, (1, LANES), 1)
    first_qk_head = lane < HEAD_DIM
    lane_lgf = jnp.where(first_qk_head, lgf[0], lgf[1])
    lane_lgb = jnp.where(first_qk_head, lgb[0], lgb[1])
    t = lax.broadcasted_iota(jnp.int32, (c, 1), 0).astype(F32)
    kdec_f = jnp.exp((c - 1.0 - t) * lane_lgf)
    kdec_b = jnp.exp(t * lane_lgb)
    qdec_f = jnp.exp((t + 1.0) * lane_lgf)
    qdec_b = jnp.exp((c - t) * lane_lgb)
    row_first = lax.broadcasted_iota(jnp.int32, (LANES, 1), 0) < HEAD_DIM
    first_v_head = lax.broadcasted_iota(jnp.int32, (1, pv), 1) < B_VAL
    same_head = row_first == first_v_head
    cdec_f = jnp.exp(c * jnp.where(row_first, lgf[0], lgf[1]))
    cdec_b = jnp.exp(c * jnp.where(row_first, lgb[0], lgb[1]))
    diff = (lax.broadcasted_iota(jnp.int32, (c, c), 0) - lax.broadcasted_iota(jnp.int32, (c, c), 1)).astype(F32)
    intra_decay = jnp.concatenate(
        [jnp.where(diff >= 0, jnp.exp(lgf[hh] * jnp.maximum(diff, 0.0)), jnp.exp(lgb[hh] * jnp.maximum(-diff, 0.0)))
         for hh in range(2)], axis=1)

    def chunk_rows(n):
        return pl.ds(pl.multiple_of(n * c, c), c)

    def kv_body(n, carry):
        rows = chunk_rows(n)
        k = k_ref[rows, :].astype(F32)
        v = v_ref[rows, :]
        dn = (((0,), (0,)), ((), ()))
        kvf = lax.dot_general((k * kdec_f).astype(BF16), v, dn, preferred_element_type=F32)
        kvb = lax.dot_general((k * kdec_b).astype(BF16), v, dn, preferred_element_type=F32)
        st_ref[n, 0:LANES, :] = jnp.where(same_head, kvf, 0.0)
        st_ref[n, LANES:2 * LANES, :] = jnp.where(same_head, kvb, 0.0)
        return carry

    lax.fori_loop(0, nc, kv_body, 0, unroll=RET_UNROLL)

    def scan_fwd(n, state):
        kv = st_ref[n, 0:LANES, :]
        st_ref[n, 0:LANES, :] = state
        return state * cdec_f + kv

    lax.fori_loop(0, nc, scan_fwd, jnp.zeros((LANES, pv), F32))

    def scan_bwd(i, state):
        n = nc - 1 - i
        kv = st_ref[n, LANES:2 * LANES, :]
        st_ref[n, LANES:2 * LANES, :] = state
        return state * cdec_b + kv

    lax.fori_loop(0, nc, scan_bwd, jnp.zeros((LANES, pv), F32))

    def out_body(n, carry):
        rows = chunk_rows(n)
        q = q_ref[rows, :].astype(F32)
        k = k_ref[rows, :]
        v = v_ref[rows, :]
        zero = jnp.zeros_like(k)
        k_by_head = jnp.concatenate([jnp.where(first_qk_head, k, zero), jnp.where(first_qk_head, zero, k)], axis=0)
        s = lax.dot_general(q.astype(BF16), k_by_head, (((1,), (1,)), ((), ())),
                            preferred_element_type=F32)
        zero_v = jnp.zeros_like(v)
        v_by_head = jnp.concatenate([jnp.where(first_v_head, v, zero_v), jnp.where(first_v_head, zero_v, v)], axis=0)
        lhs = jnp.concatenate([s * intra_decay, q * qdec_f, q * qdec_b], axis=1).astype(BF16)
        rhs = jnp.concatenate([v_by_head, st_ref[n].astype(BF16)], axis=0)
        o_pair = jnp.dot(lhs, rhs, preferred_element_type=F32)
        for hh in range(2):
            o = o_pair[:, hh * B_VAL:(hh + 1) * B_VAL]
            o = o * lax.rsqrt(jnp.mean(o * o, axis=-1, keepdims=True) + EPS)
            o = o * sg_ref[rows, hh * B_VAL:(hh + 1) * B_VAL].astype(F32)
            o_ref[rows, hh * B_VAL:(hh + 1) * B_VAL] = o.astype(BF16)
        return carry

    lax.fori_loop(0, nc, out_body, 0, unroll=RET_UNROLL)


def _retention(log_gamma, qb, kb, vb, sgb, batch, seq):
    pairs = B_HEADS // 2
    qk_spec = pl.BlockSpec((None, seq, LANES), lambda b, j: (b, 0, j))
    v_spec = pl.BlockSpec((None, seq, 2 * B_VAL), lambda b, j: (b, 0, j))
    out = pl.pallas_call(
        functools.partial(_retention_kernel, seq=seq), grid=(batch, pairs),
        in_specs=[pl.BlockSpec(memory_space=pltpu.SMEM), qk_spec, qk_spec, v_spec, v_spec],
        out_specs=v_spec, out_shape=jax.ShapeDtypeStruct((batch, seq, B_VW), BF16),
        scratch_shapes=[pltpu.VMEM((seq // CHUNK, 2 * LANES, 2 * B_VAL), F32)],
        compiler_params=_params(("parallel", "parallel")), name="retention",
    )(log_gamma, qb.reshape(batch, seq, B_QKW), kb.reshape(batch, seq, B_QKW),
      vb.reshape(batch, seq, B_VW), sgb.reshape(batch, seq, B_VW))
    return out.reshape(batch * seq, B_VW)


def _mid_proj_kernel(x_ref, ma_ref, mb_ref, wo_ref, g_ref, wc_ref, x1_ref, q_ref, k_ref, v_ref, sg_ref):
    part_rows = x_ref.shape[0] // PROJ_SPLIT
    for part in range(PROJ_SPLIT):
        rows = slice(part * part_rows, (part + 1) * part_rows)
        y = jnp.dot(ma_ref[rows, :], wo_ref[0:A_W, :], preferred_element_type=F32)
        y += jnp.dot(mb_ref[rows, :], wo_ref[A_W:A_W + B_VW, :], preferred_element_type=F32)
        x1 = x_ref[rows, :] + y
        x1_ref[rows, :] = x1
        xn = _rmsnorm_rows(x1, g_ref[...]).astype(BF16)

        def proj(lo, hi):
            return jnp.dot(xn, wc_ref[:, lo:hi], preferred_element_type=F32)

        sg_ref[rows, :] = _silu(proj(C_G, C_IN)).astype(BF16)
        zkv = proj(C_K, C_G)
        k_ref[0, rows, :] = zkv[:, :HEAD_DIM].astype(BF16)
        k_ref[1, rows, :] = zkv[:, HEAD_DIM:LANES].astype(BF16)
        _transposed_chunks_store(v_ref, zkv[:, LANES:], part * part_rows)
        q_ref[rows, :] = (proj(C_Q, C_K) * (HEAD_DIM ** -0.5 * LOG2E)).astype(BF16)


def _mid_proj(x2d, ma, mb, wo, g, wc):
    rows = x2d.shape[0]
    tm = ROW_TILE
    row = lambda i: (i, 0)
    const = lambda i: (0, 0)
    kvrow = lambda i: (0, i, 0)
    out_shape = (
        jax.ShapeDtypeStruct((rows, D_MODEL), F32), jax.ShapeDtypeStruct((rows, C_W), BF16),
        jax.ShapeDtypeStruct((C_KV, rows, HEAD_DIM), BF16), jax.ShapeDtypeStruct((rows // BLOCK, C_KVW, BLOCK), BF16),
        jax.ShapeDtypeStruct((rows, C_W), BF16),
    )
    out_specs = (
        pl.BlockSpec((tm, D_MODEL), row), pl.BlockSpec((tm, C_W), row),
        pl.BlockSpec((C_KV, tm, HEAD_DIM), kvrow), pl.BlockSpec((tm // BLOCK, C_KVW, BLOCK), lambda i: (i, 0, 0)),
        pl.BlockSpec((tm, C_W), row),
    )
    in_specs = [
        pl.BlockSpec((tm, D_MODEL), row), pl.BlockSpec((tm, A_W), row), pl.BlockSpec((tm, B_VW), row),
        pl.BlockSpec((A_W + B_VW, D_MODEL), const), pl.BlockSpec((1, D_MODEL), const),
        pl.BlockSpec((D_MODEL, C_IN), const),
    ]
    return pl.pallas_call(
        _mid_proj_kernel, grid=(rows // tm,), in_specs=in_specs, out_specs=out_specs,
        out_shape=out_shape, compiler_params=_params(("parallel",)), name="mid_proj",
    )(x2d, ma, mb, wo, g, wc)


def _win_attn_kernel(q_ref, kp_ref, k_ref, kn_ref, vp_ref, v_ref, vn_ref, bias_ref, sink_ref, sg_ref, o_ref,
                     kw_all, vt_all, s_buf, p_buf, *, blocks_per_seq):
    j = pl.program_id(1)
    nblk = q_ref.shape[0] // BLOCK
    gw = C_GROUP * HEAD_DIM
    for g in range(C_KV):
        kw_all[g, 0:BLOCK] = kp_ref[g]
        kw_all[g, BLOCK:(nblk + 1) * BLOCK] = k_ref[g]
        kw_all[g, (nblk + 1) * BLOCK:(nblk + 2) * BLOCK] = kn_ref[g]
    vt_all[0] = vp_ref[0]
    vt_all[1:nblk + 1] = v_ref[...]
    vt_all[nblk + 1] = vn_ref[0]

    def block_rows(i):
        return pl.ds(pl.multiple_of(i * BLOCK, BLOCK), BLOCK)

    def scores(i, g):
        qs = _stack_heads_on_rows(q_ref[block_rows(i), g * gw:(g + 1) * gw], C_GROUP)
        kw = kw_all[g, pl.ds(pl.multiple_of(i * BLOCK, BLOCK), 3 * BLOCK), :]
        s_buf[g] = lax.dot_general(kw, qs, (((1,), (1,)), ((), ())), preferred_element_type=F32)

    def softmax(i, g):
        block_in_seq = j * nblk + i
        prev_bias = jnp.where(block_in_seq == 0, 3, 0)
        next_bias = jnp.where(block_in_seq == blocks_per_seq - 1, 3, 2)
        s_t = jnp.concatenate([s_buf[g, 0:BLOCK] + bias_ref[prev_bias, g],
                               s_buf[g, BLOCK:2 * BLOCK] + bias_ref[1, g],
                               s_buf[g, 2 * BLOCK:3 * BLOCK] + bias_ref[next_bias, g]], axis=0)
        sink = sink_ref[g]
        mx = jnp.maximum(jnp.max(s_t, axis=0, keepdims=True), sink)
        p_buf[g] = jnp.exp2(s_t - mx).astype(BF16)
        return jnp.exp2(sink - mx)

    def finish(i, g, sink_term):
        heads = slice(g * HEAD_DIM, (g + 1) * HEAD_DIM)
        vw_t = jnp.concatenate([vt_all[i, heads, :], vt_all[i + 1, heads, :], vt_all[i + 2, heads, :]], axis=1)
        acc = jnp.dot(_with_ones_rows(vw_t), p_buf[g], preferred_element_type=F32)
        denom = acc[HEAD_DIM:HEAD_DIM + 1] + sink_term
        o = _unstack_transposed(acc[:HEAD_DIM] / denom, C_GROUP, BLOCK)
        gate = sg_ref[block_rows(i), g * gw:(g + 1) * gw].astype(F32)
        o_ref[block_rows(i), g * gw:(g + 1) * gw] = (o * gate).astype(BF16)

    scores(0, 0)
    scores(0, 1)

    def body(step, sink_term0):
        for u in range(WIN_UNROLL):
            i = step * WIN_UNROLL + u
            nxt = jnp.minimum(i + 1, nblk - 1)
            scores(nxt, 0)
            sink_term1 = softmax(i, 1)
            finish(i, 0, sink_term0)
            scores(nxt, 1)
            sink_term0 = softmax(nxt, 0)
            finish(i, 1, sink_term1)
        return sink_term0

    lax.fori_loop(0, nblk // WIN_UNROLL, body, softmax(0, 0))


def _win_attn(q, k, vt, bias, sink, sg, batch, seq):
    nb = seq // BLOCK
    sup = WIN_SUPER
    nblk = sup // BLOCK
    q3 = q.reshape(batch, seq, C_W)
    sg3 = sg.reshape(batch, seq, C_W)
    k4 = k.reshape(C_KV, batch, seq, HEAD_DIM)
    rowspec = pl.BlockSpec((None, sup, C_W), lambda b, j: (b, j, 0))
    before = lambda j: jnp.maximum(j * nblk - 1, 0)
    after = lambda j: jnp.minimum((j + 1) * nblk, nb - 1)
    halo_k = (C_KV, None, BLOCK, HEAD_DIM)
    halo_v = (1, C_KVW, BLOCK)
    out = pl.pallas_call(
        functools.partial(_win_attn_kernel, blocks_per_seq=nb), grid=(batch, seq // sup),
        in_specs=[rowspec,
                  pl.BlockSpec(halo_k, lambda b, j: (0, b, before(j), 0)),
                  pl.BlockSpec((C_KV, None, sup, HEAD_DIM), lambda b, j: (0, b, j, 0)),
                  pl.BlockSpec(halo_k, lambda b, j: (0, b, after(j), 0)),
                  pl.BlockSpec(halo_v, lambda b, j: (b * nb + before(j), 0, 0)),
                  pl.BlockSpec((nblk, C_KVW, BLOCK), lambda b, j: (b * (nb // nblk) + j, 0, 0)),
                  pl.BlockSpec(halo_v, lambda b, j: (b * nb + after(j), 0, 0)),
                  pl.BlockSpec((4, C_KV, BLOCK, C_GROUP * BLOCK), lambda b, j: (0, 0, 0, 0)),
                  pl.BlockSpec((C_KV, 1, C_GROUP * BLOCK), lambda b, j: (0, 0, 0)), rowspec],
        out_specs=rowspec, out_shape=jax.ShapeDtypeStruct((batch, seq, C_W), BF16),
        scratch_shapes=[pltpu.VMEM((C_KV, sup + 2 * BLOCK, HEAD_DIM), BF16),
                        pltpu.VMEM((nblk + 2, C_KVW, BLOCK), BF16),
                        pltpu.VMEM((C_KV, 3 * BLOCK, C_GROUP * BLOCK), F32),
                        pltpu.VMEM((C_KV, 3 * BLOCK, C_GROUP * BLOCK), BF16)],
        compiler_params=_params(("parallel", "parallel")), name="win_attn",
    )(q3, k4, k4, k4, vt, vt, vt, bias, sink, sg3)
    return out.reshape(batch * seq, C_W)


def _out_proj_kernel(x_ref, m_ref, wo_ref, g_ref, y_ref):
    part_rows = x_ref.shape[0] // PROJ_SPLIT
    for part in range(PROJ_SPLIT):
        rows = slice(part * part_rows, (part + 1) * part_rows)
        x2 = x_ref[rows, :] + jnp.dot(m_ref[rows, :], wo_ref[...], preferred_element_type=F32)
        y_ref[rows, :] = _rmsnorm_rows(x2, g_ref[...])


def _out_proj(x2d, m, wo, g):
    rows = x2d.shape[0]
    tm = ROW_TILE
    row = lambda i: (i, 0)
    const = lambda i: (0, 0)
    return pl.pallas_call(
        _out_proj_kernel, grid=(rows // tm,),
        in_specs=[pl.BlockSpec((tm, D_MODEL), row), pl.BlockSpec((tm, C_W), row),
                  pl.BlockSpec((C_W, D_MODEL), const), pl.BlockSpec((1, D_MODEL), const)],
        out_specs=pl.BlockSpec((tm, D_MODEL), row), out_shape=jax.ShapeDtypeStruct((rows, D_MODEL), F32),
        compiler_params=_params(("parallel",)), name="out_proj",
    )(x2d, m, wo, g)


def _trunk(x, prm):
    batch, seq, _ = x.shape
    x2d = x.reshape(batch * seq, D_MODEL)
    qa, ka, va, sga, qb, kb, vb, sgb = _inproj_ab(
        x2d, seq, prm["g0"], prm["w_in_ab"], prm["gain"], prm["ones"], prm["tabs"])
    ma = _dense_attn(qa, ka, va, sga, batch, seq)
    mb = _retention(prm["log_gamma"], qb, kb, vb, sgb, batch, seq)
    x1, qc, kc, vc, sgc = _mid_proj(x2d, ma, mb, prm["w_out_ab"], prm["g1"], prm["w_in_c"])
    mc = _win_attn(qc, kc, vc, prm["bias"], prm["sink"], sgc, batch, seq)
    y = _out_proj(x1, mc, prm["w_out_c"], prm["gf"])
    return y.reshape(batch, seq, D_MODEL)


def kernel(x_prompt, x_sample, norm_g, w_in_ab, qk_norm_a, ret_decay, w_out_ab, w_in_c, sink_c, w_out_c,
           rel_bias, final_norm):
    assert w_in_ab.shape[0] == 1 and w_in_c.shape[0] == 1, "two-layer trunk: one layer of each kind"
    max_seq = max(x_prompt.shape[1], x_sample.shape[1])

    w_ab = w_in_ab[0].astype(BF16)
    gain = jnp.stack([jnp.tile(qk_norm_a[0, 0], 2) * (HEAD_DIM ** -0.5 * LOG2E),
                      jnp.tile(qk_norm_a[0, 1], 2)]).astype(F32)
    ones = np.kron(np.eye(2 * LANES // HEAD_DIM), np.ones((HEAD_DIM, HEAD_DIM)))

    rel = np.arange(3 * BLOCK)[:, None] - BLOCK - np.arange(BLOCK)[None, :]
    bucket_onehot = jax.nn.one_hot(jnp.asarray(_t5_bucket(rel).reshape(-1)), REL_BUCKETS, dtype=F32, axis=0)
    bias = jnp.einsum("bh,bn->hn", rel_bias.astype(F32) * LOG2E, bucket_onehot,
                      precision=lax.Precision.HIGHEST)
    bias = jnp.where(jnp.asarray((np.abs(rel) <= WINDOW).reshape(1, -1)), bias, NEG)
    bias = bias.reshape(C_KV, C_GROUP, 3, BLOCK, BLOCK).transpose(2, 0, 3, 1, 4)
    bias = bias.reshape(3, C_KV, BLOCK, C_GROUP * BLOCK)
    bias = jnp.concatenate([bias, jnp.full((1,) + bias.shape[1:], NEG, F32)], axis=0)
    sink = jnp.repeat(sink_c[0].astype(F32) * LOG2E, BLOCK).reshape(C_KV, 1, C_GROUP * BLOCK)

    prm = {
        "g0": norm_g[0].reshape(1, D_MODEL), "g1": norm_g[1].reshape(1, D_MODEL),
        "gf": final_norm.reshape(1, D_MODEL),
        "w_in_ab": w_ab, "w_out_ab": w_out_ab[0].astype(BF16),
        "w_in_c": w_in_c[0].astype(BF16), "w_out_c": w_out_c[0].astype(BF16),
        "gain": gain, "ones": jnp.asarray(ones, BF16),
        "tabs": _rope_tables(_axial_angles(max_seq)) + _rope_tables(_linear_angles(max_seq)),
        "log_gamma": -jnp.exp(ret_decay[0].astype(F32)),
        "bias": bias, "sink": sink,
    }
    return _trunk(x_prompt, prm), _trunk(x_sample, prm)
```

```python
import functools

import jax
import jax.numpy as jnp
import numpy as np
from jax import lax
from jax.experimental import pallas as pl
from jax.experimental.pallas import tpu as pltpu

F32 = jnp.float32
BF16 = jnp.bfloat16

D_MODEL = 1024
GRID_W = 64
EPS = 1e-6
ROPE_THETA = 10000.0
HEAD_DIM = 64
HALF = HEAD_DIM // 2
LANES = 128

A_HEADS, A_KV = 8, 2
A_GROUP = A_HEADS // A_KV
A_W, A_KVW = A_HEADS * HEAD_DIM, A_KV * HEAD_DIM
B_HEADS, B_VAL = 4, 128
B_QKW, B_VW = B_HEADS * HEAD_DIM, B_HEADS * B_VAL
C_HEADS, C_KV = 16, 2
C_GROUP = C_HEADS // C_KV
C_W, C_KVW = C_HEADS * HEAD_DIM, C_KV * HEAD_DIM
BLOCK = 128
WINDOW = 128
REL_BUCKETS, REL_MAX_DIST = 32, 128
CHUNK = 128
RET_UNROLL = 32
NEG = -1e30
LOG2E = float(np.log2(np.e))
ONES_ROWS = 16

AB_QA, AB_KA, AB_VA, AB_GA = 0, A_W, A_W + A_KVW, A_W + 2 * A_KVW
AB_QB = AB_GA + A_W
AB_KB, AB_VB, AB_GB = AB_QB + B_QKW, AB_QB + 2 * B_QKW, AB_QB + 2 * B_QKW + B_VW
AB_IN = AB_GB + B_VW
C_Q, C_K, C_V, C_G = 0, C_W, C_W + C_KVW, C_W + 2 * C_KVW
C_IN = C_G + C_W

PROJ_SPLIT = 2
ROW_TILE = 1024
ATT_TQ = 256
ATT_TK_MAX = 1024
ATT_SEQ_CHUNKS = 4
ATT_BODY_KEYS = 16384
WIN_SUPER = 2048
WIN_UNROLL = 16
VMEM_LIMIT = 48 * 1024 * 1024


def _att_tk(seq):
    return min(ATT_TK_MAX, seq // ATT_SEQ_CHUNKS)


def _att_tq(seq):
    del seq
    return ATT_TQ


def _params(sem, flags=None):
    return pltpu.CompilerParams(dimension_semantics=sem, vmem_limit_bytes=VMEM_LIMIT, flags=flags)


def _rope_tables(ang):
    c, s = np.repeat(np.cos(ang), 2, axis=-1), np.repeat(np.sin(ang), 2, axis=-1)
    s[:, 0::2] *= -1.0
    return jnp.asarray(np.tile(c, (1, 2)), F32), jnp.asarray(np.tile(s, (1, 2)), F32)


def _axial_angles(n):
    t = np.arange(n)
    quarter = HEAD_DIM // 4
    freqs = ROPE_THETA ** (-np.arange(quarter, dtype=np.float64) / quarter)
    row = (t // GRID_W).astype(np.float64)
    col = (t % GRID_W).astype(np.float64)
    return np.concatenate([row[:, None] * freqs, col[:, None] * freqs], axis=-1)


def _linear_angles(n):
    freqs = ROPE_THETA ** (-np.arange(HALF, dtype=np.float64) / HALF)
    return np.arange(n, dtype=np.float64)[:, None] * freqs


def _t5_bucket(rel):
    half = REL_BUCKETS // 2
    max_exact = half // 2
    ret = (rel > 0).astype(np.int32) * half
    dist = np.abs(rel)
    large = max_exact + (np.log(np.maximum(dist, 1) / max_exact) / np.log(REL_MAX_DIST / max_exact)
                         * (half - max_exact)).astype(np.int32)
    large = np.minimum(large, half - 1)
    return ret + np.where(dist < max_exact, dist, large)


def _rmsnorm_rows(x, gain):
    return x * lax.rsqrt(jnp.mean(x * x, axis=-1, keepdims=True) + EPS) * gain


def _silu(z):
    return z * jax.nn.sigmoid(z)


def _rotate_pairs(x, cos_t, sin_t):
    lane = lax.broadcasted_iota(jnp.int32, (1, LANES), 1)
    partner = jnp.where(lane % 2 == 0, pltpu.roll(x, LANES - 1, 1), pltpu.roll(x, 1, 1))
    return x * cos_t + partner * sin_t


def _transposed_chunks_store(ref, tile, row0=0):
    chunk = ref.shape[2]
    rows = tile.shape[0]
    tile_t = tile.T
    step = min(chunk, rows)
    for s in range(0, rows, step):
        c, off = divmod(row0 + s, chunk)
        ref[c, :, off:off + step] = tile_t[:, s:s + step].astype(ref.dtype)


def _stack_heads_on_rows(q, heads):
    return jnp.concatenate([q[:, h * HEAD_DIM:(h + 1) * HEAD_DIM] for h in range(heads)], axis=0)


def _unstack_transposed(o_t, heads, tokens):
    by_head = jnp.concatenate([o_t[:, h * tokens:(h + 1) * tokens] for h in range(heads)], axis=0)
    return by_head.T


def _with_ones_rows(v_t):
    return jnp.concatenate([v_t, jnp.ones((ONES_ROWS, v_t.shape[1]), v_t.dtype)], axis=0)


def _inproj_ab_kernel(x_ref, g_ref, w_ref, gain_ref, ones_ref, cosa_ref, sina_ref, cosb_ref, sinb_ref,
                      qa_ref, ka_ref, va_ref, sga_ref, qb_ref, kb_ref, vb_ref, sgb_ref):
    part_rows = x_ref.shape[0] // PROJ_SPLIT
    for part in range(PROJ_SPLIT):
        rows = slice(part * part_rows, (part + 1) * part_rows)
        xn = _rmsnorm_rows(x_ref[rows, :], g_ref[...]).astype(BF16)

        def proj(lo, hi):
            return jnp.dot(xn, w_ref[:, lo:hi], preferred_element_type=F32)

        cosa, sina = cosa_ref[rows, :], sina_ref[rows, :]
        cosb, sinb = cosb_ref[rows, :], sinb_ref[rows, :]

        def head_norm_rope(z, gain):
            width = z.shape[1]
            ss = jnp.dot((z * z).astype(BF16), ones_ref[0:width, 0:width], preferred_element_type=F32)
            z = z * lax.rsqrt(ss * (1.0 / HEAD_DIM) + EPS)
            return [_rotate_pairs(z[:, c * LANES:(c + 1) * LANES] * gain, cosa, sina)
                    for c in range(width // LANES)]

        zq = proj(AB_QA, AB_KA)
        for pair in range(A_W // (2 * LANES)):
            tiles = head_norm_rope(zq[:, pair * 2 * LANES:(pair + 1) * 2 * LANES], gain_ref[0:1, :])
            for c, tile in enumerate(tiles):
                lo = (2 * pair + c) * LANES
                qa_ref[rows, lo:lo + LANES] = tile.astype(BF16)
        zkv = proj(AB_KA, AB_GA)
        k_tile, = head_norm_rope(zkv[:, :LANES], gain_ref[1:2, :])
        ka_ref[0, rows, :] = k_tile[:, :HEAD_DIM].astype(BF16)
        ka_ref[1, rows, :] = k_tile[:, HEAD_DIM:].astype(BF16)
        _transposed_chunks_store(va_ref, zkv[:, LANES:], part * part_rows)
        sga_ref[rows, :] = _silu(proj(AB_GA, AB_QB)).astype(BF16)
        sgb_ref[rows, :] = _silu(proj(AB_GB, AB_IN)).astype(BF16)

        zqk = proj(AB_QB, AB_VB)
        for c in range(B_QKW // LANES):
            qb_ref[rows, c * LANES:(c + 1) * LANES] = _rotate_pairs(
                zqk[:, c * LANES:(c + 1) * LANES], cosb, sinb).astype(BF16)
            kt = zqk[:, B_QKW + c * LANES:B_QKW + (c + 1) * LANES] * (HEAD_DIM ** -0.5)
            kb_ref[rows, c * LANES:(c + 1) * LANES] = _rotate_pairs(kt, cosb, sinb).astype(BF16)
        vb_ref[rows, :] = proj(AB_VB, AB_GB).astype(BF16)


def _inproj_ab(x2d, seq, g, w, gain, ones, tabs):
    rows = x2d.shape[0]
    tm = ROW_TILE
    per_seq = seq // tm
    row = lambda i: (i, 0)
    const = lambda i: (0, 0)
    tab = lambda i: (i % per_seq, 0)
    kvrow = lambda i: (0, i, 0)
    tk = _att_tk(seq)
    vt_chunks = tm // tk
    out_shape = (
        jax.ShapeDtypeStruct((rows, A_W), BF16), jax.ShapeDtypeStruct((A_KV, rows, HEAD_DIM), BF16),
        jax.ShapeDtypeStruct((rows // tk, A_KVW, tk), BF16), jax.ShapeDtypeStruct((rows, A_W), BF16),
        jax.ShapeDtypeStruct((rows, B_QKW), BF16), jax.ShapeDtypeStruct((rows, B_QKW), BF16),
        jax.ShapeDtypeStruct((rows, B_VW), BF16), jax.ShapeDtypeStruct((rows, B_VW), BF16),
    )
    out_specs = (
        pl.BlockSpec((tm, A_W), row), pl.BlockSpec((A_KV, tm, HEAD_DIM), kvrow),
        pl.BlockSpec((vt_chunks, A_KVW, tk), lambda i: (i, 0, 0)), pl.BlockSpec((tm, A_W), row),
        pl.BlockSpec((tm, B_QKW), row), pl.BlockSpec((tm, B_QKW), row),
        pl.BlockSpec((tm, B_VW), row), pl.BlockSpec((tm, B_VW), row),
    )
    in_specs = [
        pl.BlockSpec((tm, D_MODEL), row), pl.BlockSpec((1, D_MODEL), const),
        pl.BlockSpec((D_MODEL, AB_IN), const), pl.BlockSpec((2, LANES), const),
        pl.BlockSpec((2 * LANES, 2 * LANES), const),
    ] + [pl.BlockSpec((tm, LANES), tab)] * 4
    return pl.pallas_call(
        _inproj_ab_kernel, grid=(rows // tm,), in_specs=in_specs, out_specs=out_specs,
        out_shape=out_shape, compiler_params=_params(("parallel",)), name="inproj_ab",
    )(x2d, g, w, gain, ones, *tabs)


def _dense_attn_kernel(q_ref, k_ref, vt_ref, sg_ref, o_ref, s_buf, p_buf, *, seq):
    tq, tk = _att_tq(seq), _att_tk(seq)
    rows = A_GROUP * tq
    n = seq // tk
    blocks = seq // tq
    assert n % 2 == 0 and n >= 2

    def query_rows(i):
        return pl.ds(pl.multiple_of(i * tq, tq), tq)

    def stacked_q(i):
        return _stack_heads_on_rows(q_ref[query_rows(i), :], A_GROUP)

    def scores(qs, c):
        k = k_ref[c * tk:(c + 1) * tk, :]
        s_t = lax.dot_general(k, qs, (((1,), (1,)), ((), ())), preferred_element_type=F32)
        s_buf[c % 2] = s_t
        return jnp.max(s_t, axis=0, keepdims=True)

    def softmax(m, c, chunk_max):
        m_new = jnp.maximum(m, chunk_max)
        p_buf[c % 2] = jnp.exp2(s_buf[c % 2] - m_new).astype(BF16)
        return m_new, jnp.exp2(m - m_new)

    fresh_m = jnp.full((1, rows), NEG, F32)
    first_q = stacked_q(0)
    first_max = [scores(first_q, 0), scores(first_q, 1)]

    def block(i, m, alpha, max0, max1):
        chunk_max = [max0, max1]
        qs = stacked_q(i)
        qs_next = stacked_q(jnp.minimum(i + 1, blocks - 1))
        acc = jnp.zeros((HEAD_DIM + ONES_ROWS, rows), F32)
        for t in range(n):
            if t + 2 < n:
                scored = scores(qs, t + 2)
            else:
                scored = scores(qs_next, t + 2 - n)
            if t + 1 < n:
                m, alpha_next = softmax(m, t + 1, chunk_max[(t + 1) % 2])
            else:
                m, alpha_next = softmax(fresh_m, 0, chunk_max[0])
            chunk_max[t % 2] = scored
            pv = jnp.dot(_with_ones_rows(vt_ref[t]), p_buf[t % 2], preferred_element_type=F32)
            acc = alpha * acc + pv
            alpha = alpha_next
        o_t = acc[:HEAD_DIM] / acc[HEAD_DIM:HEAD_DIM + 1]
        o = _unstack_transposed(o_t, A_GROUP, tq)
        o_ref[query_rows(i), :] = (o * sg_ref[query_rows(i), :].astype(F32)).astype(BF16)
        return m, alpha, chunk_max[0], chunk_max[1]

    per_body = min(blocks, max(1, ATT_BODY_KEYS // seq))

    def body(step, carry):
        for u in range(per_body):
            carry = block(step * per_body + u, *carry)
        return carry

    lax.fori_loop(0, blocks // per_body, body, softmax(fresh_m, 0, first_max[0]) + tuple(first_max))


def _dense_attn(qa, ka, vat, sga, batch, seq):
    gw = A_GROUP * HEAD_DIM
    q3 = qa.reshape(batch, seq, A_W)
    sg3 = sga.reshape(batch, seq, A_W)
    k4 = ka.reshape(A_KV, batch, seq, HEAD_DIM)
    tk = _att_tk(seq)
    vt4 = vat.reshape(batch, seq // tk, A_KVW, tk)
    qspec = pl.BlockSpec((None, seq, gw), lambda b, g: (b, 0, g))
    kspec = pl.BlockSpec((None, None, seq, HEAD_DIM), lambda b, g: (g, b, 0, 0))
    vtspec = pl.BlockSpec((None, seq // tk, HEAD_DIM, tk), lambda b, g: (b, 0, g, 0))
    out = pl.pallas_call(
        functools.partial(_dense_attn_kernel, seq=seq), grid=(batch, A_KV),
        in_specs=[qspec, kspec, vtspec, qspec], out_specs=qspec,
        out_shape=jax.ShapeDtypeStruct((batch, seq, A_W), BF16),
        scratch_shapes=[pltpu.VMEM((2, tk, A_GROUP * _att_tq(seq)), F32),
                        pltpu.VMEM((2, tk, A_GROUP * _att_tq(seq)), BF16)],
        compiler_params=_params(("parallel", "parallel")), name="dense_attn",
    )(q3, k4, vt4, sg3)
    return out.reshape(batch * seq, A_W)


def _retention_kernel(lg_ref, q_ref, k_ref, v_ref, sg_ref, o_ref, st_ref, *, seq):
    c = CHUNK
    nc = seq // c
    pv = 2 * B_VAL
    pair = pl.program_id(1)
    lgf = [lg_ref[0, 2 * pair + hh] for hh in range(2)]
    lgb = [lg_ref[1, 2 * pair + hh] for hh in range(2)]
    lane = lax.broadcasted_iota(jnp.int32, (1, LANES), 1)
    first_qk_head = lane < HEAD_DIM
    lane_lgf = jnp.where(first_qk_head, lgf[0], lgf[1])
    lane_lgb = jnp.where(first_qk_head, lgb[0], lgb[1])
    t = lax.broadcasted_iota(jnp.int32, (c, 1), 0).astype(F32)
    kdec_f = jnp.exp((c - 1.0 - t) * lane_lgf)
    kdec_b = jnp.exp(t * lane_lgb)
    qdec_f = jnp.exp((t + 1.0) * lane_lgf)
    qdec_b = jnp.exp((c - t) * lane_lgb)
    row_first = lax.broadcasted_iota(jnp.int32, (LANES, 1), 0) < HEAD_DIM
    first_v_head = lax.broadcasted_iota(jnp.int32, (1, pv), 1) < B_VAL
    same_head = row_first == first_v_head
    cdec_f = jnp.exp(c * jnp.where(row_first, lgf[0], lgf[1]))
    cdec_b = jnp.exp(c * jnp.where(row_first, lgb[0], lgb[1]))
    diff = (lax.broadcasted_iota(jnp.int32, (c, c), 0) - lax.broadcasted_iota(jnp.int32, (c, c), 1)).astype(F32)
    intra_decay = jnp.concatenate(
        [jnp.where(diff >= 0, jnp.exp(lgf[hh] * jnp.maximum(diff, 0.0)), jnp.exp(lgb[hh] * jnp.maximum(-diff, 0.0)))
         for hh in range(2)], axis=1)

    def chunk_rows(n):
        return pl.ds(pl.multiple_of(n * c, c), c)

    def kv_body(n, carry):
        rows = chunk_rows(n)
        k = k_ref[rows, :].astype(F32)
        v = v_ref[rows, :]
        dn = (((0,), (0,)), ((), ()))
        kvf = lax.dot_general((k * kdec_f).astype(BF16), v, dn, preferred_element_type=F32)
        kvb = lax.dot_general((k * kdec_b).astype(BF16), v, dn, preferred_element_type=F32)
        st_ref[n, 0:LANES, :] = jnp.where(same_head, kvf, 0.0)
        st_ref[n, LANES:2 * LANES, :] = jnp.where(same_head, kvb, 0.0)
        return carry

    lax.fori_loop(0, nc, kv_body, 0, unroll=RET_UNROLL)

    def scan_fwd(n, state):
        kv = st_ref[n, 0:LANES, :]
        st_ref[n, 0:LANES, :] = state
        return state * cdec_f + kv

    lax.fori_loop(0, nc, scan_fwd, jnp.zeros((LANES, pv), F32))

    def scan_bwd(i, state):
        n = nc - 1 - i
        kv = st_ref[n, LANES:2 * LANES, :]
        st_ref[n, LANES:2 * LANES, :] = state
        return state * cdec_b + kv

    lax.fori_loop(0, nc, scan_bwd, jnp.zeros((LANES, pv), F32))

    def out_body(n, carry):
        rows = chunk_rows(n)
        q = q_ref[rows, :].astype(F32)
        k = k_ref[rows, :]
        v = v_ref[rows, :]
        zero = jnp.zeros_like(k)
        k_by_head = jnp.concatenate([jnp.where(first_qk_head, k, zero), jnp.where(first_qk_head, zero, k)], axis=0)
        s = lax.dot_general(q.astype(BF16), k_by_head, (((1,), (1,)), ((), ())),
                            preferred_element_type=F32)
        zero_v = jnp.zeros_like(v)
        v_by_head = jnp.concatenate([jnp.where(first_v_head, v, zero_v), jnp.where(first_v_head, zero_v, v)], axis=0)
        lhs = jnp.concatenate([s * intra_decay, q * qdec_f, q * qdec_b], axis=1).astype(BF16)
        rhs = jnp.concatenate([v_by_head, st_ref[n].astype(BF16)], axis=0)
        o_pair = jnp.dot(lhs, rhs, preferred_element_type=F32)
        for hh in range(2):
            o = o_pair[:, hh * B_VAL:(hh + 1) * B_VAL]
            o = o * lax.rsqrt(jnp.mean(o * o, axis=-1, keepdims=True) + EPS)
            o = o * sg_ref[rows, hh * B_VAL:(hh + 1) * B_VAL].astype(F32)
            o_ref[rows, hh * B_VAL:(hh + 1) * B_VAL] = o.astype(BF16)
        return carry

    lax.fori_loop(0, nc, out_body, 0, unroll=RET_UNROLL)


def _retention(log_gamma, qb, kb, vb, sgb, batch, seq):
    pairs = B_HEADS // 2
    qk_spec = pl.BlockSpec((None, seq, LANES), lambda b, j: (b, 0, j))
    v_spec = pl.BlockSpec((None, seq, 2 * B_VAL), lambda b, j: (b, 0, j))
    out = pl.pallas_call(
        functools.partial(_retention_kernel, seq=seq), grid=(batch, pairs),
        in_specs=[pl.BlockSpec(memory_space=pltpu.SMEM), qk_spec, qk_spec, v_spec, v_spec],
        out_specs=v_spec, out_shape=jax.ShapeDtypeStruct((batch, seq, B_VW), BF16),
        scratch_shapes=[pltpu.VMEM((seq // CHUNK, 2 * LANES, 2 * B_VAL), F32)],
        compiler_params=_params(("parallel", "parallel")), name="retention",
    )(log_gamma, qb.reshape(batch, seq, B_QKW), kb.reshape(batch, seq, B_QKW),
      vb.reshape(batch, seq, B_VW), sgb.reshape(batch, seq, B_VW))
    return out.reshape(batch * seq, B_VW)


def _mid_proj_kernel(x_ref, ma_ref, mb_ref, wo_ref, g_ref, wc_ref, x1_ref, q_ref, k_ref, v_ref, sg_ref):
    part_rows = x_ref.shape[0] // PROJ_SPLIT
    for part in range(PROJ_SPLIT):
        rows = slice(part * part_rows, (part + 1) * part_rows)
        y = jnp.dot(ma_ref[rows, :], wo_ref[0:A_W, :], preferred_element_type=F32)
        y += jnp.dot(mb_ref[rows, :], wo_ref[A_W:A_W + B_VW, :], preferred_element_type=F32)
        x1 = x_ref[rows, :] + y
        x1_ref[rows, :] = x1
        xn = _rmsnorm_rows(x1, g_ref[...]).astype(BF16)

        def proj(lo, hi):
            return jnp.dot(xn, wc_ref[:, lo:hi], preferred_element_type=F32)

        sg_ref[rows, :] = _silu(proj(C_G, C_IN)).astype(BF16)
        zkv = proj(C_K, C_G)
        k_ref[0, rows, :] = zkv[:, :HEAD_DIM].astype(BF16)
        k_ref[1, rows, :] = zkv[:, HEAD_DIM:LANES].astype(BF16)
        _transposed_chunks_store(v_ref, zkv[:, LANES:], part * part_rows)
        q_ref[rows, :] = (proj(C_Q, C_K) * (HEAD_DIM ** -0.5 * LOG2E)).astype(BF16)


def _mid_proj(x2d, ma, mb, wo, g, wc):
    rows = x2d.shape[0]
    tm = ROW_TILE
    row = lambda i: (i, 0)
    const = lambda i: (0, 0)
    kvrow = lambda i: (0, i, 0)
    out_shape = (
        jax.ShapeDtypeStruct((rows, D_MODEL), F32), jax.ShapeDtypeStruct((rows, C_W), BF16),
        jax.ShapeDtypeStruct((C_KV, rows, HEAD_DIM), BF16), jax.ShapeDtypeStruct((rows // BLOCK, C_KVW, BLOCK), BF16),
        jax.ShapeDtypeStruct((rows, C_W), BF16),
    )
    out_specs = (
        pl.BlockSpec((tm, D_MODEL), row), pl.BlockSpec((tm, C_W), row),
        pl.BlockSpec((C_KV, tm, HEAD_DIM), kvrow), pl.BlockSpec((tm // BLOCK, C_KVW, BLOCK), lambda i: (i, 0, 0)),
        pl.BlockSpec((tm, C_W), row),
    )
    in_specs = [
        pl.BlockSpec((tm, D_MODEL), row), pl.BlockSpec((tm, A_W), row), pl.BlockSpec((tm, B_VW), row),
        pl.BlockSpec((A_W + B_VW, D_MODEL), const), pl.BlockSpec((1, D_MODEL), const),
        pl.BlockSpec((D_MODEL, C_IN), const),
    ]
    return pl.pallas_call(
        _mid_proj_kernel, grid=(rows // tm,), in_specs=in_specs, out_specs=out_specs,
        out_shape=out_shape, compiler_params=_params(("parallel",)), name="mid_proj",
    )(x2d, ma, mb, wo, g, wc)


def _win_attn_kernel(q_ref, kp_ref, k_ref, kn_ref, vp_ref, v_ref, vn_ref, bias_ref, sink_ref, sg_ref, o_ref,
                     kw_all, vt_all, s_buf, p_buf, *, blocks_per_seq):
    j = pl.program_id(1)
    nblk = q_ref.shape[0] // BLOCK
    gw = C_GROUP * HEAD_DIM
    for g in range(C_KV):
        kw_all[g, 0:BLOCK] = kp_ref[g]
        kw_all[g, BLOCK:(nblk + 1) * BLOCK] = k_ref[g]
        kw_all[g, (nblk + 1) * BLOCK:(nblk + 2) * BLOCK] = kn_ref[g]
    vt_all[0] = vp_ref[0]
    vt_all[1:nblk + 1] = v_ref[...]
    vt_all[nblk + 1] = vn_ref[0]

    def block_rows(i):
        return pl.ds(pl.multiple_of(i * BLOCK, BLOCK), BLOCK)

    def scores(i, g):
        qs = _stack_heads_on_rows(q_ref[block_rows(i), g * gw:(g + 1) * gw], C_GROUP)
        kw = kw_all[g, pl.ds(pl.multiple_of(i * BLOCK, BLOCK), 3 * BLOCK), :]
        s_buf[g] = lax.dot_general(kw, qs, (((1,), (1,)), ((), ())), preferred_element_type=F32)

    def softmax(i, g):
        block_in_seq = j * nblk + i
        prev_bias = jnp.where(block_in_seq == 0, 3, 0)
        next_bias = jnp.where(block_in_seq == blocks_per_seq - 1, 3, 2)
        s_t = jnp.concatenate([s_buf[g, 0:BLOCK] + bias_ref[prev_bias, g],
                               s_buf[g, BLOCK:2 * BLOCK] + bias_ref[1, g],
                               s_buf[g, 2 * BLOCK:3 * BLOCK] + bias_ref[next_bias, g]], axis=0)
        sink = sink_ref[g]
        mx = jnp.maximum(jnp.max(s_t, axis=0, keepdims=True), sink)
        p_buf[g] = jnp.exp2(s_t - mx).astype(BF16)
        return jnp.exp2(sink - mx)

    def finish(i, g, sink_term):
        heads = slice(g * HEAD_DIM, (g + 1) * HEAD_DIM)
        vw_t = jnp.concatenate([vt_all[i, heads, :], vt_all[i + 1, heads, :], vt_all[i + 2, heads, :]], axis=1)
        acc = jnp.dot(_with_ones_rows(vw_t), p_buf[g], preferred_element_type=F32)
        denom = acc[HEAD_DIM:HEAD_DIM + 1] + sink_term
        o = _unstack_transposed(acc[:HEAD_DIM] / denom, C_GROUP, BLOCK)
        gate = sg_ref[block_rows(i), g * gw:(g + 1) * gw].astype(F32)
        o_ref[block_rows(i), g * gw:(g + 1) * gw] = (o * gate).astype(BF16)

    scores(0, 0)
    scores(0, 1)

    def body(step, sink_term0):
        for u in range(WIN_UNROLL):
            i = step * WIN_UNROLL + u
            nxt = jnp.minimum(i + 1, nblk - 1)
            scores(nxt, 0)
            sink_term1 = softmax(i, 1)
            finish(i, 0, sink_term0)
            scores(nxt, 1)
            sink_term0 = softmax(nxt, 0)
            finish(i, 1, sink_term1)
        return sink_term0

    lax.fori_loop(0, nblk // WIN_UNROLL, body, softmax(0, 0))


def _win_attn(q, k, vt, bias, sink, sg, batch, seq):
    nb = seq // BLOCK
    sup = WIN_SUPER
    nblk = sup // BLOCK
    q3 = q.reshape(batch, seq, C_W)
    sg3 = sg.reshape(batch, seq, C_W)
    k4 = k.reshape(C_KV, batch, seq, HEAD_DIM)
    rowspec = pl.BlockSpec((None, sup, C_W), lambda b, j: (b, j, 0))
    before = lambda j: jnp.maximum(j * nblk - 1, 0)
    after = lambda j: jnp.minimum((j + 1) * nblk, nb - 1)
    halo_k = (C_KV, None, BLOCK, HEAD_DIM)
    halo_v = (1, C_KVW, BLOCK)
    out = pl.pallas_call(
        functools.partial(_win_attn_kernel, blocks_per_seq=nb), grid=(batch, seq // sup),
        in_specs=[rowspec,
                  pl.BlockSpec(halo_k, lambda b, j: (0, b, before(j), 0)),
                  pl.BlockSpec((C_KV, None, sup, HEAD_DIM), lambda b, j: (0, b, j, 0)),
                  pl.BlockSpec(halo_k, lambda b, j: (0, b, after(j), 0)),
                  pl.BlockSpec(halo_v, lambda b, j: (b * nb + before(j), 0, 0)),
                  pl.BlockSpec((nblk, C_KVW, BLOCK), lambda b, j: (b * (nb // nblk) + j, 0, 0)),
                  pl.BlockSpec(halo_v, lambda b, j: (b * nb + after(j), 0, 0)),
                  pl.BlockSpec((4, C_KV, BLOCK, C_GROUP * BLOCK), lambda b, j: (0, 0, 0, 0)),
                  pl.BlockSpec((C_KV, 1, C_GROUP * BLOCK), lambda b, j: (0, 0, 0)), rowspec],
        out_specs=rowspec, out_shape=jax.ShapeDtypeStruct((batch, seq, C_W), BF16),
        scratch_shapes=[pltpu.VMEM((C_KV, sup + 2 * BLOCK, HEAD_DIM), BF16),
                        pltpu.VMEM((nblk + 2, C_KVW, BLOCK), BF16),
                        pltpu.VMEM((C_KV, 3 * BLOCK, C_GROUP * BLOCK), F32),
                        pltpu.VMEM((C_KV, 3 * BLOCK, C_GROUP * BLOCK), BF16)],
        compiler_params=_params(("parallel", "parallel")), name="win_attn",
    )(q3, k4, k4, k4, vt, vt, vt, bias, sink, sg3)
    return out.reshape(batch * seq, C_W)


def _out_proj_kernel(x_ref, m_ref, wo_ref, g_ref, y_ref):
    part_rows = x_ref.shape[0] // PROJ_SPLIT
    for part in range(PROJ_SPLIT):
        rows = slice(part * part_rows, (part + 1) * part_rows)
        x2 = x_ref[rows, :] + jnp.dot(m_ref[rows, :], wo_ref[...], preferred_element_type=F32)
        y_ref[rows, :] = _rmsnorm_rows(x2, g_ref[...])


def _out_proj(x2d, m, wo, g):
    rows = x2d.shape[0]
    tm = ROW_TILE
    row = lambda i: (i, 0)
    const = lambda i: (0, 0)
    return pl.pallas_call(
        _out_proj_kernel, grid=(rows // tm,),
        in_specs=[pl.BlockSpec((tm, D_MODEL), row), pl.BlockSpec((tm, C_W), row),
                  pl.BlockSpec((C_W, D_MODEL), const), pl.BlockSpec((1, D_MODEL), const)],
        out_specs=pl.BlockSpec((tm, D_MODEL), row), out_shape=jax.ShapeDtypeStruct((rows, D_MODEL), F32),
        compiler_params=_params(("parallel",)), name="out_proj",
    )(x2d, m, wo, g)


def _trunk(x, prm):
    batch, seq, _ = x.shape
    x2d = x.reshape(batch * seq, D_MODEL)
    qa, ka, va, sga, qb, kb, vb, sgb = _inproj_ab(
        x2d, seq, prm["g0"], prm["w_in_ab"], prm["gain"], prm["ones"], prm["tabs"])
    ma = _dense_attn(qa, ka, va, sga, batch, seq)
    mb = _retention(prm["log_gamma"], qb, kb, vb, sgb, batch, seq)
    x1, qc, kc, vc, sgc = _mid_proj(x2d, ma, mb, prm["w_out_ab"], prm["g1"], prm["w_in_c"])
    mc = _win_attn(qc, kc, vc, prm["bias"], prm["sink"], sgc, batch, seq)
    y = _out_proj(x1, mc, prm["w_out_c"], prm["gf"])
    return y.reshape(batch, seq, D_MODEL)


def kernel(x_prompt, x_sample, norm_g, w_in_ab, qk_norm_a, ret_decay, w_out_ab, w_in_c, sink_c, w_out_c,
           rel_bias, final_norm):
    assert w_in_ab.shape[0] == 1 and w_in_c.shape[0] == 1, "two-layer trunk: one layer of each kind"
    max_seq = max(x_prompt.shape[1], x_sample.shape[1])

    w_ab = w_in_ab[0].astype(BF16)
    gain = jnp.stack([jnp.tile(qk_norm_a[0, 0], 2) * (HEAD_DIM ** -0.5 * LOG2E),
                      jnp.tile(qk_norm_a[0, 1], 2)]).astype(F32)
    ones = np.kron(np.eye(2 * LANES // HEAD_DIM), np.ones((HEAD_DIM, HEAD_DIM)))

    rel = np.arange(3 * BLOCK)[:, None] - BLOCK - np.arange(BLOCK)[None, :]
    bucket_onehot = jax.nn.one_hot(jnp.asarray(_t5_bucket(rel).reshape(-1)), REL_BUCKETS, dtype=F32, axis=0)
    bias = jnp.einsum("bh,bn->hn", rel_bias.astype(F32) * LOG2E, bucket_onehot,
                      precision=lax.Precision.HIGHEST)
    bias = jnp.where(jnp.asarray((np.abs(rel) <= WINDOW).reshape(1, -1)), bias, NEG)
    bias = bias.reshape(C_KV, C_GROUP, 3, BLOCK, BLOCK).transpose(2, 0, 3, 1, 4)
    bias = bias.reshape(3, C_KV, BLOCK, C_GROUP * BLOCK)
    bias = jnp.concatenate([bias, jnp.full((1,) + bias.shape[1:], NEG, F32)], axis=0)
    sink = jnp.repeat(sink_c[0].astype(F32) * LOG2E, BLOCK).reshape(C_KV, 1, C_GROUP * BLOCK)

    prm = {
        "g0": norm_g[0].reshape(1, D_MODEL), "g1": norm_g[1].reshape(1, D_MODEL),
        "gf": final_norm.reshape(1, D_MODEL),
        "w_in_ab": w_ab, "w_out_ab": w_out_ab[0].astype(BF16),
        "w_in_c": w_in_c[0].astype(BF16), "w_out_c": w_out_c[0].astype(BF16),
        "gain": gain, "ones": jnp.asarray(ones, BF16),
        "tabs": _rope_tables(_axial_angles(max_seq)) + _rope_tables(_linear_angles(max_seq)),
        "log_gamma": -jnp.exp(ret_decay[0].astype(F32)),
        "bias": bias, "sink": sink,
    }
    return _trunk(x_prompt, prm), _trunk(x_sample, prm)
```

```python
import functools

import jax
import jax.numpy as jnp
import numpy as np
from jax import lax
from jax.experimental import pallas as pl
from jax.experimental.pallas import tpu as pltpu

F32 = jnp.float32
BF16 = jnp.bfloat16

D_MODEL = 1024
GRID_W = 64
EPS = 1e-6
ROPE_THETA = 10000.0
HEAD_DIM = 64
HALF = HEAD_DIM // 2
LANES = 128

A_HEADS, A_KV = 8, 2
A_GROUP = A_HEADS // A_KV
A_W, A_KVW = A_HEADS * HEAD_DIM, A_KV * HEAD_DIM
B_HEADS, B_VAL = 4, 128
B_QKW, B_VW = B_HEADS * HEAD_DIM, B_HEADS * B_VAL
C_HEADS, C_KV = 16, 2
C_GROUP = C_HEADS // C_KV
C_W, C_KVW = C_HEADS * HEAD_DIM, C_KV * HEAD_DIM
BLOCK = 128
WINDOW = 128
REL_BUCKETS, REL_MAX_DIST = 32, 128
CHUNK = 128
RET_UNROLL = 32
NEG = -1e30
LOG2E = float(np.log2(np.e))
ONES_ROWS = 16

AB_QA, AB_KA, AB_VA, AB_GA = 0, A_W, A_W + A_KVW, A_W + 2 * A_KVW
AB_QB = AB_GA + A_W
AB_KB, AB_VB, AB_GB = AB_QB + B_QKW, AB_QB + 2 * B_QKW, AB_QB + 2 * B_QKW + B_VW
AB_IN = AB_GB + B_VW
C_Q, C_K, C_V, C_G = 0, C_W, C_W + C_KVW, C_W + 2 * C_KVW
C_IN = C_G + C_W

PROJ_SPLIT = 2
ROW_TILE = 1024
ATT_TQ = 256
ATT_TK_MAX = 1024
ATT_SEQ_CHUNKS = 4
ATT_BODY_KEYS = 32768
WIN_SUPER = 2048
WIN_UNROLL = 16
VMEM_LIMIT = 48 * 1024 * 1024


def _att_tk(seq):
    return min(ATT_TK_MAX, seq // ATT_SEQ_CHUNKS)


def _att_tq(seq):
    del seq
    return ATT_TQ


def _params(sem, flags=None):
    return pltpu.CompilerParams(dimension_semantics=sem, vmem_limit_bytes=VMEM_LIMIT, flags=flags)


def _rope_tables(ang):
    c, s = np.repeat(np.cos(ang), 2, axis=-1), np.repeat(np.sin(ang), 2, axis=-1)
    s[:, 0::2] *= -1.0
    return jnp.asarray(np.tile(c, (1, 2)), F32), jnp.asarray(np.tile(s, (1, 2)), F32)


def _axial_angles(n):
    t = np.arange(n)
    quarter = HEAD_DIM // 4
    freqs = ROPE_THETA ** (-np.arange(quarter, dtype=np.float64) / quarter)
    row = (t // GRID_W).astype(np.float64)
    col = (t % GRID_W).astype(np.float64)
    return np.concatenate([row[:, None] * freqs, col[:, None] * freqs], axis=-1)


def _linear_angles(n):
    freqs = ROPE_THETA ** (-np.arange(HALF, dtype=np.float64) / HALF)
    return np.arange(n, dtype=np.float64)[:, None] * freqs


def _t5_bucket(rel):
    half = REL_BUCKETS // 2
    max_exact = half // 2
    ret = (rel > 0).astype(np.int32) * half
    dist = np.abs(rel)
    large = max_exact + (np.log(np.maximum(dist, 1) / max_exact) / np.log(REL_MAX_DIST / max_exact)
                         * (half - max_exact)).astype(np.int32)
    large = np.minimum(large, half - 1)
    return ret + np.where(dist < max_exact, dist, large)


def _rmsnorm_rows(x, gain):
    return x * lax.rsqrt(jnp.mean(x * x, axis=-1, keepdims=True) + EPS) * gain


def _silu(z):
    return z * jax.nn.sigmoid(z)


def _rotate_pairs(x, cos_t, sin_t):
    lane = lax.broadcasted_iota(jnp.int32, (1, LANES), 1)
    partner = jnp.where(lane % 2 == 0, pltpu.roll(x, LANES - 1, 1), pltpu.roll(x, 1, 1))
    return x * cos_t + partner * sin_t


def _transposed_chunks_store(ref, tile, row0=0):
    chunk = ref.shape[2]
    rows = tile.shape[0]
    tile_t = tile.T
    step = min(chunk, rows)
    for s in range(0, rows, step):
        c, off = divmod(row0 + s, chunk)
        ref[c, :, off:off + step] = tile_t[:, s:s + step].astype(ref.dtype)


def _stack_heads_on_rows(q, heads):
    return jnp.concatenate([q[:, h * HEAD_DIM:(h + 1) * HEAD_DIM] for h in range(heads)], axis=0)


def _unstack_transposed(o_t, heads, tokens):
    by_head = jnp.concatenate([o_t[:, h * tokens:(h + 1) * tokens] for h in range(heads)], axis=0)
    return by_head.T


def _with_ones_rows(v_t):
    return jnp.concatenate([v_t, jnp.ones((ONES_ROWS, v_t.shape[1]), v_t.dtype)], axis=0)


def _inproj_ab_kernel(x_ref, g_ref, w_ref, gain_ref, ones_ref, cosa_ref, sina_ref, cosb_ref, sinb_ref,
                      qa_ref, ka_ref, va_ref, sga_ref, qb_ref, kb_ref, vb_ref, sgb_ref):
    part_rows = x_ref.shape[0] // PROJ_SPLIT
    for part in range(PROJ_SPLIT):
        rows = slice(part * part_rows, (part + 1) * part_rows)
        xn = _rmsnorm_rows(x_ref[rows, :], g_ref[...]).astype(BF16)

        def proj(lo, hi):
            return jnp.dot(xn, w_ref[:, lo:hi], preferred_element_type=F32)

        cosa, sina = cosa_ref[rows, :], sina_ref[rows, :]
        cosb, sinb = cosb_ref[rows, :], sinb_ref[rows, :]

        def head_norm_rope(z, gain):
            width = z.shape[1]
            ss = jnp.dot((z * z).astype(BF16), ones_ref[0:width, 0:width], preferred_element_type=F32)
            z = z * lax.rsqrt(ss * (1.0 / HEAD_DIM) + EPS)
            return [_rotate_pairs(z[:, c * LANES:(c + 1) * LANES] * gain, cosa, sina)
                    for c in range(width // LANES)]

        zq = proj(AB_QA, AB_KA)
        for pair in range(A_W // (2 * LANES)):
            tiles = head_norm_rope(zq[:, pair * 2 * LANES:(pair + 1) * 2 * LANES], gain_ref[0:1, :])
            for c, tile in enumerate(tiles):
                lo = (2 * pair + c) * LANES
                qa_ref[rows, lo:lo + LANES] = tile.astype(BF16)
        zkv = proj(AB_KA, AB_GA)
        k_tile, = head_norm_rope(zkv[:, :LANES], gain_ref[1:2, :])
        ka_ref[0, rows, :] = k_tile[:, :HEAD_DIM].astype(BF16)
        ka_ref[1, rows, :] = k_tile[:, HEAD_DIM:].astype(BF16)
        _transposed_chunks_store(va_ref, zkv[:, LANES:], part * part_rows)
        sga_ref[rows, :] = _silu(proj(AB_GA, AB_QB)).astype(BF16)
        sgb_ref[rows, :] = _silu(proj(AB_GB, AB_IN)).astype(BF16)

        zqk = proj(AB_QB, AB_VB)
        for c in range(B_QKW // LANES):
            qb_ref[rows, c * LANES:(c + 1) * LANES] = _rotate_pairs(
                zqk[:, c * LANES:(c + 1) * LANES], cosb, sinb).astype(BF16)
            kt = zqk[:, B_QKW + c * LANES:B_QKW + (c + 1) * LANES] * (HEAD_DIM ** -0.5)
            kb_ref[rows, c * LANES:(c + 1) * LANES] = _rotate_pairs(kt, cosb, sinb).astype(BF16)
        vb_ref[rows, :] = proj(AB_VB, AB_GB).astype(BF16)


def _inproj_ab(x2d, seq, g, w, gain, ones, tabs):
    rows = x2d.shape[0]
    tm = ROW_TILE
    per_seq = seq // tm
    row = lambda i: (i, 0)
    const = lambda i: (0, 0)
    tab = lambda i: (i % per_seq, 0)
    kvrow = lambda i: (0, i, 0)
    tk = _att_tk(seq)
    vt_chunks = tm // tk
    out_shape = (
        jax.ShapeDtypeStruct((rows, A_W), BF16), jax.ShapeDtypeStruct((A_KV, rows, HEAD_DIM), BF16),
        jax.ShapeDtypeStruct((rows // tk, A_KVW, tk), BF16), jax.ShapeDtypeStruct((rows, A_W), BF16),
        jax.ShapeDtypeStruct((rows, B_QKW), BF16), jax.ShapeDtypeStruct((rows, B_QKW), BF16),
        jax.ShapeDtypeStruct((rows, B_VW), BF16), jax.ShapeDtypeStruct((rows, B_VW), BF16),
    )
    out_specs = (
        pl.BlockSpec((tm, A_W), row), pl.BlockSpec((A_KV, tm, HEAD_DIM), kvrow),
        pl.BlockSpec((vt_chunks, A_KVW, tk), lambda i: (i, 0, 0)), pl.BlockSpec((tm, A_W), row),
        pl.BlockSpec((tm, B_QKW), row), pl.BlockSpec((tm, B_QKW), row),
        pl.BlockSpec((tm, B_VW), row), pl.BlockSpec((tm, B_VW), row),
    )
    in_specs = [
        pl.BlockSpec((tm, D_MODEL), row), pl.BlockSpec((1, D_MODEL), const),
        pl.BlockSpec((D_MODEL, AB_IN), const), pl.BlockSpec((2, LANES), const),
        pl.BlockSpec((2 * LANES, 2 * LANES), const),
    ] + [pl.BlockSpec((tm, LANES), tab)] * 4
    return pl.pallas_call(
        _inproj_ab_kernel, grid=(rows // tm,), in_specs=in_specs, out_specs=out_specs,
        out_shape=out_shape, compiler_params=_params(("parallel",)), name="inproj_ab",
    )(x2d, g, w, gain, ones, *tabs)


def _dense_attn_kernel(q_ref, k_ref, vt_ref, sg_ref, o_ref, s_buf, p_buf, *, seq):
    tq, tk = _att_tq(seq), _att_tk(seq)
    rows = A_GROUP * tq
    n = seq // tk
    blocks = seq // tq
    assert n % 2 == 0 and n >= 2

    def query_rows(i):
        return pl.ds(pl.multiple_of(i * tq, tq), tq)

    def stacked_q(i):
        return _stack_heads_on_rows(q_ref[query_rows(i), :], A_GROUP)

    def scores(qs, c):
        k = k_ref[c * tk:(c + 1) * tk, :]
        s_t = lax.dot_general(k, qs, (((1,), (1,)), ((), ())), preferred_element_type=F32)
        s_buf[c % 2] = s_t
        return jnp.max(s_t, axis=0, keepdims=True)

    def softmax(m, c, chunk_max):
        m_new = jnp.maximum(m, chunk_max)
        p_buf[c % 2] = jnp.exp2(s_buf[c % 2] - m_new).astype(BF16)
        return m_new, jnp.exp2(m - m_new)

    fresh_m = jnp.full((1, rows), NEG, F32)
    first_q = stacked_q(0)
    first_max = [scores(first_q, 0), scores(first_q, 1)]

    def block(i, m, alpha, max0, max1):
        chunk_max = [max0, max1]
        qs = stacked_q(i)
        qs_next = stacked_q(jnp.minimum(i + 1, blocks - 1))
        acc = jnp.zeros((HEAD_DIM + ONES_ROWS, rows), F32)
        for t in range(n):
            if t + 2 < n:
                scored = scores(qs, t + 2)
            else:
                scored = scores(qs_next, t + 2 - n)
            if t + 1 < n:
                m, alpha_next = softmax(m, t + 1, chunk_max[(t + 1) % 2])
            else:
                m, alpha_next = softmax(fresh_m, 0, chunk_max[0])
            chunk_max[t % 2] = scored
            pv = jnp.dot(_with_ones_rows(vt_ref[t]), p_buf[t % 2], preferred_element_type=F32)
            acc = alpha * acc + pv
            alpha = alpha_next
        o_t = acc[:HEAD_DIM] / acc[HEAD_DIM:HEAD_DIM + 1]
        o = _unstack_transposed(o_t, A_GROUP, tq)
        o_ref[query_rows(i), :] = (o * sg_ref[query_rows(i), :].astype(F32)).astype(BF16)
        return m, alpha, chunk_max[0], chunk_max[1]

    per_body = min(blocks, max(1, ATT_BODY_KEYS // seq))

    def body(step, carry):
        for u in range(per_body):
            carry = block(step * per_body + u, *carry)
        return carry

    lax.fori_loop(0, blocks // per_body, body, softmax(fresh_m, 0, first_max[0]) + tuple(first_max))


def _dense_attn(qa, ka, vat, sga, batch, seq):
    gw = A_GROUP * HEAD_DIM
    q3 = qa.reshape(batch, seq, A_W)
    sg3 = sga.reshape(batch, seq, A_W)
    k4 = ka.reshape(A_KV, batch, seq, HEAD_DIM)
    tk = _att_tk(seq)
    vt4 = vat.reshape(batch, seq // tk, A_KVW, tk)
    qspec = pl.BlockSpec((None, seq, gw), lambda b, g: (b, 0, g))
    kspec = pl.BlockSpec((None, None, seq, HEAD_DIM), lambda b, g: (g, b, 0, 0))
    vtspec = pl.BlockSpec((None, seq // tk, HEAD_DIM, tk), lambda b, g: (b, 0, g, 0))
    out = pl.pallas_call(
        functools.partial(_dense_attn_kernel, seq=seq), grid=(batch, A_KV),
        in_specs=[qspec, kspec, vtspec, qspec], out_specs=qspec,
        out_shape=jax.ShapeDtypeStruct((batch, seq, A_W), BF16),
        scratch_shapes=[pltpu.VMEM((2, tk, A_GROUP * _att_tq(seq)), F32),
                        pltpu.VMEM((2, tk, A_GROUP * _att_tq(seq)), BF16)],
        compiler_params=_params(("parallel", "parallel")), name="dense_attn",
    )(q3, k4, vt4, sg3)
    return out.reshape(batch * seq, A_W)


def _retention_kernel(lg_ref, q_ref, k_ref, v_ref, sg_ref, o_ref, st_ref, *, seq):
    c = CHUNK
    nc = seq // c
    pv = 2 * B_VAL
    pair = pl.program_id(1)
    lgf = [lg_ref[0, 2 * pair + hh] for hh in range(2)]
    lgb = [lg_ref[1, 2 * pair + hh] for hh in range(2)]
    lane = lax.broadcasted_iota(jnp.int32, (1, LANES), 1)
    first_qk_head = lane < HEAD_DIM
    lane_lgf = jnp.where(first_qk_head, lgf[0], lgf[1])
    lane_lgb = jnp.where(first_qk_head, lgb[0], lgb[1])
    t = lax.broadcasted_iota(jnp.int32, (c, 1), 0).astype(F32)
    kdec_f = jnp.exp((c - 1.0 - t) * lane_lgf)
    kdec_b = jnp.exp(t * lane_lgb)
    qdec_f = jnp.exp((t + 1.0) * lane_lgf)
    qdec_b = jnp.exp((c - t) * lane_lgb)
    row_first = lax.broadcasted_iota(jnp.int32, (LANES, 1), 0) < HEAD_DIM
    first_v_head = lax.broadcasted_iota(jnp.int32, (1, pv), 1) < B_VAL
    same_head = row_first == first_v_head
    cdec_f = jnp.exp(c * jnp.where(row_first, lgf[0], lgf[1]))
    cdec_b = jnp.exp(c * jnp.where(row_first, lgb[0], lgb[1]))
    diff = (lax.broadcasted_iota(jnp.int32, (c, c), 0) - lax.broadcasted_iota(jnp.int32, (c, c), 1)).astype(F32)
    intra_decay = jnp.concatenate(
        [jnp.where(diff >= 0, jnp.exp(lgf[hh] * jnp.maximum(diff, 0.0)), jnp.exp(lgb[hh] * jnp.maximum(-diff, 0.0)))
         for hh in range(2)], axis=1)

    def chunk_rows(n):
        return pl.ds(pl.multiple_of(n * c, c), c)

    def kv_body(n, carry):
        rows = chunk_rows(n)
        k = k_ref[rows, :].astype(F32)
        v = v_ref[rows, :]
        dn = (((0,), (0,)), ((), ()))
        kvf = lax.dot_general((k * kdec_f).astype(BF16), v, dn, preferred_element_type=F32)
        kvb = lax.dot_general((k * kdec_b).astype(BF16), v, dn, preferred_element_type=F32)
        st_ref[n, 0:LANES, :] = jnp.where(same_head, kvf, 0.0)
        st_ref[n, LANES:2 * LANES, :] = jnp.where(same_head, kvb, 0.0)
        return carry

    lax.fori_loop(0, nc, kv_body, 0, unroll=RET_UNROLL)

    def scan_fwd(n, state):
        kv = st_ref[n, 0:LANES, :]
        st_ref[n, 0:LANES, :] = state
        return state * cdec_f + kv

    lax.fori_loop(0, nc, scan_fwd, jnp.zeros((LANES, pv), F32))

    def scan_bwd(i, state):
        n = nc - 1 - i
        kv = st_ref[n, LANES:2 * LANES, :]
        st_ref[n, LANES:2 * LANES, :] = state
        return state * cdec_b + kv

    lax.fori_loop(0, nc, scan_bwd, jnp.zeros((LANES, pv), F32))

    def out_body(n, carry):
        rows = chunk_rows(n)
        q = q_ref[rows, :].astype(F32)
        k = k_ref[rows, :]
        v = v_ref[rows, :]
        zero = jnp.zeros_like(k)
        k_by_head = jnp.concatenate([jnp.where(first_qk_head, k, zero), jnp.where(first_qk_head, zero, k)], axis=0)
        s = lax.dot_general(q.astype(BF16), k_by_head, (((1,), (1,)), ((), ())),
                            preferred_element_type=F32)
        zero_v = jnp.zeros_like(v)
        v_by_head = jnp.concatenate([jnp.where(first_v_head, v, zero_v), jnp.where(first_v_head, zero_v, v)], axis=0)
        lhs = jnp.concatenate([s * intra_decay, q * qdec_f, q * qdec_b], axis=1).astype(BF16)
        rhs = jnp.concatenate([v_by_head, st_ref[n].astype(BF16)], axis=0)
        o_pair = jnp.dot(lhs, rhs, preferred_element_type=F32)
        for hh in range(2):
            o = o_pair[:, hh * B_VAL:(hh + 1) * B_VAL]
            o = o * lax.rsqrt(jnp.mean(o * o, axis=-1, keepdims=True) + EPS)
            o = o * sg_ref[rows, hh * B_VAL:(hh + 1) * B_VAL].astype(F32)
            o_ref[rows, hh * B_VAL:(hh + 1) * B_VAL] = o.astype(BF16)
        return carry

    lax.fori_loop(0, nc, out_body, 0, unroll=RET_UNROLL)


def _retention(log_gamma, qb, kb, vb, sgb, batch, seq):
    pairs = B_HEADS // 2
    qk_spec = pl.BlockSpec((None, seq, LANES), lambda b, j: (b, 0, j))
    v_spec = pl.BlockSpec((None, seq, 2 * B_VAL), lambda b, j: (b, 0, j))
    out = pl.pallas_call(
        functools.partial(_retention_kernel, seq=seq), grid=(batch, pairs),
        in_specs=[pl.BlockSpec(memory_space=pltpu.SMEM), qk_spec, qk_spec, v_spec, v_spec],
        out_specs=v_spec, out_shape=jax.ShapeDtypeStruct((batch, seq, B_VW), BF16),
        scratch_shapes=[pltpu.VMEM((seq // CHUNK, 2 * LANES, 2 * B_VAL), F32)],
        compiler_params=_params(("parallel", "parallel")), name="retention",
    )(log_gamma, qb.reshape(batch, seq, B_QKW), kb.reshape(batch, seq, B_QKW),
      vb.reshape(batch, seq, B_VW), sgb.reshape(batch, seq, B_VW))
    return out.reshape(batch * seq, B_VW)


def _mid_proj_kernel(x_ref, ma_ref, mb_ref, wo_ref, g_ref, wc_ref, x1_ref, q_ref, k_ref, v_ref, sg_ref):
    part_rows = x_ref.shape[0] // PROJ_SPLIT
    for part in range(PROJ_SPLIT):
        rows = slice(part * part_rows, (part + 1) * part_rows)
        y = jnp.dot(ma_ref[rows, :], wo_ref[0:A_W, :], preferred_element_type=F32)
        y += jnp.dot(mb_ref[rows, :], wo_ref[A_W:A_W + B_VW, :], preferred_element_type=F32)
        x1 = x_ref[rows, :] + y
        x1_ref[rows, :] = x1
        xn = _rmsnorm_rows(x1, g_ref[...]).astype(BF16)

        def proj(lo, hi):
            return jnp.dot(xn, wc_ref[:, lo:hi], preferred_element_type=F32)

        sg_ref[rows, :] = _silu(proj(C_G, C_IN)).astype(BF16)
        zkv = proj(C_K, C_G)
        k_ref[0, rows, :] = zkv[:, :HEAD_DIM].astype(BF16)
        k_ref[1, rows, :] = zkv[:, HEAD_DIM:LANES].astype(BF16)
        _transposed_chunks_store(v_ref, zkv[:, LANES:], part * part_rows)
        q_ref[rows, :] = (proj(C_Q, C_K) * (HEAD_DIM ** -0.5 * LOG2E)).astype(BF16)


def _mid_proj(x2d, ma, mb, wo, g, wc):
    rows = x2d.shape[0]
    tm = ROW_TILE
    row = lambda i: (i, 0)
    const = lambda i: (0, 0)
    kvrow = lambda i: (0, i, 0)
    out_shape = (
        jax.ShapeDtypeStruct((rows, D_MODEL), F32), jax.ShapeDtypeStruct((rows, C_W), BF16),
        jax.ShapeDtypeStruct((C_KV, rows, HEAD_DIM), BF16), jax.ShapeDtypeStruct((rows // BLOCK, C_KVW, BLOCK), BF16),
        jax.ShapeDtypeStruct((rows, C_W), BF16),
    )
    out_specs = (
        pl.BlockSpec((tm, D_MODEL), row), pl.BlockSpec((tm, C_W), row),
        pl.BlockSpec((C_KV, tm, HEAD_DIM), kvrow), pl.BlockSpec((tm // BLOCK, C_KVW, BLOCK), lambda i: (i, 0, 0)),
        pl.BlockSpec((tm, C_W), row),
    )
    in_specs = [
        pl.BlockSpec((tm, D_MODEL), row), pl.BlockSpec((tm, A_W), row), pl.BlockSpec((tm, B_VW), row),
        pl.BlockSpec((A_W + B_VW, D_MODEL), const), pl.BlockSpec((1, D_MODEL), const),
        pl.BlockSpec((D_MODEL, C_IN), const),
    ]
    return pl.pallas_call(
        _mid_proj_kernel, grid=(rows // tm,), in_specs=in_specs, out_specs=out_specs,
        out_shape=out_shape, compiler_params=_params(("parallel",)), name="mid_proj",
    )(x2d, ma, mb, wo, g, wc)


def _win_attn_kernel(q_ref, kp_ref, k_ref, kn_ref, vp_ref, v_ref, vn_ref, bias_ref, sink_ref, sg_ref, o_ref,
                     kw_all, vt_all, s_buf, p_buf, *, blocks_per_seq):
    j = pl.program_id(1)
    nblk = q_ref.shape[0] // BLOCK
    gw = C_GROUP * HEAD_DIM
    for g in range(C_KV):
        kw_all[g, 0:BLOCK] = kp_ref[g]
        kw_all[g, BLOCK:(nblk + 1) * BLOCK] = k_ref[g]
        kw_all[g, (nblk + 1) * BLOCK:(nblk + 2) * BLOCK] = kn_ref[g]
    vt_all[0] = vp_ref[0]
    vt_all[1:nblk + 1] = v_ref[...]
    vt_all[nblk + 1] = vn_ref[0]

    def block_rows(i):
        return pl.ds(pl.multiple_of(i * BLOCK, BLOCK), BLOCK)

    def scores(i, g):
        qs = _stack_heads_on_rows(q_ref[block_rows(i), g * gw:(g + 1) * gw], C_GROUP)
        kw = kw_all[g, pl.ds(pl.multiple_of(i * BLOCK, BLOCK), 3 * BLOCK), :]
        s_buf[g] = lax.dot_general(kw, qs, (((1,), (1,)), ((), ())), preferred_element_type=F32)

    def softmax(i, g):
        block_in_seq = j * nblk + i
        prev_bias = jnp.where(block_in_seq == 0, 3, 0)
        next_bias = jnp.where(block_in_seq == blocks_per_seq - 1, 3, 2)
        s_t = jnp.concatenate([s_buf[g, 0:BLOCK] + bias_ref[prev_bias, g],
                               s_buf[g, BLOCK:2 * BLOCK] + bias_ref[1, g],
                               s_buf[g, 2 * BLOCK:3 * BLOCK] + bias_ref[next_bias, g]], axis=0)
        sink = sink_ref[g]
        mx = jnp.maximum(jnp.max(s_t, axis=0, keepdims=True), sink)
        p_buf[g] = jnp.exp2(s_t - mx).astype(BF16)
        return jnp.exp2(sink - mx)

    def finish(i, g, sink_term):
        heads = slice(g * HEAD_DIM, (g + 1) * HEAD_DIM)
        vw_t = jnp.concatenate([vt_all[i, heads, :], vt_all[i + 1, heads, :], vt_all[i + 2, heads, :]], axis=1)
        acc = jnp.dot(_with_ones_rows(vw_t), p_buf[g], preferred_element_type=F32)
        denom = acc[HEAD_DIM:HEAD_DIM + 1] + sink_term
        o = _unstack_transposed(acc[:HEAD_DIM] / denom, C_GROUP, BLOCK)
        gate = sg_ref[block_rows(i), g * gw:(g + 1) * gw].astype(F32)
        o_ref[block_rows(i), g * gw:(g + 1) * gw] = (o * gate).astype(BF16)

    scores(0, 0)
    scores(0, 1)

    def body(step, sink_term0):
        for u in range(WIN_UNROLL):
            i = step * WIN_UNROLL + u
            nxt = jnp.minimum(i + 1, nblk - 1)
            scores(nxt, 0)
            sink_term1 = softmax(i, 1)
            finish(i, 0, sink_term0)
            scores(nxt, 1)
            sink_term0 = softmax(nxt, 0)
            finish(i, 1, sink_term1)
        return sink_term0

    lax.fori_loop(0, nblk // WIN_UNROLL, body, softmax(0, 0))


def _win_attn(q, k, vt, bias, sink, sg, batch, seq):
    nb = seq // BLOCK
    sup = WIN_SUPER
    nblk = sup // BLOCK
    q3 = q.reshape(batch, seq, C_W)
    sg3 = sg.reshape(batch, seq, C_W)
    k4 = k.reshape(C_KV, batch, seq, HEAD_DIM)
    rowspec = pl.BlockSpec((None, sup, C_W), lambda b, j: (b, j, 0))
    before = lambda j: jnp.maximum(j * nblk - 1, 0)
    after = lambda j: jnp.minimum((j + 1) * nblk, nb - 1)
    halo_k = (C_KV, None, BLOCK, HEAD_DIM)
    halo_v = (1, C_KVW, BLOCK)
    out = pl.pallas_call(
        functools.partial(_win_attn_kernel, blocks_per_seq=nb), grid=(batch, seq // sup),
        in_specs=[rowspec,
                  pl.BlockSpec(halo_k, lambda b, j: (0, b, before(j), 0)),
                  pl.BlockSpec((C_KV, None, sup, HEAD_DIM), lambda b, j: (0, b, j, 0)),
                  pl.BlockSpec(halo_k, lambda b, j: (0, b, after(j), 0)),
                  pl.BlockSpec(halo_v, lambda b, j: (b * nb + before(j), 0, 0)),
                  pl.BlockSpec((nblk, C_KVW, BLOCK), lambda b, j: (b * (nb // nblk) + j, 0, 0)),
                  pl.BlockSpec(halo_v, lambda b, j: (b * nb + after(j), 0, 0)),
                  pl.BlockSpec((4, C_KV, BLOCK, C_GROUP * BLOCK), lambda b, j: (0, 0, 0, 0)),
                  pl.BlockSpec((C_KV, 1, C_GROUP * BLOCK), lambda b, j: (0, 0, 0)), rowspec],
        out_specs=rowspec, out_shape=jax.ShapeDtypeStruct((batch, seq, C_W), BF16),
        scratch_shapes=[pltpu.VMEM((C_KV, sup + 2 * BLOCK, HEAD_DIM), BF16),
                        pltpu.VMEM((nblk + 2, C_KVW, BLOCK), BF16),
                        pltpu.VMEM((C_KV, 3 * BLOCK, C_GROUP * BLOCK), F32),
                        pltpu.VMEM((C_KV, 3 * BLOCK, C_GROUP * BLOCK), BF16)],
        compiler_params=_params(("parallel", "parallel")), name="win_attn",
    )(q3, k4, k4, k4, vt, vt, vt, bias, sink, sg3)
    return out.reshape(batch * seq, C_W)


def _out_proj_kernel(x_ref, m_ref, wo_ref, g_ref, y_ref):
    part_rows = x_ref.shape[0] // PROJ_SPLIT
    for part in range(PROJ_SPLIT):
        rows = slice(part * part_rows, (part + 1) * part_rows)
        x2 = x_ref[rows, :] + jnp.dot(m_ref[rows, :], wo_ref[...], preferred_element_type=F32)
        y_ref[rows, :] = _rmsnorm_rows(x2, g_ref[...])


def _out_proj(x2d, m, wo, g):
    rows = x2d.shape[0]
    tm = ROW_TILE
    row = lambda i: (i, 0)
    const = lambda i: (0, 0)
    return pl.pallas_call(
        _out_proj_kernel, grid=(rows // tm,),
        in_specs=[pl.BlockSpec((tm, D_MODEL), row), pl.BlockSpec((tm, C_W), row),
                  pl.BlockSpec((C_W, D_MODEL), const), pl.BlockSpec((1, D_MODEL), const)],
        out_specs=pl.BlockSpec((tm, D_MODEL), row), out_shape=jax.ShapeDtypeStruct((rows, D_MODEL), F32),
        compiler_params=_params(("parallel",)), name="out_proj",
    )(x2d, m, wo, g)


def _trunk(x, prm):
    batch, seq, _ = x.shape
    x2d = x.reshape(batch * seq, D_MODEL)
    qa, ka, va, sga, qb, kb, vb, sgb = _inproj_ab(
        x2d, seq, prm["g0"], prm["w_in_ab"], prm["gain"], prm["ones"], prm["tabs"])
    ma = _dense_attn(qa, ka, va, sga, batch, seq)
    mb = _retention(prm["log_gamma"], qb, kb, vb, sgb, batch, seq)
    x1, qc, kc, vc, sgc = _mid_proj(x2d, ma, mb, prm["w_out_ab"], prm["g1"], prm["w_in_c"])
    mc = _win_attn(qc, kc, vc, prm["bias"], prm["sink"], sgc, batch, seq)
    y = _out_proj(x1, mc, prm["w_out_c"], prm["gf"])
    return y.reshape(batch, seq, D_MODEL)


def kernel(x_prompt, x_sample, norm_g, w_in_ab, qk_norm_a, ret_decay, w_out_ab, w_in_c, sink_c, w_out_c,
           rel_bias, final_norm):
    assert w_in_ab.shape[0] == 1 and w_in_c.shape[0] == 1, "two-layer trunk: one layer of each kind"
    max_seq = max(x_prompt.shape[1], x_sample.shape[1])

    w_ab = w_in_ab[0].astype(BF16)
    gain = jnp.stack([jnp.tile(qk_norm_a[0, 0], 2) * (HEAD_DIM ** -0.5 * LOG2E),
                      jnp.tile(qk_norm_a[0, 1], 2)]).astype(F32)
    ones = np.kron(np.eye(2 * LANES // HEAD_DIM), np.ones((HEAD_DIM, HEAD_DIM)))

    rel = np.arange(3 * BLOCK)[:, None] - BLOCK - np.arange(BLOCK)[None, :]
    bucket_onehot = jax.nn.one_hot(jnp.asarray(_t5_bucket(rel).reshape(-1)), REL_BUCKETS, dtype=F32, axis=0)
    bias = jnp.einsum("bh,bn->hn", rel_bias.astype(F32) * LOG2E, bucket_onehot,
                      precision=lax.Precision.HIGHEST)
    bias = jnp.where(jnp.asarray((np.abs(rel) <= WINDOW).reshape(1, -1)), bias, NEG)
    bias = bias.reshape(C_KV, C_GROUP, 3, BLOCK, BLOCK).transpose(2, 0, 3, 1, 4)
    bias = bias.reshape(3, C_KV, BLOCK, C_GROUP * BLOCK)
    bias = jnp.concatenate([bias, jnp.full((1,) + bias.shape[1:], NEG, F32)], axis=0)
    sink = jnp.repeat(sink_c[0].astype(F32) * LOG2E, BLOCK).reshape(C_KV, 1, C_GROUP * BLOCK)

    prm = {
        "g0": norm_g[0].reshape(1, D_MODEL), "g1": norm_g[1].reshape(1, D_MODEL),
        "gf": final_norm.reshape(1, D_MODEL),
        "w_in_ab": w_ab, "w_out_ab": w_out_ab[0].astype(BF16),
        "w_in_c": w_in_c[0].astype(BF16), "w_out_c": w_out_c[0].astype(BF16),
        "gain": gain, "ones": jnp.asarray(ones, BF16),
        "tabs": _rope_tables(_axial_angles(max_seq)) + _rope_tables(_linear_angles(max_seq)),
        "log_gamma": -jnp.exp(ret_decay[0].astype(F32)),
        "bias": bias, "sink": sink,
    }
    return _trunk(x_prompt, prm), _trunk(x_sample, prm)
```

```python
import functools

import jax
import jax.numpy as jnp
import numpy as np
from jax import lax
from jax.experimental import pallas as pl
from jax.experimental.pallas import tpu as pltpu

F32 = jnp.float32
BF16 = jnp.bfloat16

D_MODEL = 1024
GRID_W = 64
EPS = 1e-6
ROPE_THETA = 10000.0
HEAD_DIM = 64
HALF = HEAD_DIM // 2
LANES = 128

A_HEADS, A_KV = 8, 2
A_GROUP = A_HEADS // A_KV
A_W, A_KVW = A_HEADS * HEAD_DIM, A_KV * HEAD_DIM
B_HEADS, B_VAL = 4, 128
B_QKW, B_VW = B_HEADS * HEAD_DIM, B_HEADS * B_VAL
C_HEADS, C_KV = 16, 2
C_GROUP = C_HEADS // C_KV
C_W, C_KVW = C_HEADS * HEAD_DIM, C_KV * HEAD_DIM
BLOCK = 128
WINDOW = 128
REL_BUCKETS, REL_MAX_DIST = 32, 128
CHUNK = 128
RET_UNROLL = 32
NEG = -1e30
LOG2E = float(np.log2(np.e))
ONES_ROWS = 16

AB_QA, AB_KA, AB_VA, AB_GA = 0, A_W, A_W + A_KVW, A_W + 2 * A_KVW
AB_QB = AB_GA + A_W
AB_KB, AB_VB, AB_GB = AB_QB + B_QKW, AB_QB + 2 * B_QKW, AB_QB + 2 * B_QKW + B_VW
AB_IN = AB_GB + B_VW
C_Q, C_K, C_V, C_G = 0, C_W, C_W + C_KVW, C_W + 2 * C_KVW
C_IN = C_G + C_W

PROJ_SPLIT = 2
ROW_TILE = 1024
ATT_TQ = 256
ATT_TK_MAX = 1024
ATT_SEQ_CHUNKS = 4
ATT_BODY_KEYS = 16384
WIN_SUPER = 2048
WIN_UNROLL = 16
VMEM_LIMIT = 48 * 1024 * 1024


def _att_tk(seq):
    return min(ATT_TK_MAX, seq // ATT_SEQ_CHUNKS)


def _params(sem):
    return pltpu.CompilerParams(dimension_semantics=sem, vmem_limit_bytes=VMEM_LIMIT)


def _rope_tables(ang):
    c, s = np.repeat(np.cos(ang), 2, axis=-1), np.repeat(np.sin(ang), 2, axis=-1)
    s[:, 0::2] *= -1.0
    return jnp.asarray(np.tile(c, (1, 2)), F32), jnp.asarray(np.tile(s, (1, 2)), F32)


def _axial_angles(n):
    t = np.arange(n)
    quarter = HEAD_DIM // 4
    freqs = ROPE_THETA ** (-np.arange(quarter, dtype=np.float64) / quarter)
    row = (t // GRID_W).astype(np.float64)
    col = (t % GRID_W).astype(np.float64)
    return np.concatenate([row[:, None] * freqs, col[:, None] * freqs], axis=-1)


def _linear_angles(n):
    freqs = ROPE_THETA ** (-np.arange(HALF, dtype=np.float64) / HALF)
    return np.arange(n, dtype=np.float64)[:, None] * freqs


def _t5_bucket(rel):
    half = REL_BUCKETS // 2
    max_exact = half // 2
    ret = (rel > 0).astype(np.int32) * half
    dist = np.abs(rel)
    large = max_exact + (np.log(np.maximum(dist, 1) / max_exact) / np.log(REL_MAX_DIST / max_exact)
                         * (half - max_exact)).astype(np.int32)
    large = np.minimum(large, half - 1)
    return ret + np.where(dist < max_exact, dist, large)


def _rmsnorm_rows(x, gain):
    return x * lax.rsqrt(jnp.mean(x * x, axis=-1, keepdims=True) + EPS) * gain


def _silu(z):
    return z * jax.nn.sigmoid(z)


def _rotate_pairs(x, cos_t, sin_t):
    lane = lax.broadcasted_iota(jnp.int32, (1, LANES), 1)
    partner = jnp.where(lane % 2 == 0, pltpu.roll(x, LANES - 1, 1), pltpu.roll(x, 1, 1))
    return x * cos_t + partner * sin_t


def _transposed_chunks_store(ref, tile, row0=0):
    chunk = ref.shape[2]
    rows = tile.shape[0]
    tile_t = tile.T
    step = min(chunk, rows)
    for s in range(0, rows, step):
        c, off = divmod(row0 + s, chunk)
        ref[c, :, off:off + step] = tile_t[:, s:s + step].astype(ref.dtype)


def _stack_heads_on_rows(q, heads):
    return jnp.concatenate([q[:, h * HEAD_DIM:(h + 1) * HEAD_DIM] for h in range(heads)], axis=0)


def _unstack_transposed(o_t, heads, tokens):
    by_head = jnp.concatenate([o_t[:, h * tokens:(h + 1) * tokens] for h in range(heads)], axis=0)
    return by_head.T


def _with_ones_rows(v_t):
    return jnp.concatenate([v_t, jnp.ones((ONES_ROWS, v_t.shape[1]), v_t.dtype)], axis=0)


def _inproj_ab_kernel(x_ref, g_ref, w_ref, gain_ref, ones_ref, cosa_ref, sina_ref, cosb_ref, sinb_ref,
                      qa_ref, ka_ref, va_ref, sga_ref, qb_ref, kb_ref, vb_ref, sgb_ref):
    part_rows = x_ref.shape[0] // PROJ_SPLIT
    for part in range(PROJ_SPLIT):
        rows = slice(part * part_rows, (part + 1) * part_rows)
        xn = _rmsnorm_rows(x_ref[rows, :], g_ref[...]).astype(BF16)

        def proj(lo, hi):
            return jnp.dot(xn, w_ref[:, lo:hi], preferred_element_type=F32)

        cosa, sina = cosa_ref[rows, :], sina_ref[rows, :]
        cosb, sinb = cosb_ref[rows, :], sinb_ref[rows, :]

        def head_norm_rope(z, gain):
            width = z.shape[1]
            ss = jnp.dot((z * z).astype(BF16), ones_ref[0:width, 0:width], preferred_element_type=F32)
            z = z * lax.rsqrt(ss * (1.0 / HEAD_DIM) + EPS)
            return [_rotate_pairs(z[:, c * LANES:(c + 1) * LANES] * gain, cosa, sina)
                    for c in range(width // LANES)]

        zq = proj(AB_QA, AB_KA)
        for pair in range(A_W // (2 * LANES)):
            tiles = head_norm_rope(zq[:, pair * 2 * LANES:(pair + 1) * 2 * LANES], gain_ref[0:1, :])
            for c, tile in enumerate(tiles):
                lo = (2 * pair + c) * LANES
                qa_ref[rows, lo:lo + LANES] = tile.astype(BF16)
        zkv = proj(AB_KA, AB_GA)
        k_tile, = head_norm_rope(zkv[:, :LANES], gain_ref[1:2, :])
        ka_ref[0, rows, :] = k_tile[:, :HEAD_DIM].astype(BF16)
        ka_ref[1, rows, :] = k_tile[:, HEAD_DIM:].astype(BF16)
        _transposed_chunks_store(va_ref, zkv[:, LANES:], part * part_rows)
        sga_ref[rows, :] = _silu(proj(AB_GA, AB_QB)).astype(BF16)
        sgb_ref[rows, :] = _silu(proj(AB_GB, AB_IN)).astype(BF16)

        zqk = proj(AB_QB, AB_VB)
        for c in range(B_QKW // LANES):
            qb_ref[rows, c * LANES:(c + 1) * LANES] = _rotate_pairs(
                zqk[:, c * LANES:(c + 1) * LANES], cosb, sinb).astype(BF16)
            kt = zqk[:, B_QKW + c * LANES:B_QKW + (c + 1) * LANES] * (HEAD_DIM ** -0.5)
            kb_ref[rows, c * LANES:(c + 1) * LANES] = _rotate_pairs(kt, cosb, sinb).astype(BF16)
        vb_ref[rows, :] = proj(AB_VB, AB_GB).astype(BF16)


def _inproj_ab(x2d, seq, g, w, gain, ones, tabs):
    rows = x2d.shape[0]
    tm = ROW_TILE
    per_seq = seq // tm
    row = lambda i: (i, 0)
    const = lambda i: (0, 0)
    tab = lambda i: (i % per_seq, 0)
    kvrow = lambda i: (0, i, 0)
    tk = _att_tk(seq)
    vt_chunks = tm // tk
    out_shape = (
        jax.ShapeDtypeStruct((rows, A_W), BF16), jax.ShapeDtypeStruct((A_KV, rows, HEAD_DIM), BF16),
        jax.ShapeDtypeStruct((rows // tk, A_KVW, tk), BF16), jax.ShapeDtypeStruct((rows, A_W), BF16),
        jax.ShapeDtypeStruct((rows, B_QKW), BF16), jax.ShapeDtypeStruct((rows, B_QKW), BF16),
        jax.ShapeDtypeStruct((rows, B_VW), BF16), jax.ShapeDtypeStruct((rows, B_VW), BF16),
    )
    out_specs = (
        pl.BlockSpec((tm, A_W), row), pl.BlockSpec((A_KV, tm, HEAD_DIM), kvrow),
        pl.BlockSpec((vt_chunks, A_KVW, tk), lambda i: (i, 0, 0)), pl.BlockSpec((tm, A_W), row),
        pl.BlockSpec((tm, B_QKW), row), pl.BlockSpec((tm, B_QKW), row),
        pl.BlockSpec((tm, B_VW), row), pl.BlockSpec((tm, B_VW), row),
    )
    in_specs = [
        pl.BlockSpec((tm, D_MODEL), row), pl.BlockSpec((1, D_MODEL), const),
        pl.BlockSpec((D_MODEL, AB_IN), const), pl.BlockSpec((2, LANES), const),
        pl.BlockSpec((2 * LANES, 2 * LANES), const),
    ] + [pl.BlockSpec((tm, LANES), tab)] * 4
    return pl.pallas_call(
        _inproj_ab_kernel, grid=(rows // tm,), in_specs=in_specs, out_specs=out_specs,
        out_shape=out_shape, compiler_params=_params(("parallel",)), name="inproj_ab",
    )(x2d, g, w, gain, ones, *tabs)


def _dense_attn_kernel(q_ref, k_ref, vt_ref, sg_ref, o_ref, s_buf, p_buf, *, seq):
    tq, tk = ATT_TQ, _att_tk(seq)
    rows = A_GROUP * tq
    n = seq // tk
    blocks = seq // tq
    assert n % 2 == 0 and n >= 2

    def query_rows(i):
        return pl.ds(pl.multiple_of(i * tq, tq), tq)

    def stacked_q(i):
        return _stack_heads_on_rows(q_ref[query_rows(i), :], A_GROUP)

    def scores(qs, c):
        k = k_ref[c * tk:(c + 1) * tk, :]
        s_t = lax.dot_general(k, qs, (((1,), (1,)), ((), ())), preferred_element_type=F32)
        s_buf[c % 2] = s_t
        return jnp.max(s_t, axis=0, keepdims=True)

    def softmax(m, c, chunk_max):
        m_new = jnp.maximum(m, chunk_max)
        p_buf[c % 2] = jnp.exp2(s_buf[c % 2] - m_new).astype(BF16)
        return m_new, jnp.exp2(m - m_new)

    fresh_m = jnp.full((1, rows), NEG, F32)
    first_q = stacked_q(0)
    first_max = [scores(first_q, 0), scores(first_q, 1)]

    def block(i, m, alpha, max0, max1):
        chunk_max = [max0, max1]
        qs = stacked_q(i)
        qs_next = stacked_q(jnp.minimum(i + 1, blocks - 1))
        acc = jnp.zeros((HEAD_DIM + ONES_ROWS, rows), F32)
        for t in range(n):
            if t + 2 < n:
                scored = scores(qs, t + 2)
            else:
                scored = scores(qs_next, t + 2 - n)
            if t + 1 < n:
                m, alpha_next = softmax(m, t + 1, chunk_max[(t + 1) % 2])
            else:
                m, alpha_next = softmax(fresh_m, 0, chunk_max[0])
            chunk_max[t % 2] = scored
            pv = jnp.dot(_with_ones_rows(vt_ref[t]), p_buf[t % 2], preferred_element_type=F32)
            acc = alpha * acc + pv
            alpha = alpha_next
        o_t = acc[:HEAD_DIM] / acc[HEAD_DIM:HEAD_DIM + 1]
        o = _unstack_transposed(o_t, A_GROUP, tq)
        o_ref[query_rows(i), :] = (o * sg_ref[query_rows(i), :].astype(F32)).astype(BF16)
        return m, alpha, chunk_max[0], chunk_max[1]

    per_body = min(blocks, max(1, ATT_BODY_KEYS // seq))

    def body(step, carry):
        for u in range(per_body):
            carry = block(step * per_body + u, *carry)
        return carry

    lax.fori_loop(0, blocks // per_body, body, softmax(fresh_m, 0, first_max[0]) + tuple(first_max))


def _dense_attn(qa, ka, vat, sga, batch, seq):
    gw = A_GROUP * HEAD_DIM
    q3 = qa.reshape(batch, seq, A_W)
    sg3 = sga.reshape(batch, seq, A_W)
    k4 = ka.reshape(A_KV, batch, seq, HEAD_DIM)
    tk = _att_tk(seq)
    vt4 = vat.reshape(batch, seq // tk, A_KVW, tk)
    qspec = pl.BlockSpec((None, seq, gw), lambda b, g: (b, 0, g))
    kspec = pl.BlockSpec((None, None, seq, HEAD_DIM), lambda b, g: (g, b, 0, 0))
    vtspec = pl.BlockSpec((None, seq // tk, HEAD_DIM, tk), lambda b, g: (b, 0, g, 0))
    out = pl.pallas_call(
        functools.partial(_dense_attn_kernel, seq=seq), grid=(batch, A_KV),
        in_specs=[qspec, kspec, vtspec, qspec], out_specs=qspec,
        out_shape=jax.ShapeDtypeStruct((batch, seq, A_W), BF16),
        scratch_shapes=[pltpu.VMEM((2, tk, A_GROUP * ATT_TQ), F32),
                        pltpu.VMEM((2, tk, A_GROUP * ATT_TQ), BF16)],
        compiler_params=_params(("parallel", "parallel")), name="dense_attn",
    )(q3, k4, vt4, sg3)
    return out.reshape(batch * seq, A_W)


def _retention_kernel(lg_ref, q_ref, k_ref, v_ref, sg_ref, o_ref, st_ref, *, seq):
    c = CHUNK
    nc = seq // c
    pv = 2 * B_VAL
    pair = pl.program_id(1)
    lgf = [lg_ref[0, 2 * pair + hh] for hh in range(2)]
    lgb = [lg_ref[1, 2 * pair + hh] for hh in range(2)]
    lane = lax.broadcasted_iota(jnp.int32, (1, LANES), 1)
    first_qk_head = lane < HEAD_DIM
    lane_lgf = jnp.where(first_qk_head, lgf[0], lgf[1])
    lane_lgb = jnp.where(first_qk_head, lgb[0], lgb[1])
    t = lax.broadcasted_iota(jnp.int32, (c, 1), 0).astype(F32)
    kdec_f = jnp.exp((c - 1.0 - t) * lane_lgf)
    kdec_b = jnp.exp(t * lane_lgb)
    qdec_f = jnp.exp((t + 1.0) * lane_lgf)
    qdec_b = jnp.exp((c - t) * lane_lgb)
    row_first = lax.broadcasted_iota(jnp.int32, (LANES, 1), 0) < HEAD_DIM
    first_v_head = lax.broadcasted_iota(jnp.int32, (1, pv), 1) < B_VAL
    same_head = row_first == first_v_head
    cdec_f = jnp.exp(c * jnp.where(row_first, lgf[0], lgf[1]))
    cdec_b = jnp.exp(c * jnp.where(row_first, lgb[0], lgb[1]))
    diff = (lax.broadcasted_iota(jnp.int32, (c, c), 0) - lax.broadcasted_iota(jnp.int32, (c, c), 1)).astype(F32)
    intra_decay = jnp.concatenate(
        [jnp.where(diff >= 0, jnp.exp(lgf[hh] * jnp.maximum(diff, 0.0)), jnp.exp(lgb[hh] * jnp.maximum(-diff, 0.0)))
         for hh in range(2)], axis=1)

    def chunk_rows(n):
        return pl.ds(pl.multiple_of(n * c, c), c)

    def kv_body(n, carry):
        rows = chunk_rows(n)
        k = k_ref[rows, :].astype(F32)
        v = v_ref[rows, :]
        dn = (((0,), (0,)), ((), ()))
        kvf = lax.dot_general((k * kdec_f).astype(BF16), v, dn, preferred_element_type=F32)
        kvb = lax.dot_general((k * kdec_b).astype(BF16), v, dn, preferred_element_type=F32)
        st_ref[n, 0:LANES, :] = jnp.where(same_head, kvf, 0.0)
        st_ref[n, LANES:2 * LANES, :] = jnp.where(same_head, kvb, 0.0)
        return carry

    lax.fori_loop(0, nc, kv_body, 0, unroll=RET_UNROLL)

    def scan_fwd(n, state):
        kv = st_ref[n, 0:LANES, :]
        st_ref[n, 0:LANES, :] = state
        return state * cdec_f + kv

    lax.fori_loop(0, nc, scan_fwd, jnp.zeros((LANES, pv), F32))

    def scan_bwd(i, state):
        n = nc - 1 - i
        kv = st_ref[n, LANES:2 * LANES, :]
        st_ref[n, LANES:2 * LANES, :] = state
        return state * cdec_b + kv

    lax.fori_loop(0, nc, scan_bwd, jnp.zeros((LANES, pv), F32))

    def out_body(n, carry):
        rows = chunk_rows(n)
        q = q_ref[rows, :].astype(F32)
        k = k_ref[rows, :]
        v = v_ref[rows, :]
        zero = jnp.zeros_like(k)
        k_by_head = jnp.concatenate([jnp.where(first_qk_head, k, zero), jnp.where(first_qk_head, zero, k)], axis=0)
        s = lax.dot_general(q.astype(BF16), k_by_head, (((1,), (1,)), ((), ())),
                            preferred_element_type=F32)
        zero_v = jnp.zeros_like(v)
        v_by_head = jnp.concatenate([jnp.where(first_v_head, v, zero_v), jnp.where(first_v_head, zero_v, v)], axis=0)
        lhs = jnp.concatenate([s * intra_decay, q * qdec_f, q * qdec_b], axis=1).astype(BF16)
        rhs = jnp.concatenate([v_by_head, st_ref[n].astype(BF16)], axis=0)
        o_pair = jnp.dot(lhs, rhs, preferred_element_type=F32)
        for hh in range(2):
            o = o_pair[:, hh * B_VAL:(hh + 1) * B_VAL]
            o = o * lax.rsqrt(jnp.mean(o * o, axis=-1, keepdims=True) + EPS)
            o = o * sg_ref[rows, hh * B_VAL:(hh + 1) * B_VAL].astype(F32)
            o_ref[rows, hh * B_VAL:(hh + 1) * B_VAL] = o.astype(BF16)
        return carry

    lax.fori_loop(0, nc, out_body, 0, unroll=RET_UNROLL)


def _retention(log_gamma, qb, kb, vb, sgb, batch, seq):
    pairs = B_HEADS // 2
    qk_spec = pl.BlockSpec((None, seq, LANES), lambda b, j: (b, 0, j))
    v_spec = pl.BlockSpec((None, seq, 2 * B_VAL), lambda b, j: (b, 0, j))
    out = pl.pallas_call(
        functools.partial(_retention_kernel, seq=seq), grid=(batch, pairs),
        in_specs=[pl.BlockSpec(memory_space=pltpu.SMEM), qk_spec, qk_spec, v_spec, v_spec],
        out_specs=v_spec, out_shape=jax.ShapeDtypeStruct((batch, seq, B_VW), BF16),
        scratch_shapes=[pltpu.VMEM((seq // CHUNK, 2 * LANES, 2 * B_VAL), F32)],
        compiler_params=_params(("parallel", "parallel")), name="retention",
    )(log_gamma, qb.reshape(batch, seq, B_QKW), kb.reshape(batch, seq, B_QKW),
      vb.reshape(batch, seq, B_VW), sgb.reshape(batch, seq, B_VW))
    return out.reshape(batch * seq, B_VW)


def _mid_proj_kernel(x_ref, ma_ref, mb_ref, wo_ref, g_ref, wc_ref, x1_ref, q_ref, k_ref, v_ref, sg_ref):
    part_rows = x_ref.shape[0] // PROJ_SPLIT
    for part in range(PROJ_SPLIT):
        rows = slice(part * part_rows, (part + 1) * part_rows)
        y = jnp.dot(ma_ref[rows, :], wo_ref[0:A_W, :], preferred_element_type=F32)
        y += jnp.dot(mb_ref[rows, :], wo_ref[A_W:A_W + B_VW, :], preferred_element_type=F32)
        x1 = x_ref[rows, :] + y
        x1_ref[rows, :] = x1
        xn = _rmsnorm_rows(x1, g_ref[...]).astype(BF16)

        def proj(lo, hi):
            return jnp.dot(xn, wc_ref[:, lo:hi], preferred_element_type=F32)

        sg_ref[rows, :] = _silu(proj(C_G, C_IN)).astype(BF16)
        zkv = proj(C_K, C_G)
        k_ref[0, rows, :] = zkv[:, :HEAD_DIM].astype(BF16)
        k_ref[1, rows, :] = zkv[:, HEAD_DIM:LANES].astype(BF16)
        _transposed_chunks_store(v_ref, zkv[:, LANES:], part * part_rows)
        q_ref[rows, :] = (proj(C_Q, C_K) * (HEAD_DIM ** -0.5 * LOG2E)).astype(BF16)


def _mid_proj(x2d, ma, mb, wo, g, wc):
    rows = x2d.shape[0]
    tm = ROW_TILE
    row = lambda i: (i, 0)
    const = lambda i: (0, 0)
    kvrow = lambda i: (0, i, 0)
    out_shape = (
        jax.ShapeDtypeStruct((rows, D_MODEL), F32), jax.ShapeDtypeStruct((rows, C_W), BF16),
        jax.ShapeDtypeStruct((C_KV, rows, HEAD_DIM), BF16), jax.ShapeDtypeStruct((rows // BLOCK, C_KVW, BLOCK), BF16),
        jax.ShapeDtypeStruct((rows, C_W), BF16),
    )
    out_specs = (
        pl.BlockSpec((tm, D_MODEL), row), pl.BlockSpec((tm, C_W), row),
        pl.BlockSpec((C_KV, tm, HEAD_DIM), kvrow), pl.BlockSpec((tm // BLOCK, C_KVW, BLOCK), lambda i: (i, 0, 0)),
        pl.BlockSpec((tm, C_W), row),
    )
    in_specs = [
        pl.BlockSpec((tm, D_MODEL), row), pl.BlockSpec((tm, A_W), row), pl.BlockSpec((tm, B_VW), row),
        pl.BlockSpec((A_W + B_VW, D_MODEL), const), pl.BlockSpec((1, D_MODEL), const),
        pl.BlockSpec((D_MODEL, C_IN), const),
    ]
    return pl.pallas_call(
        _mid_proj_kernel, grid=(rows // tm,), in_specs=in_specs, out_specs=out_specs,
        out_shape=out_shape, compiler_params=_params(("parallel",)), name="mid_proj",
    )(x2d, ma, mb, wo, g, wc)


def _win_attn_kernel(q_ref, kp_ref, k_ref, kn_ref, vp_ref, v_ref, vn_ref, bias_ref, sink_ref, sg_ref, o_ref,
                     kw_all, vt_all, s_buf, p_buf, *, blocks_per_seq):
    j = pl.program_id(1)
    nblk = q_ref.shape[0] // BLOCK
    gw = C_GROUP * HEAD_DIM
    for g in range(C_KV):
        kw_all[g, 0:BLOCK] = kp_ref[g]
        kw_all[g, BLOCK:(nblk + 1) * BLOCK] = k_ref[g]
        kw_all[g, (nblk + 1) * BLOCK:(nblk + 2) * BLOCK] = kn_ref[g]
    vt_all[0] = vp_ref[0]
    vt_all[1:nblk + 1] = v_ref[...]
    vt_all[nblk + 1] = vn_ref[0]

    def block_rows(i):
        return pl.ds(pl.multiple_of(i * BLOCK, BLOCK), BLOCK)

    def scores(i, g):
        qs = _stack_heads_on_rows(q_ref[block_rows(i), g * gw:(g + 1) * gw], C_GROUP)
        kw = kw_all[g, pl.ds(pl.multiple_of(i * BLOCK, BLOCK), 3 * BLOCK), :]
        s_buf[g] = lax.dot_general(kw, qs, (((1,), (1,)), ((), ())), preferred_element_type=F32)

    def softmax(i, g):
        block_in_seq = j * nblk + i
        prev_bias = jnp.where(block_in_seq == 0, 3, 0)
        next_bias = jnp.where(block_in_seq == blocks_per_seq - 1, 3, 2)
        s_t = jnp.concatenate([s_buf[g, 0:BLOCK] + bias_ref[prev_bias, g],
                               s_buf[g, BLOCK:2 * BLOCK] + bias_ref[1, g],
                               s_buf[g, 2 * BLOCK:3 * BLOCK] + bias_ref[next_bias, g]], axis=0)
        sink = sink_ref[g]
        mx = jnp.maximum(jnp.max(s_t, axis=0, keepdims=True), sink)
        p_buf[g] = jnp.exp2(s_t - mx).astype(BF16)
        return jnp.exp2(sink - mx)

    def finish(i, g, sink_term):
        heads = slice(g * HEAD_DIM, (g + 1) * HEAD_DIM)
        vw_t = jnp.concatenate([vt_all[i, heads, :], vt_all[i + 1, heads, :], vt_all[i + 2, heads, :]], axis=1)
        acc = jnp.dot(_with_ones_rows(vw_t), p_buf[g], preferred_element_type=F32)
        denom = acc[HEAD_DIM:HEAD_DIM + 1] + sink_term
        o = _unstack_transposed(acc[:HEAD_DIM] / denom, C_GROUP, BLOCK)
        gate = sg_ref[block_rows(i), g * gw:(g + 1) * gw].astype(F32)
        o_ref[block_rows(i), g * gw:(g + 1) * gw] = (o * gate).astype(BF16)

    scores(0, 0)
    scores(0, 1)

    def body(step, sink_term0):
        for u in range(WIN_UNROLL):
            i = step * WIN_UNROLL + u
            nxt = jnp.minimum(i + 1, nblk - 1)
            scores(nxt, 0)
            sink_term1 = softmax(i, 1)
            finish(i, 0, sink_term0)
            scores(nxt, 1)
            sink_term0 = softmax(nxt, 0)
            finish(i, 1, sink_term1)
        return sink_term0

    lax.fori_loop(0, nblk // WIN_UNROLL, body, softmax(0, 0))


def _win_attn(q, k, vt, bias, sink, sg, batch, seq):
    nb = seq // BLOCK
    sup = WIN_SUPER
    nblk = sup // BLOCK
    q3 = q.reshape(batch, seq, C_W)
    sg3 = sg.reshape(batch, seq, C_W)
    k4 = k.reshape(C_KV, batch, seq, HEAD_DIM)
    rowspec = pl.BlockSpec((None, sup, C_W), lambda b, j: (b, j, 0))
    before = lambda j: jnp.maximum(j * nblk - 1, 0)
    after = lambda j: jnp.minimum((j + 1) * nblk, nb - 1)
    halo_k = (C_KV, None, BLOCK, HEAD_DIM)
    halo_v = (1, C_KVW, BLOCK)
    out = pl.pallas_call(
        functools.partial(_win_attn_kernel, blocks_per_seq=nb), grid=(batch, seq // sup),
        in_specs=[rowspec,
                  pl.BlockSpec(halo_k, lambda b, j: (0, b, before(j), 0)),
                  pl.BlockSpec((C_KV, None, sup, HEAD_DIM), lambda b, j: (0, b, j, 0)),
                  pl.BlockSpec(halo_k, lambda b, j: (0, b, after(j), 0)),
                  pl.BlockSpec(halo_v, lambda b, j: (b * nb + before(j), 0, 0)),
                  pl.BlockSpec((nblk, C_KVW, BLOCK), lambda b, j: (b * (nb // nblk) + j, 0, 0)),
                  pl.BlockSpec(halo_v, lambda b, j: (b * nb + after(j), 0, 0)),
                  pl.BlockSpec((4, C_KV, BLOCK, C_GROUP * BLOCK), lambda b, j: (0, 0, 0, 0)),
                  pl.BlockSpec((C_KV, 1, C_GROUP * BLOCK), lambda b, j: (0, 0, 0)), rowspec],
        out_specs=rowspec, out_shape=jax.ShapeDtypeStruct((batch, seq, C_W), BF16),
        scratch_shapes=[pltpu.VMEM((C_KV, sup + 2 * BLOCK, HEAD_DIM), BF16),
                        pltpu.VMEM((nblk + 2, C_KVW, BLOCK), BF16),
                        pltpu.VMEM((C_KV, 3 * BLOCK, C_GROUP * BLOCK), F32),
                        pltpu.VMEM((C_KV, 3 * BLOCK, C_GROUP * BLOCK), BF16)],
        compiler_params=_params(("parallel", "parallel")), name="win_attn",
    )(q3, k4, k4, k4, vt, vt, vt, bias, sink, sg3)
    return out.reshape(batch * seq, C_W)


def _out_proj_kernel(x_ref, m_ref, wo_ref, g_ref, y_ref):
    part_rows = x_ref.shape[0] // PROJ_SPLIT
    for part in range(PROJ_SPLIT):
        rows = slice(part * part_rows, (part + 1) * part_rows)
        x2 = x_ref[rows, :] + jnp.dot(m_ref[rows, :], wo_ref[...], preferred_element_type=F32)
        y_ref[rows, :] = _rmsnorm_rows(x2, g_ref[...])


def _out_proj(x2d, m, wo, g):
    rows = x2d.shape[0]
    tm = ROW_TILE
    row = lambda i: (i, 0)
    const = lambda i: (0, 0)
    return pl.pallas_call(
        _out_proj_kernel, grid=(rows // tm,),
        in_specs=[pl.BlockSpec((tm, D_MODEL), row), pl.BlockSpec((tm, C_W), row),
                  pl.BlockSpec((C_W, D_MODEL), const), pl.BlockSpec((1, D_MODEL), const)],
        out_specs=pl.BlockSpec((tm, D_MODEL), row), out_shape=jax.ShapeDtypeStruct((rows, D_MODEL), F32),
        compiler_params=_params(("parallel",)), name="out_proj",
    )(x2d, m, wo, g)


def _trunk(x, prm):
    batch, seq, _ = x.shape
    x2d = x.reshape(batch * seq, D_MODEL)
    qa, ka, va, sga, qb, kb, vb, sgb = _inproj_ab(
        x2d, seq, prm["g0"], prm["w_in_ab"], prm["gain"], prm["ones"], prm["tabs"])
    ma = _dense_attn(qa, ka, va, sga, batch, seq)
    mb = _retention(prm["log_gamma"], qb, kb, vb, sgb, batch, seq)
    x1, qc, kc, vc, sgc = _mid_proj(x2d, ma, mb, prm["w_out_ab"], prm["g1"], prm["w_in_c"])
    mc = _win_attn(qc, kc, vc, prm["bias"], prm["sink"], sgc, batch, seq)
    y = _out_proj(x1, mc, prm["w_out_c"], prm["gf"])
    return y.reshape(batch, seq, D_MODEL)


def kernel(x_prompt, x_sample, norm_g, w_in_ab, qk_norm_a, ret_decay, w_out_ab, w_in_c, sink_c, w_out_c,
           rel_bias, final_norm):
    assert w_in_ab.shape[0] == 1 and w_in_c.shape[0] == 1, "two-layer trunk: one layer of each kind"
    max_seq = max(x_prompt.shape[1], x_sample.shape[1])

    w_ab = w_in_ab[0].astype(BF16)
    gain = jnp.stack([jnp.tile(qk_norm_a[0, 0], 2) * (HEAD_DIM ** -0.5 * LOG2E),
                      jnp.tile(qk_norm_a[0, 1], 2)]).astype(F32)
    ones = np.kron(np.eye(2 * LANES // HEAD_DIM), np.ones((HEAD_DIM, HEAD_DIM)))

    rel = np.arange(4 * BLOCK)[:, None] - BLOCK - np.arange(BLOCK)[None, :]
    bucket_onehot = jax.nn.one_hot(jnp.asarray(_t5_bucket(rel).reshape(-1)), REL_BUCKETS, dtype=F32, axis=0)
    bias = jnp.einsum("bh,bn->hn", rel_bias.astype(F32) * LOG2E, bucket_onehot,
                      precision=lax.Precision.HIGHEST)
    bias = jnp.where(jnp.asarray((np.abs(rel) <= WINDOW).reshape(1, -1)), bias, NEG)
    bias = bias.reshape(C_KV, C_GROUP, 4, BLOCK, BLOCK).transpose(2, 0, 3, 1, 4)
    bias = bias.reshape(4, C_KV, BLOCK, C_GROUP * BLOCK)
    sink = jnp.repeat(sink_c[0].astype(F32) * LOG2E, BLOCK).reshape(C_KV, 1, C_GROUP * BLOCK)

    prm = {
        "g0": norm_g[0].reshape(1, D_MODEL), "g1": norm_g[1].reshape(1, D_MODEL),
        "gf": final_norm.reshape(1, D_MODEL),
        "w_in_ab": w_ab, "w_out_ab": w_out_ab[0].astype(BF16),
        "w_in_c": w_in_c[0].astype(BF16), "w_out_c": w_out_c[0].astype(BF16),
        "gain": gain, "ones": jnp.asarray(ones, BF16),
        "tabs": _rope_tables(_axial_angles(max_seq)) + _rope_tables(_linear_angles(max_seq)),
        "log_gamma": -jnp.exp(ret_decay[0].astype(F32)),
        "bias": bias, "sink": sink,
    }
    return _trunk(x_prompt, prm), _trunk(x_sample, prm)
```

```python
import functools

import jax
import jax.numpy as jnp
import numpy as np
from jax import lax
from jax.experimental import pallas as pl
from jax.experimental.pallas import tpu as pltpu

F32 = jnp.float32
BF16 = jnp.bfloat16

D_MODEL = 1024
GRID_W = 64
EPS = 1e-6
ROPE_THETA = 10000.0
HEAD_DIM = 64
HALF = HEAD_DIM // 2
LANES = 128

A_HEADS, A_KV = 8, 2
A_GROUP = A_HEADS // A_KV
A_W, A_KVW = A_HEADS * HEAD_DIM, A_KV * HEAD_DIM
B_HEADS, B_VAL = 4, 128
B_QKW, B_VW = B_HEADS * HEAD_DIM, B_HEADS * B_VAL
C_HEADS, C_KV = 16, 2
C_GROUP = C_HEADS // C_KV
C_W, C_KVW = C_HEADS * HEAD_DIM, C_KV * HEAD_DIM
BLOCK = 128
WINDOW = 128
REL_BUCKETS, REL_MAX_DIST = 32, 128
CHUNK = 128
RET_UNROLL = 32
NEG = -1e30
LOG2E = float(np.log2(np.e))
ONES_ROWS = 16

AB_QA, AB_KA, AB_VA, AB_GA = 0, A_W, A_W + A_KVW, A_W + 2 * A_KVW
AB_QB = AB_GA + A_W
AB_KB, AB_VB, AB_GB = AB_QB + B_QKW, AB_QB + 2 * B_QKW, AB_QB + 2 * B_QKW + B_VW
AB_IN = AB_GB + B_VW
C_Q, C_K, C_V, C_G = 0, C_W, C_W + C_KVW, C_W + 2 * C_KVW
C_IN = C_G + C_W

PROJ_SPLIT = 2
ROW_TILE = 1024
OUT_ROW_TILE = 1024
OUT_INPUT_BUFFERS = 3
ATT_TQ = 256
ATT_TK_MAX = 1024
ATT_SEQ_CHUNKS = 4
ATT_BODY_KEYS = 16384
WIN_SUPER = 2048
WIN_UNROLL = 16
VMEM_LIMIT = 56 * 1024 * 1024


def _att_tk(seq):
    return min(ATT_TK_MAX, seq // ATT_SEQ_CHUNKS)


def _params(sem):
    return pltpu.CompilerParams(dimension_semantics=sem, vmem_limit_bytes=VMEM_LIMIT)


def _rope_tables(ang):
    c, s = np.repeat(np.cos(ang), 2, axis=-1), np.repeat(np.sin(ang), 2, axis=-1)
    s[:, 0::2] *= -1.0
    return jnp.asarray(np.tile(c, (1, 2)), F32), jnp.asarray(np.tile(s, (1, 2)), F32)


def _axial_angles(n):
    t = np.arange(n)
    quarter = HEAD_DIM // 4
    freqs = ROPE_THETA ** (-np.arange(quarter, dtype=np.float64) / quarter)
    row = (t // GRID_W).astype(np.float64)
    col = (t % GRID_W).astype(np.float64)
    return np.concatenate([row[:, None] * freqs, col[:, None] * freqs], axis=-1)


def _linear_angles(n):
    freqs = ROPE_THETA ** (-np.arange(HALF, dtype=np.float64) / HALF)
    return np.arange(n, dtype=np.float64)[:, None] * freqs


def _t5_bucket(rel):
    half = REL_BUCKETS // 2
    max_exact = half // 2
    ret = (rel > 0).astype(np.int32) * half
    dist = np.abs(rel)
    large = max_exact + (np.log(np.maximum(dist, 1) / max_exact) / np.log(REL_MAX_DIST / max_exact)
                         * (half - max_exact)).astype(np.int32)
    large = np.minimum(large, half - 1)
    return ret + np.where(dist < max_exact, dist, large)


def _rmsnorm_rows(x, gain):
    return x * lax.rsqrt(jnp.mean(x * x, axis=-1, keepdims=True) + EPS) * gain


def _silu(z):
    return z * jax.nn.sigmoid(z)


def _rotate_pairs(x, cos_t, sin_t):
    lane = lax.broadcasted_iota(jnp.int32, (1, LANES), 1)
    partner = jnp.where(lane % 2 == 0, pltpu.roll(x, LANES - 1, 1), pltpu.roll(x, 1, 1))
    return x * cos_t + partner * sin_t


def _transposed_chunks_store(ref, tile, row0=0):
    chunk = ref.shape[2]
    rows = tile.shape[0]
    tile_t = tile.T
    step = min(chunk, rows)
    for s in range(0, rows, step):
        c, off = divmod(row0 + s, chunk)
        ref[c, :, off:off + step] = tile_t[:, s:s + step].astype(ref.dtype)


def _stack_heads_on_rows(q, heads):
    return jnp.concatenate([q[:, h * HEAD_DIM:(h + 1) * HEAD_DIM] for h in range(heads)], axis=0)


def _unstack_transposed(o_t, heads, tokens):
    by_head = jnp.concatenate([o_t[:, h * tokens:(h + 1) * tokens] for h in range(heads)], axis=0)
    return by_head.T


def _with_ones_rows(v_t):
    return jnp.concatenate([v_t, jnp.ones((ONES_ROWS, v_t.shape[1]), v_t.dtype)], axis=0)


def _inproj_ab_kernel(x_ref, g_ref, w_ref, gain_ref, ones_ref, cosa_ref, sina_ref, cosb_ref, sinb_ref,
                      qa_ref, ka_ref, va_ref, sga_ref, qb_ref, kb_ref, vb_ref, sgb_ref):
    part_rows = x_ref.shape[0] // PROJ_SPLIT
    for part in range(PROJ_SPLIT):
        rows = slice(part * part_rows, (part + 1) * part_rows)
        xn = _rmsnorm_rows(x_ref[rows, :], g_ref[...]).astype(BF16)

        def proj(lo, hi):
            return jnp.dot(xn, w_ref[:, lo:hi], preferred_element_type=F32)

        cosa, sina = cosa_ref[rows, :], sina_ref[rows, :]
        cosb, sinb = cosb_ref[rows, :], sinb_ref[rows, :]

        def head_norm_rope(z, gain):
            width = z.shape[1]
            ss = jnp.dot((z * z).astype(BF16), ones_ref[0:width, 0:width], preferred_element_type=F32)
            z = z * lax.rsqrt(ss * (1.0 / HEAD_DIM) + EPS)
            return [_rotate_pairs(z[:, c * LANES:(c + 1) * LANES] * gain, cosa, sina)
                    for c in range(width // LANES)]

        zq = proj(AB_QA, AB_KA)
        for pair in range(A_W // (2 * LANES)):
            tiles = head_norm_rope(zq[:, pair * 2 * LANES:(pair + 1) * 2 * LANES], gain_ref[0:1, :])
            for c, tile in enumerate(tiles):
                lo = (2 * pair + c) * LANES
                qa_ref[rows, lo:lo + LANES] = tile.astype(BF16)
        zkv = proj(AB_KA, AB_GA)
        k_tile, = head_norm_rope(zkv[:, :LANES], gain_ref[1:2, :])
        ka_ref[0, rows, :] = k_tile[:, :HEAD_DIM].astype(BF16)
        ka_ref[1, rows, :] = k_tile[:, HEAD_DIM:].astype(BF16)
        _transposed_chunks_store(va_ref, zkv[:, LANES:], part * part_rows)
        sga_ref[rows, :] = _silu(proj(AB_GA, AB_QB)).astype(BF16)
        sgb_ref[rows, :] = _silu(proj(AB_GB, AB_IN)).astype(BF16)

        zqk = proj(AB_QB, AB_VB)
        for c in range(B_QKW // LANES):
            qb_ref[rows, c * LANES:(c + 1) * LANES] = _rotate_pairs(
                zqk[:, c * LANES:(c + 1) * LANES], cosb, sinb).astype(BF16)
            kt = zqk[:, B_QKW + c * LANES:B_QKW + (c + 1) * LANES] * (HEAD_DIM ** -0.5)
            kb_ref[rows, c * LANES:(c + 1) * LANES] = _rotate_pairs(kt, cosb, sinb).astype(BF16)
        vb_ref[rows, :] = proj(AB_VB, AB_GB).astype(BF16)


def _inproj_ab(x2d, seq, g, w, gain, ones, tabs):
    rows = x2d.shape[0]
    tm = ROW_TILE
    per_seq = seq // tm
    row = lambda i: (i, 0)
    const = lambda i: (0, 0)
    tab = lambda i: (i % per_seq, 0)
    kvrow = lambda i: (0, i, 0)
    tk = _att_tk(seq)
    vt_chunks = tm // tk
    out_shape = (
        jax.ShapeDtypeStruct((rows, A_W), BF16), jax.ShapeDtypeStruct((A_KV, rows, HEAD_DIM), BF16),
        jax.ShapeDtypeStruct((rows // tk, A_KVW, tk), BF16), jax.ShapeDtypeStruct((rows, A_W), BF16),
        jax.ShapeDtypeStruct((rows, B_QKW), BF16), jax.ShapeDtypeStruct((rows, B_QKW), BF16),
        jax.ShapeDtypeStruct((rows, B_VW), BF16), jax.ShapeDtypeStruct((rows, B_VW), BF16),
    )
    out_specs = (
        pl.BlockSpec((tm, A_W), row), pl.BlockSpec((A_KV, tm, HEAD_DIM), kvrow),
        pl.BlockSpec((vt_chunks, A_KVW, tk), lambda i: (i, 0, 0)), pl.BlockSpec((tm, A_W), row),
        pl.BlockSpec((tm, B_QKW), row), pl.BlockSpec((tm, B_QKW), row),
        pl.BlockSpec((tm, B_VW), row), pl.BlockSpec((tm, B_VW), row),
    )
    in_specs = [
        pl.BlockSpec((tm, D_MODEL), row), pl.BlockSpec((1, D_MODEL), const),
        pl.BlockSpec((D_MODEL, AB_IN), const), pl.BlockSpec((2, LANES), const),
        pl.BlockSpec((2 * LANES, 2 * LANES), const),
    ] + [pl.BlockSpec((tm, LANES), tab)] * 4
    return pl.pallas_call(
        _inproj_ab_kernel, grid=(rows // tm,), in_specs=in_specs, out_specs=out_specs,
        out_shape=out_shape, compiler_params=_params(("parallel",)), name="inproj_ab",
    )(x2d, g, w, gain, ones, *tabs)


def _dense_attn_kernel(q_ref, k_ref, vt_ref, sg_ref, o_ref, s_buf, p_buf, *, seq):
    tq, tk = ATT_TQ, _att_tk(seq)
    rows = A_GROUP * tq
    n = seq // tk
    blocks = seq // tq
    assert n % 2 == 0 and n >= 2

    def query_rows(i):
        return pl.ds(pl.multiple_of(i * tq, tq), tq)

    def stacked_q(i):
        return _stack_heads_on_rows(q_ref[query_rows(i), :], A_GROUP)

    def scores(qs, c):
        k = k_ref[c * tk:(c + 1) * tk, :]
        s_t = lax.dot_general(k, qs, (((1,), (1,)), ((), ())), preferred_element_type=F32)
        s_buf[c % 2] = s_t
        return jnp.max(s_t, axis=0, keepdims=True)

    def softmax(m, c, chunk_max):
        m_new = jnp.maximum(m, chunk_max)
        p_buf[c % 2] = jnp.exp2(s_buf[c % 2] - m_new).astype(BF16)
        return m_new, jnp.exp2(m - m_new)

    fresh_m = jnp.full((1, rows), NEG, F32)
    first_q = stacked_q(0)
    first_max = [scores(first_q, 0), scores(first_q, 1)]

    def block(i, m, alpha, max0, max1):
        chunk_max = [max0, max1]
        qs = stacked_q(i)
        qs_next = stacked_q(jnp.minimum(i + 1, blocks - 1))
        acc = jnp.zeros((HEAD_DIM + ONES_ROWS, rows), F32)
        for t in range(n):
            if t + 2 < n:
                scored = scores(qs, t + 2)
            else:
                scored = scores(qs_next, t + 2 - n)
            if t + 1 < n:
                m, alpha_next = softmax(m, t + 1, chunk_max[(t + 1) % 2])
            else:
                m, alpha_next = softmax(fresh_m, 0, chunk_max[0])
            chunk_max[t % 2] = scored
            pv = jnp.dot(_with_ones_rows(vt_ref[t]), p_buf[t % 2], preferred_element_type=F32)
            acc = alpha * acc + pv
            alpha = alpha_next
        o_t = acc[:HEAD_DIM] / acc[HEAD_DIM:HEAD_DIM + 1]
        o = _unstack_transposed(o_t, A_GROUP, tq)
        o_ref[query_rows(i), :] = (o * sg_ref[query_rows(i), :].astype(F32)).astype(BF16)
        return m, alpha, chunk_max[0], chunk_max[1]

    per_body = min(blocks, max(1, ATT_BODY_KEYS // seq))

    def body(step, carry):
        for u in range(per_body):
            carry = block(step * per_body + u, *carry)
        return carry

    lax.fori_loop(0, blocks // per_body, body, softmax(fresh_m, 0, first_max[0]) + tuple(first_max))


def _dense_attn(qa, ka, vat, sga, batch, seq):
    gw = A_GROUP * HEAD_DIM
    q3 = qa.reshape(batch, seq, A_W)
    sg3 = sga.reshape(batch, seq, A_W)
    k4 = ka.reshape(A_KV, batch, seq, HEAD_DIM)
    tk = _att_tk(seq)
    vt4 = vat.reshape(batch, seq // tk, A_KVW, tk)
    qspec = pl.BlockSpec((None, seq, gw), lambda b, g: (b, 0, g))
    kspec = pl.BlockSpec((None, None, seq, HEAD_DIM), lambda b, g: (g, b, 0, 0))
    vtspec = pl.BlockSpec((None, seq // tk, HEAD_DIM, tk), lambda b, g: (b, 0, g, 0))
    out = pl.pallas_call(
        functools.partial(_dense_attn_kernel, seq=seq), grid=(batch, A_KV),
        in_specs=[qspec, kspec, vtspec, qspec], out_specs=qspec,
        out_shape=jax.ShapeDtypeStruct((batch, seq, A_W), BF16),
        scratch_shapes=[pltpu.VMEM((2, tk, A_GROUP * ATT_TQ), F32),
                        pltpu.VMEM((2, tk, A_GROUP * ATT_TQ), BF16)],
        compiler_params=_params(("parallel", "parallel")), name="dense_attn",
    )(q3, k4, vt4, sg3)
    return out.reshape(batch * seq, A_W)


def _retention_kernel(lg_ref, q_ref, k_ref, v_ref, sg_ref, o_ref, st_ref, *, seq):
    c = CHUNK
    nc = seq // c
    pv = 2 * B_VAL
    pair = pl.program_id(1)
    lgf = [lg_ref[0, 2 * pair + hh] for hh in range(2)]
    lgb = [lg_ref[1, 2 * pair + hh] for hh in range(2)]
    lane = lax.broadcasted_iota(jnp.int32, (1, LANES), 1)
    first_qk_head = lane < HEAD_DIM
    lane_lgf = jnp.where(first_qk_head, lgf[0], lgf[1])
    lane_lgb = jnp.where(first_qk_head, lgb[0], lgb[1])
    t = lax.broadcasted_iota(jnp.int32, (c, 1), 0).astype(F32)
    kdec_f = jnp.exp((c - 1.0 - t) * lane_lgf)
    kdec_b = jnp.exp(t * lane_lgb)
    qdec_f = jnp.exp((t + 1.0) * lane_lgf)
    qdec_b = jnp.exp((c - t) * lane_lgb)
    row_first = lax.broadcasted_iota(jnp.int32, (LANES, 1), 0) < HEAD_DIM
    first_v_head = lax.broadcasted_iota(jnp.int32, (1, pv), 1) < B_VAL
    same_head = row_first == first_v_head
    cdec_f = jnp.exp(c * jnp.where(row_first, lgf[0], lgf[1]))
    cdec_b = jnp.exp(c * jnp.where(row_first, lgb[0], lgb[1]))
    diff = (lax.broadcasted_iota(jnp.int32, (c, c), 0) - lax.broadcasted_iota(jnp.int32, (c, c), 1)).astype(F32)
    intra_decay = jnp.concatenate(
        [jnp.where(diff >= 0, jnp.exp(lgf[hh] * jnp.maximum(diff, 0.0)), jnp.exp(lgb[hh] * jnp.maximum(-diff, 0.0)))
         for hh in range(2)], axis=1)

    def chunk_rows(n):
        return pl.ds(pl.multiple_of(n * c, c), c)

    def kv_body(n, carry):
        rows = chunk_rows(n)
        k = k_ref[rows, :].astype(F32)
        v = v_ref[rows, :]
        dn = (((0,), (0,)), ((), ()))
        kvf = lax.dot_general((k * kdec_f).astype(BF16), v, dn, preferred_element_type=F32)
        kvb = lax.dot_general((k * kdec_b).astype(BF16), v, dn, preferred_element_type=F32)
        st_ref[n, 0:LANES, :] = jnp.where(same_head, kvf, 0.0)
        st_ref[n, LANES:2 * LANES, :] = jnp.where(same_head, kvb, 0.0)
        return carry

    lax.fori_loop(0, nc, kv_body, 0, unroll=RET_UNROLL)

    def scan_fwd(n, state):
        kv = st_ref[n, 0:LANES, :]
        st_ref[n, 0:LANES, :] = state
        return state * cdec_f + kv

    lax.fori_loop(0, nc, scan_fwd, jnp.zeros((LANES, pv), F32))

    def scan_bwd(i, state):
        n = nc - 1 - i
        kv = st_ref[n, LANES:2 * LANES, :]
        st_ref[n, LANES:2 * LANES, :] = state
        return state * cdec_b + kv

    lax.fori_loop(0, nc, scan_bwd, jnp.zeros((LANES, pv), F32))

    def out_body(n, carry):
        rows = chunk_rows(n)
        q = q_ref[rows, :].astype(F32)
        k = k_ref[rows, :]
        v = v_ref[rows, :]
        zero = jnp.zeros_like(k)
        k_by_head = jnp.concatenate([jnp.where(first_qk_head, k, zero), jnp.where(first_qk_head, zero, k)], axis=0)
        s = lax.dot_general(q.astype(BF16), k_by_head, (((1,), (1,)), ((), ())),
                            preferred_element_type=F32)
        zero_v = jnp.zeros_like(v)
        v_by_head = jnp.concatenate([jnp.where(first_v_head, v, zero_v), jnp.where(first_v_head, zero_v, v)], axis=0)
        lhs = jnp.concatenate([s * intra_decay, q * qdec_f, q * qdec_b], axis=1).astype(BF16)
        rhs = jnp.concatenate([v_by_head, st_ref[n].astype(BF16)], axis=0)
        o_pair = jnp.dot(lhs, rhs, preferred_element_type=F32)
        for hh in range(2):
            o = o_pair[:, hh * B_VAL:(hh + 1) * B_VAL]
            o = o * lax.rsqrt(jnp.mean(o * o, axis=-1, keepdims=True) + EPS)
            o = o * sg_ref[rows, hh * B_VAL:(hh + 1) * B_VAL].astype(F32)
            o_ref[rows, hh * B_VAL:(hh + 1) * B_VAL] = o.astype(BF16)
        return carry

    lax.fori_loop(0, nc, out_body, 0, unroll=RET_UNROLL)


def _retention(log_gamma, qb, kb, vb, sgb, batch, seq):
    pairs = B_HEADS // 2
    qk_spec = pl.BlockSpec((None, seq, LANES), lambda b, j: (b, 0, j))
    v_spec = pl.BlockSpec((None, seq, 2 * B_VAL), lambda b, j: (b, 0, j))
    out = pl.pallas_call(
        functools.partial(_retention_kernel, seq=seq), grid=(batch, pairs),
        in_specs=[pl.BlockSpec(memory_space=pltpu.SMEM), qk_spec, qk_spec, v_spec, v_spec],
        out_specs=v_spec, out_shape=jax.ShapeDtypeStruct((batch, seq, B_VW), BF16),
        scratch_shapes=[pltpu.VMEM((seq // CHUNK, 2 * LANES, 2 * B_VAL), F32)],
        compiler_params=_params(("parallel", "parallel")), name="retention",
    )(log_gamma, qb.reshape(batch, seq, B_QKW), kb.reshape(batch, seq, B_QKW),
      vb.reshape(batch, seq, B_VW), sgb.reshape(batch, seq, B_VW))
    return out.reshape(batch * seq, B_VW)


def _mid_proj_kernel(x_ref, ma_ref, mb_ref, wo_ref, g_ref, wc_ref, x1_ref, q_ref, k_ref, v_ref, sg_ref):
    part_rows = x_ref.shape[0] // PROJ_SPLIT
    for part in range(PROJ_SPLIT):
        rows = slice(part * part_rows, (part + 1) * part_rows)
        y = jnp.dot(ma_ref[rows, :], wo_ref[0:A_W, :], preferred_element_type=F32)
        y += jnp.dot(mb_ref[rows, :], wo_ref[A_W:A_W + B_VW, :], preferred_element_type=F32)
        x1 = x_ref[rows, :] + y
        x1_ref[rows, :] = x1
        xn = _rmsnorm_rows(x1, g_ref[...]).astype(BF16)

        def proj(lo, hi):
            return jnp.dot(xn, wc_ref[:, lo:hi], preferred_element_type=F32)

        sg_ref[rows, :] = _silu(proj(C_G, C_IN)).astype(BF16)
        zkv = proj(C_K, C_G)
        k_ref[0, rows, :] = zkv[:, :HEAD_DIM].astype(BF16)
        k_ref[1, rows, :] = zkv[:, HEAD_DIM:LANES].astype(BF16)
        _transposed_chunks_store(v_ref, zkv[:, LANES:], part * part_rows)
        q_ref[rows, :] = (proj(C_Q, C_K) * (HEAD_DIM ** -0.5 * LOG2E)).astype(BF16)


def _mid_proj(x2d, ma, mb, wo, g, wc):
    rows = x2d.shape[0]
    tm = ROW_TILE
    row = lambda i: (i, 0)
    const = lambda i: (0, 0)
    kvrow = lambda i: (0, i, 0)
    out_shape = (
        jax.ShapeDtypeStruct((rows, D_MODEL), F32), jax.ShapeDtypeStruct((rows, C_W), BF16),
        jax.ShapeDtypeStruct((C_KV, rows, HEAD_DIM), BF16), jax.ShapeDtypeStruct((rows // BLOCK, C_KVW, BLOCK), BF16),
        jax.ShapeDtypeStruct((rows, C_W), BF16),
    )
    out_specs = (
        pl.BlockSpec((tm, D_MODEL), row), pl.BlockSpec((tm, C_W), row),
        pl.BlockSpec((C_KV, tm, HEAD_DIM), kvrow), pl.BlockSpec((tm // BLOCK, C_KVW, BLOCK), lambda i: (i, 0, 0)),
        pl.BlockSpec((tm, C_W), row),
    )
    in_specs = [
        pl.BlockSpec((tm, D_MODEL), row), pl.BlockSpec((tm, A_W), row), pl.BlockSpec((tm, B_VW), row),
        pl.BlockSpec((A_W + B_VW, D_MODEL), const), pl.BlockSpec((1, D_MODEL), const),
        pl.BlockSpec((D_MODEL, C_IN), const),
    ]
    return pl.pallas_call(
        _mid_proj_kernel, grid=(rows // tm,), in_specs=in_specs, out_specs=out_specs,
        out_shape=out_shape, compiler_params=_params(("parallel",)), name="mid_proj",
    )(x2d, ma, mb, wo, g, wc)


def _win_attn_kernel(q_ref, kp_ref, k_ref, kn_ref, vp_ref, v_ref, vn_ref, bias_ref, sink_ref, sg_ref, o_ref,
                     kw_all, vt_all, s_buf, p_buf, *, blocks_per_seq):
    j = pl.program_id(1)
    nblk = q_ref.shape[0] // BLOCK
    gw = C_GROUP * HEAD_DIM
    for g in range(C_KV):
        kw_all[g, 0:BLOCK] = kp_ref[g]
        kw_all[g, BLOCK:(nblk + 1) * BLOCK] = k_ref[g]
        kw_all[g, (nblk + 1) * BLOCK:(nblk + 2) * BLOCK] = kn_ref[g]
    vt_all[0] = vp_ref[0]
    vt_all[1:nblk + 1] = v_ref[...]
    vt_all[nblk + 1] = vn_ref[0]

    def block_rows(i):
        return pl.ds(pl.multiple_of(i * BLOCK, BLOCK), BLOCK)

    def scores(i, g):
        qs = _stack_heads_on_rows(q_ref[block_rows(i), g * gw:(g + 1) * gw], C_GROUP)
        kw = kw_all[g, pl.ds(pl.multiple_of(i * BLOCK, BLOCK), 3 * BLOCK), :]
        s_buf[g] = lax.dot_general(kw, qs, (((1,), (1,)), ((), ())), preferred_element_type=F32)

    def softmax(i, g):
        block_in_seq = j * nblk + i
        prev_bias = jnp.where(block_in_seq == 0, 3, 0)
        next_bias = jnp.where(block_in_seq == blocks_per_seq - 1, 3, 2)
        s_t = jnp.concatenate([s_buf[g, 0:BLOCK] + bias_ref[prev_bias, g],
                               s_buf[g, BLOCK:2 * BLOCK] + bias_ref[1, g],
                               s_buf[g, 2 * BLOCK:3 * BLOCK] + bias_ref[next_bias, g]], axis=0)
        sink = sink_ref[g]
        mx = jnp.maximum(jnp.max(s_t, axis=0, keepdims=True), sink)
        p_buf[g] = jnp.exp2(s_t - mx).astype(BF16)
        return jnp.exp2(sink - mx)

    def finish(i, g, sink_term):
        heads = slice(g * HEAD_DIM, (g + 1) * HEAD_DIM)
        vw_t = jnp.concatenate([vt_all[i, heads, :], vt_all[i + 1, heads, :], vt_all[i + 2, heads, :]], axis=1)
        acc = jnp.dot(_with_ones_rows(vw_t), p_buf[g], preferred_element_type=F32)
        denom = acc[HEAD_DIM:HEAD_DIM + 1] + sink_term
        o = _unstack_transposed(acc[:HEAD_DIM] / denom, C_GROUP, BLOCK)
        gate = sg_ref[block_rows(i), g * gw:(g + 1) * gw].astype(F32)
        o_ref[block_rows(i), g * gw:(g + 1) * gw] = (o * gate).astype(BF16)

    scores(0, 0)
    scores(0, 1)

    def body(step, sink_term0):
        for u in range(WIN_UNROLL):
            i = step * WIN_UNROLL + u
            nxt = jnp.minimum(i + 1, nblk - 1)
            scores(nxt, 0)
            sink_term1 = softmax(i, 1)
            finish(i, 0, sink_term0)
            scores(nxt, 1)
            sink_term0 = softmax(nxt, 0)
            finish(i, 1, sink_term1)
        return sink_term0

    lax.fori_loop(0, nblk // WIN_UNROLL, body, softmax(0, 0))


def _win_attn(q, k, vt, bias, sink, sg, batch, seq):
    nb = seq // BLOCK
    sup = WIN_SUPER
    nblk = sup // BLOCK
    q3 = q.reshape(batch, seq, C_W)
    sg3 = sg.reshape(batch, seq, C_W)
    k4 = k.reshape(C_KV, batch, seq, HEAD_DIM)
    rowspec = pl.BlockSpec((None, sup, C_W), lambda b, j: (b, j, 0))
    before = lambda j: jnp.maximum(j * nblk - 1, 0)
    after = lambda j: jnp.minimum((j + 1) * nblk, nb - 1)
    halo_k = (C_KV, None, BLOCK, HEAD_DIM)
    halo_v = (1, C_KVW, BLOCK)
    out = pl.pallas_call(
        functools.partial(_win_attn_kernel, blocks_per_seq=nb), grid=(batch, seq // sup),
        in_specs=[rowspec,
                  pl.BlockSpec(halo_k, lambda b, j: (0, b, before(j), 0)),
                  pl.BlockSpec((C_KV, None, sup, HEAD_DIM), lambda b, j: (0, b, j, 0)),
                  pl.BlockSpec(halo_k, lambda b, j: (0, b, after(j), 0)),
                  pl.BlockSpec(halo_v, lambda b, j: (b * nb + before(j), 0, 0)),
                  pl.BlockSpec((nblk, C_KVW, BLOCK), lambda b, j: (b * (nb // nblk) + j, 0, 0)),
                  pl.BlockSpec(halo_v, lambda b, j: (b * nb + after(j), 0, 0)),
                  pl.BlockSpec((4, C_KV, BLOCK, C_GROUP * BLOCK), lambda b, j: (0, 0, 0, 0)),
                  pl.BlockSpec((C_KV, 1, C_GROUP * BLOCK), lambda b, j: (0, 0, 0)), rowspec],
        out_specs=rowspec, out_shape=jax.ShapeDtypeStruct((batch, seq, C_W), BF16),
        scratch_shapes=[pltpu.VMEM((C_KV, sup + 2 * BLOCK, HEAD_DIM), BF16),
                        pltpu.VMEM((nblk + 2, C_KVW, BLOCK), BF16),
                        pltpu.VMEM((C_KV, 3 * BLOCK, C_GROUP * BLOCK), F32),
                        pltpu.VMEM((C_KV, 3 * BLOCK, C_GROUP * BLOCK), BF16)],
        compiler_params=_params(("parallel", "parallel")), name="win_attn",
    )(q3, k4, k4, k4, vt, vt, vt, bias, sink, sg3)
    return out.reshape(batch * seq, C_W)


def _out_proj_kernel(x_hbm, m_hbm, wo_ref, g_ref, y_hbm, x_buf, m_buf, y_buf, in_sem, out_sem):
    nbuf, tm = x_buf.shape[0], x_buf.shape[1]
    steps = x_hbm.shape[0] // tm

    def tile_rows(step):
        start = step * tm
        return pl.ds(start if isinstance(step, int) else pl.multiple_of(start, tm), tm)

    def in_copies(step):
        slot = step % nbuf
        return (pltpu.make_async_copy(x_hbm.at[tile_rows(step)], x_buf.at[slot], in_sem.at[0, slot]),
                pltpu.make_async_copy(m_hbm.at[tile_rows(step)], m_buf.at[slot], in_sem.at[1, slot]))

    def out_copy(step):
        slot = step % 2
        return pltpu.make_async_copy(y_buf.at[slot], y_hbm.at[tile_rows(step)], out_sem.at[slot])

    for step in range(nbuf - 1):
        for copy in in_copies(step):
            copy.start()

    def body(step, carry):
        @pl.when(step + nbuf - 1 < steps)
        def _():
            for copy in in_copies(step + nbuf - 1):
                copy.start()

        for copy in in_copies(step):
            copy.wait()

        @pl.when(step >= 2)
        def _():
            out_copy(step - 2).wait()

        slot, out_slot = step % nbuf, step % 2
        part_rows = tm // PROJ_SPLIT
        for part in range(PROJ_SPLIT):
            rows = slice(part * part_rows, (part + 1) * part_rows)
            x2 = x_buf[slot, rows, :] + jnp.dot(m_buf[slot, rows, :], wo_ref[...], preferred_element_type=F32)
            y_buf[out_slot, rows, :] = _rmsnorm_rows(x2, g_ref[...])
        out_copy(step).start()
        return carry

    lax.fori_loop(0, steps, body, 0)
    out_copy(steps - 2).wait()
    out_copy(steps - 1).wait()


def _out_proj(x2d, m, wo, g):
    rows = x2d.shape[0]
    tm, nbuf = OUT_ROW_TILE, OUT_INPUT_BUFFERS
    assert rows % tm == 0 and rows // tm >= max(2, nbuf)
    hbm = pl.BlockSpec(memory_space=pl.ANY)
    vmem = pl.BlockSpec(memory_space=pltpu.VMEM)
    return pl.pallas_call(
        _out_proj_kernel, in_specs=[hbm, hbm, vmem, vmem], out_specs=hbm,
        out_shape=jax.ShapeDtypeStruct((rows, D_MODEL), F32),
        scratch_shapes=[pltpu.VMEM((nbuf, tm, D_MODEL), F32), pltpu.VMEM((nbuf, tm, C_W), BF16),
                        pltpu.VMEM((2, tm, D_MODEL), F32),
                        pltpu.SemaphoreType.DMA((2, nbuf)), pltpu.SemaphoreType.DMA((2,))],
        compiler_params=pltpu.CompilerParams(vmem_limit_bytes=VMEM_LIMIT), name="out_proj",
    )(x2d, m, wo, g)


def _trunk(x, prm):
    batch, seq, _ = x.shape
    x2d = x.reshape(batch * seq, D_MODEL)
    qa, ka, va, sga, qb, kb, vb, sgb = _inproj_ab(
        x2d, seq, prm["g0"], prm["w_in_ab"], prm["gain"], prm["ones"], prm["tabs"])
    ma = _dense_attn(qa, ka, va, sga, batch, seq)
    mb = _retention(prm["log_gamma"], qb, kb, vb, sgb, batch, seq)
    x1, qc, kc, vc, sgc = _mid_proj(x2d, ma, mb, prm["w_out_ab"], prm["g1"], prm["w_in_c"])
    mc = _win_attn(qc, kc, vc, prm["bias"], prm["sink"], sgc, batch, seq)
    y = _out_proj(x1, mc, prm["w_out_c"], prm["gf"])
    return y.reshape(batch, seq, D_MODEL)


def kernel(x_prompt, x_sample, norm_g, w_in_ab, qk_norm_a, ret_decay, w_out_ab, w_in_c, sink_c, w_out_c,
           rel_bias, final_norm):
    assert w_in_ab.shape[0] == 1 and w_in_c.shape[0] == 1, "two-layer trunk: one layer of each kind"
    max_seq = max(x_prompt.shape[1], x_sample.shape[1])

    w_ab = w_in_ab[0].astype(BF16)
    gain = jnp.stack([jnp.tile(qk_norm_a[0, 0], 2) * (HEAD_DIM ** -0.5 * LOG2E),
                      jnp.tile(qk_norm_a[0, 1], 2)]).astype(F32)
    ones = np.kron(np.eye(2 * LANES // HEAD_DIM), np.ones((HEAD_DIM, HEAD_DIM)))

    rel = np.arange(4 * BLOCK)[:, None] - BLOCK - np.arange(BLOCK)[None, :]
    bucket_onehot = jax.nn.one_hot(jnp.asarray(_t5_bucket(rel).reshape(-1)), REL_BUCKETS, dtype=F32, axis=0)
    bias = jnp.einsum("bh,bn->hn", rel_bias.astype(F32) * LOG2E, bucket_onehot,
                      precision=lax.Precision.HIGHEST)
    bias = jnp.where(jnp.asarray((np.abs(rel) <= WINDOW).reshape(1, -1)), bias, NEG)
    bias = bias.reshape(C_KV, C_GROUP, 4, BLOCK, BLOCK).transpose(2, 0, 3, 1, 4)
    bias = bias.reshape(4, C_KV, BLOCK, C_GROUP * BLOCK)
    sink = jnp.repeat(sink_c[0].astype(F32) * LOG2E, BLOCK).reshape(C_KV, 1, C_GROUP * BLOCK)

    prm = {
        "g0": norm_g[0].reshape(1, D_MODEL), "g1": norm_g[1].reshape(1, D_MODEL),
        "gf": final_norm.reshape(1, D_MODEL),
        "w_in_ab": w_ab, "w_out_ab": w_out_ab[0].astype(BF16),
        "w_in_c": w_in_c[0].astype(BF16), "w_out_c": w_out_c[0].astype(BF16),
        "gain": gain, "ones": jnp.asarray(ones, BF16),
        "tabs": _rope_tables(_axial_angles(max_seq)) + _rope_tables(_linear_angles(max_seq)),
        "log_gamma": -jnp.exp(ret_decay[0].astype(F32)),
        "bias": bias, "sink": sink,
    }
    return _trunk(x_prompt, prm), _trunk(x_sample, prm)
```

```python
import functools

import jax
import jax.numpy as jnp
import numpy as np
from jax import lax
from jax.experimental import pallas as pl
from jax.experimental.pallas import tpu as pltpu

F32 = jnp.float32
BF16 = jnp.bfloat16

D_MODEL = 1024
GRID_W = 64
EPS = 1e-6
ROPE_THETA = 10000.0
HEAD_DIM = 64
HALF = HEAD_DIM // 2
LANES = 128

A_HEADS, A_KV = 8, 2
A_GROUP = A_HEADS // A_KV
A_W, A_KVW = A_HEADS * HEAD_DIM, A_KV * HEAD_DIM
B_HEADS, B_VAL = 4, 128
B_QKW, B_VW = B_HEADS * HEAD_DIM, B_HEADS * B_VAL
C_HEADS, C_KV = 16, 2
C_GROUP = C_HEADS // C_KV
C_W, C_KVW = C_HEADS * HEAD_DIM, C_KV * HEAD_DIM
BLOCK = 128
WINDOW = 128
REL_BUCKETS, REL_MAX_DIST = 32, 128
CHUNK = 128
RET_UNROLL = 32
NEG = -1e30
LOG2E = float(np.log2(np.e))
ONES_ROWS = 16

AB_QA, AB_KA, AB_VA, AB_GA = 0, A_W, A_W + A_KVW, A_W + 2 * A_KVW
AB_QB = AB_GA + A_W
AB_KB, AB_VB, AB_GB = AB_QB + B_QKW, AB_QB + 2 * B_QKW, AB_QB + 2 * B_QKW + B_VW
AB_IN = AB_GB + B_VW
C_Q, C_K, C_V, C_G = 0, C_W, C_W + C_KVW, C_W + 2 * C_KVW
C_IN = C_G + C_W

PROJ_SPLIT = 2
ROW_TILE = 1024
OUT_ROW_TILE = 1024
OUT_INPUT_BUFFERS = 4
ATT_TQ = 256
ATT_TK_MAX = 1024
ATT_SEQ_CHUNKS = 4
ATT_BODY_KEYS = 16384
WIN_SUPER = 2048
WIN_UNROLL = 16
VMEM_LIMIT = 56 * 1024 * 1024


def _att_tk(seq):
    return min(ATT_TK_MAX, seq // ATT_SEQ_CHUNKS)


def _params(sem):
    return pltpu.CompilerParams(dimension_semantics=sem, vmem_limit_bytes=VMEM_LIMIT)


def _rope_tables(ang):
    c, s = np.repeat(np.cos(ang), 2, axis=-1), np.repeat(np.sin(ang), 2, axis=-1)
    s[:, 0::2] *= -1.0
    return jnp.asarray(np.tile(c, (1, 2)), F32), jnp.asarray(np.tile(s, (1, 2)), F32)


def _axial_angles(n):
    t = np.arange(n)
    quarter = HEAD_DIM // 4
    freqs = ROPE_THETA ** (-np.arange(quarter, dtype=np.float64) / quarter)
    row = (t // GRID_W).astype(np.float64)
    col = (t % GRID_W).astype(np.float64)
    return np.concatenate([row[:, None] * freqs, col[:, None] * freqs], axis=-1)


def _linear_angles(n):
    freqs = ROPE_THETA ** (-np.arange(HALF, dtype=np.float64) / HALF)
    return np.arange(n, dtype=np.float64)[:, None] * freqs


def _t5_bucket(rel):
    half = REL_BUCKETS // 2
    max_exact = half // 2
    ret = (rel > 0).astype(np.int32) * half
    dist = np.abs(rel)
    large = max_exact + (np.log(np.maximum(dist, 1) / max_exact) / np.log(REL_MAX_DIST / max_exact)
                         * (half - max_exact)).astype(np.int32)
    large = np.minimum(large, half - 1)
    return ret + np.where(dist < max_exact, dist, large)


def _rmsnorm_rows(x, gain):
    return x * lax.rsqrt(jnp.mean(x * x, axis=-1, keepdims=True) + EPS) * gain


def _silu(z):
    return z * jax.nn.sigmoid(z)


def _rotate_pairs(x, cos_t, sin_t):
    lane = lax.broadcasted_iota(jnp.int32, (1, LANES), 1)
    partner = jnp.where(lane % 2 == 0, pltpu.roll(x, LANES - 1, 1), pltpu.roll(x, 1, 1))
    return x * cos_t + partner * sin_t


def _transposed_chunks_store(ref, tile, row0=0):
    chunk = ref.shape[2]
    rows = tile.shape[0]
    tile_t = tile.T
    step = min(chunk, rows)
    for s in range(0, rows, step):
        c, off = divmod(row0 + s, chunk)
        ref[c, :, off:off + step] = tile_t[:, s:s + step].astype(ref.dtype)


def _stack_heads_on_rows(q, heads):
    return jnp.concatenate([q[:, h * HEAD_DIM:(h + 1) * HEAD_DIM] for h in range(heads)], axis=0)


def _unstack_transposed(o_t, heads, tokens):
    by_head = jnp.concatenate([o_t[:, h * tokens:(h + 1) * tokens] for h in range(heads)], axis=0)
    return by_head.T


def _with_ones_rows(v_t):
    return jnp.concatenate([v_t, jnp.ones((ONES_ROWS, v_t.shape[1]), v_t.dtype)], axis=0)


def _inproj_ab_kernel(x_ref, g_ref, w_ref, gain_ref, ones_ref, cosa_ref, sina_ref, cosb_ref, sinb_ref,
                      qa_ref, ka_ref, va_ref, sga_ref, qb_ref, kb_ref, vb_ref, sgb_ref):
    part_rows = x_ref.shape[0] // PROJ_SPLIT
    for part in range(PROJ_SPLIT):
        rows = slice(part * part_rows, (part + 1) * part_rows)
        xn = _rmsnorm_rows(x_ref[rows, :], g_ref[...]).astype(BF16)

        def proj(lo, hi):
            return jnp.dot(xn, w_ref[:, lo:hi], preferred_element_type=F32)

        cosa, sina = cosa_ref[rows, :], sina_ref[rows, :]
        cosb, sinb = cosb_ref[rows, :], sinb_ref[rows, :]

        def head_norm_rope(z, gain):
            width = z.shape[1]
            ss = jnp.dot((z * z).astype(BF16), ones_ref[0:width, 0:width], preferred_element_type=F32)
            z = z * lax.rsqrt(ss * (1.0 / HEAD_DIM) + EPS)
            return [_rotate_pairs(z[:, c * LANES:(c + 1) * LANES] * gain, cosa, sina)
                    for c in range(width // LANES)]

        zq = proj(AB_QA, AB_KA)
        for pair in range(A_W // (2 * LANES)):
            tiles = head_norm_rope(zq[:, pair * 2 * LANES:(pair + 1) * 2 * LANES], gain_ref[0:1, :])
            for c, tile in enumerate(tiles):
                lo = (2 * pair + c) * LANES
                qa_ref[rows, lo:lo + LANES] = tile.astype(BF16)
        zkv = proj(AB_KA, AB_GA)
        k_tile, = head_norm_rope(zkv[:, :LANES], gain_ref[1:2, :])
        ka_ref[0, rows, :] = k_tile[:, :HEAD_DIM].astype(BF16)
        ka_ref[1, rows, :] = k_tile[:, HEAD_DIM:].astype(BF16)
        _transposed_chunks_store(va_ref, zkv[:, LANES:], part * part_rows)
        sga_ref[rows, :] = _silu(proj(AB_GA, AB_QB)).astype(BF16)
        sgb_ref[rows, :] = _silu(proj(AB_GB, AB_IN)).astype(BF16)

        zqk = proj(AB_QB, AB_VB)
        for c in range(B_QKW // LANES):
            qb_ref[rows, c * LANES:(c + 1) * LANES] = _rotate_pairs(
                zqk[:, c * LANES:(c + 1) * LANES], cosb, sinb).astype(BF16)
            kt = zqk[:, B_QKW + c * LANES:B_QKW + (c + 1) * LANES] * (HEAD_DIM ** -0.5)
            kb_ref[rows, c * LANES:(c + 1) * LANES] = _rotate_pairs(kt, cosb, sinb).astype(BF16)
        vb_ref[rows, :] = proj(AB_VB, AB_GB).astype(BF16)


def _inproj_ab(x2d, seq, g, w, gain, ones, tabs):
    rows = x2d.shape[0]
    tm = ROW_TILE
    per_seq = seq // tm
    row = lambda i: (i, 0)
    const = lambda i: (0, 0)
    tab = lambda i: (i % per_seq, 0)
    kvrow = lambda i: (0, i, 0)
    tk = _att_tk(seq)
    vt_chunks = tm // tk
    out_shape = (
        jax.ShapeDtypeStruct((rows, A_W), BF16), jax.ShapeDtypeStruct((A_KV, rows, HEAD_DIM), BF16),
        jax.ShapeDtypeStruct((rows // tk, A_KVW, tk), BF16), jax.ShapeDtypeStruct((rows, A_W), BF16),
        jax.ShapeDtypeStruct((rows, B_QKW), BF16), jax.ShapeDtypeStruct((rows, B_QKW), BF16),
        jax.ShapeDtypeStruct((rows, B_VW), BF16), jax.ShapeDtypeStruct((rows, B_VW), BF16),
    )
    out_specs = (
        pl.BlockSpec((tm, A_W), row), pl.BlockSpec((A_KV, tm, HEAD_DIM), kvrow),
        pl.BlockSpec((vt_chunks, A_KVW, tk), lambda i: (i, 0, 0)), pl.BlockSpec((tm, A_W), row),
        pl.BlockSpec((tm, B_QKW), row), pl.BlockSpec((tm, B_QKW), row),
        pl.BlockSpec((tm, B_VW), row), pl.BlockSpec((tm, B_VW), row),
    )
    in_specs = [
        pl.BlockSpec((tm, D_MODEL), row), pl.BlockSpec((1, D_MODEL), const),
        pl.BlockSpec((D_MODEL, AB_IN), const), pl.BlockSpec((2, LANES), const),
        pl.BlockSpec((2 * LANES, 2 * LANES), const),
    ] + [pl.BlockSpec((tm, LANES), tab)] * 4
    return pl.pallas_call(
        _inproj_ab_kernel, grid=(rows // tm,), in_specs=in_specs, out_specs=out_specs,
        out_shape=out_shape, compiler_params=_params(("parallel",)), name="inproj_ab",
    )(x2d, g, w, gain, ones, *tabs)


def _dense_attn_kernel(q_ref, k_ref, vt_ref, sg_ref, o_ref, s_buf, p_buf, *, seq):
    tq, tk = ATT_TQ, _att_tk(seq)
    rows = A_GROUP * tq
    n = seq // tk
    blocks = seq // tq
    assert n % 2 == 0 and n >= 2

    def query_rows(i):
        return pl.ds(pl.multiple_of(i * tq, tq), tq)

    def stacked_q(i):
        return _stack_heads_on_rows(q_ref[query_rows(i), :], A_GROUP)

    def scores(qs, c):
        k = k_ref[c * tk:(c + 1) * tk, :]
        s_t = lax.dot_general(k, qs, (((1,), (1,)), ((), ())), preferred_element_type=F32)
        s_buf[c % 2] = s_t
        return jnp.max(s_t, axis=0, keepdims=True)

    def softmax(m, c, chunk_max):
        m_new = jnp.maximum(m, chunk_max)
        p_buf[c % 2] = jnp.exp2(s_buf[c % 2] - m_new).astype(BF16)
        return m_new, jnp.exp2(m - m_new)

    fresh_m = jnp.full((1, rows), NEG, F32)
    first_q = stacked_q(0)
    first_max = [scores(first_q, 0), scores(first_q, 1)]

    def block(i, m, alpha, max0, max1):
        chunk_max = [max0, max1]
        qs = stacked_q(i)
        qs_next = stacked_q(jnp.minimum(i + 1, blocks - 1))
        acc = jnp.zeros((HEAD_DIM + ONES_ROWS, rows), F32)
        for t in range(n):
            if t + 2 < n:
                scored = scores(qs, t + 2)
            else:
                scored = scores(qs_next, t + 2 - n)
            if t + 1 < n:
                m, alpha_next = softmax(m, t + 1, chunk_max[(t + 1) % 2])
            else:
                m, alpha_next = softmax(fresh_m, 0, chunk_max[0])
            chunk_max[t % 2] = scored
            pv = jnp.dot(_with_ones_rows(vt_ref[t]), p_buf[t % 2], preferred_element_type=F32)
            acc = alpha * acc + pv
            alpha = alpha_next
        o_t = acc[:HEAD_DIM] / acc[HEAD_DIM:HEAD_DIM + 1]
        o = _unstack_transposed(o_t, A_GROUP, tq)
        o_ref[query_rows(i), :] = (o * sg_ref[query_rows(i), :].astype(F32)).astype(BF16)
        return m, alpha, chunk_max[0], chunk_max[1]

    per_body = min(blocks, max(1, ATT_BODY_KEYS // seq))

    def body(step, carry):
        for u in range(per_body):
            carry = block(step * per_body + u, *carry)
        return carry

    lax.fori_loop(0, blocks // per_body, body, softmax(fresh_m, 0, first_max[0]) + tuple(first_max))


def _dense_attn(qa, ka, vat, sga, batch, seq):
    gw = A_GROUP * HEAD_DIM
    q3 = qa.reshape(batch, seq, A_W)
    sg3 = sga.reshape(batch, seq, A_W)
    k4 = ka.reshape(A_KV, batch, seq, HEAD_DIM)
    tk = _att_tk(seq)
    vt4 = vat.reshape(batch, seq // tk, A_KVW, tk)
    qspec = pl.BlockSpec((None, seq, gw), lambda b, g: (b, 0, g))
    kspec = pl.BlockSpec((None, None, seq, HEAD_DIM), lambda b, g: (g, b, 0, 0))
    vtspec = pl.BlockSpec((None, seq // tk, HEAD_DIM, tk), lambda b, g: (b, 0, g, 0))
    out = pl.pallas_call(
        functools.partial(_dense_attn_kernel, seq=seq), grid=(batch, A_KV),
        in_specs=[qspec, kspec, vtspec, qspec], out_specs=qspec,
        out_shape=jax.ShapeDtypeStruct((batch, seq, A_W), BF16),
        scratch_shapes=[pltpu.VMEM((2, tk, A_GROUP * ATT_TQ), F32),
                        pltpu.VMEM((2, tk, A_GROUP * ATT_TQ), BF16)],
        compiler_params=_params(("parallel", "parallel")), name="dense_attn",
    )(q3, k4, vt4, sg3)
    return out.reshape(batch * seq, A_W)


def _retention_kernel(lg_ref, q_ref, k_ref, v_ref, sg_ref, o_ref, st_ref, *, seq):
    c = CHUNK
    nc = seq // c
    pv = 2 * B_VAL
    pair = pl.program_id(1)
    lgf = [lg_ref[0, 2 * pair + hh] for hh in range(2)]
    lgb = [lg_ref[1, 2 * pair + hh] for hh in range(2)]
    lane = lax.broadcasted_iota(jnp.int32, (1, LANES), 1)
    first_qk_head = lane < HEAD_DIM
    lane_lgf = jnp.where(first_qk_head, lgf[0], lgf[1])
    lane_lgb = jnp.where(first_qk_head, lgb[0], lgb[1])
    t = lax.broadcasted_iota(jnp.int32, (c, 1), 0).astype(F32)
    kdec_f = jnp.exp((c - 1.0 - t) * lane_lgf)
    kdec_b = jnp.exp(t * lane_lgb)
    qdec_f = jnp.exp((t + 1.0) * lane_lgf)
    qdec_b = jnp.exp((c - t) * lane_lgb)
    row_first = lax.broadcasted_iota(jnp.int32, (LANES, 1), 0) < HEAD_DIM
    first_v_head = lax.broadcasted_iota(jnp.int32, (1, pv), 1) < B_VAL
    same_head = row_first == first_v_head
    cdec_f = jnp.exp(c * jnp.where(row_first, lgf[0], lgf[1]))
    cdec_b = jnp.exp(c * jnp.where(row_first, lgb[0], lgb[1]))
    diff = (lax.broadcasted_iota(jnp.int32, (c, c), 0) - lax.broadcasted_iota(jnp.int32, (c, c), 1)).astype(F32)
    intra_decay = jnp.concatenate(
        [jnp.where(diff >= 0, jnp.exp(lgf[hh] * jnp.maximum(diff, 0.0)), jnp.exp(lgb[hh] * jnp.maximum(-diff, 0.0)))
         for hh in range(2)], axis=1)

    def chunk_rows(n):
        return pl.ds(pl.multiple_of(n * c, c), c)

    def kv_body(n, carry):
        rows = chunk_rows(n)
        k = k_ref[rows, :].astype(F32)
        v = v_ref[rows, :]
        dn = (((0,), (0,)), ((), ()))
        kvf = lax.dot_general((k * kdec_f).astype(BF16), v, dn, preferred_element_type=F32)
        kvb = lax.dot_general((k * kdec_b).astype(BF16), v, dn, preferred_element_type=F32)
        st_ref[n, 0:LANES, :] = jnp.where(same_head, kvf, 0.0)
        st_ref[n, LANES:2 * LANES, :] = jnp.where(same_head, kvb, 0.0)
        return carry

    lax.fori_loop(0, nc, kv_body, 0, unroll=RET_UNROLL)

    def scan_fwd(n, state):
        kv = st_ref[n, 0:LANES, :]
        st_ref[n, 0:LANES, :] = state
        return state * cdec_f + kv

    lax.fori_loop(0, nc, scan_fwd, jnp.zeros((LANES, pv), F32))

    def scan_bwd(i, state):
        n = nc - 1 - i
        kv = st_ref[n, LANES:2 * LANES, :]
        st_ref[n, LANES:2 * LANES, :] = state
        return state * cdec_b + kv

    lax.fori_loop(0, nc, scan_bwd, jnp.zeros((LANES, pv), F32))

    def out_body(n, carry):
        rows = chunk_rows(n)
        q = q_ref[rows, :].astype(F32)
        k = k_ref[rows, :]
        v = v_ref[rows, :]
        zero = jnp.zeros_like(k)
        k_by_head = jnp.concatenate([jnp.where(first_qk_head, k, zero), jnp.where(first_qk_head, zero, k)], axis=0)
        s = lax.dot_general(q.astype(BF16), k_by_head, (((1,), (1,)), ((), ())),
                            preferred_element_type=F32)
        zero_v = jnp.zeros_like(v)
        v_by_head = jnp.concatenate([jnp.where(first_v_head, v, zero_v), jnp.where(first_v_head, zero_v, v)], axis=0)
        lhs = jnp.concatenate([s * intra_decay, q * qdec_f, q * qdec_b], axis=1).astype(BF16)
        rhs = jnp.concatenate([v_by_head, st_ref[n].astype(BF16)], axis=0)
        o_pair = jnp.dot(lhs, rhs, preferred_element_type=F32)
        for hh in range(2):
            o = o_pair[:, hh * B_VAL:(hh + 1) * B_VAL]
            o = o * lax.rsqrt(jnp.mean(o * o, axis=-1, keepdims=True) + EPS)
            o = o * sg_ref[rows, hh * B_VAL:(hh + 1) * B_VAL].astype(F32)
            o_ref[rows, hh * B_VAL:(hh + 1) * B_VAL] = o.astype(BF16)
        return carry

    lax.fori_loop(0, nc, out_body, 0, unroll=RET_UNROLL)


def _retention(log_gamma, qb, kb, vb, sgb, batch, seq):
    pairs = B_HEADS // 2
    qk_spec = pl.BlockSpec((None, seq, LANES), lambda b, j: (b, 0, j))
    v_spec = pl.BlockSpec((None, seq, 2 * B_VAL), lambda b, j: (b, 0, j))
    out = pl.pallas_call(
        functools.partial(_retention_kernel, seq=seq), grid=(batch, pairs),
        in_specs=[pl.BlockSpec(memory_space=pltpu.SMEM), qk_spec, qk_spec, v_spec, v_spec],
        out_specs=v_spec, out_shape=jax.ShapeDtypeStruct((batch, seq, B_VW), BF16),
        scratch_shapes=[pltpu.VMEM((seq // CHUNK, 2 * LANES, 2 * B_VAL), F32)],
        compiler_params=_params(("parallel", "parallel")), name="retention",
    )(log_gamma, qb.reshape(batch, seq, B_QKW), kb.reshape(batch, seq, B_QKW),
      vb.reshape(batch, seq, B_VW), sgb.reshape(batch, seq, B_VW))
    return out.reshape(batch * seq, B_VW)


def _mid_proj_kernel(x_ref, ma_ref, mb_ref, wo_ref, g_ref, wc_ref, x1_ref, q_ref, k_ref, v_ref, sg_ref):
    part_rows = x_ref.shape[0] // PROJ_SPLIT
    for part in range(PROJ_SPLIT):
        rows = slice(part * part_rows, (part + 1) * part_rows)
        y = jnp.dot(ma_ref[rows, :], wo_ref[0:A_W, :], preferred_element_type=F32)
        y += jnp.dot(mb_ref[rows, :], wo_ref[A_W:A_W + B_VW, :], preferred_element_type=F32)
        x1 = x_ref[rows, :] + y
        x1_ref[rows, :] = x1
        xn = _rmsnorm_rows(x1, g_ref[...]).astype(BF16)

        def proj(lo, hi):
            return jnp.dot(xn, wc_ref[:, lo:hi], preferred_element_type=F32)

        sg_ref[rows, :] = _silu(proj(C_G, C_IN)).astype(BF16)
        zkv = proj(C_K, C_G)
        k_ref[0, rows, :] = zkv[:, :HEAD_DIM].astype(BF16)
        k_ref[1, rows, :] = zkv[:, HEAD_DIM:LANES].astype(BF16)
        _transposed_chunks_store(v_ref, zkv[:, LANES:], part * part_rows)
        q_ref[rows, :] = (proj(C_Q, C_K) * (HEAD_DIM ** -0.5 * LOG2E)).astype(BF16)


def _mid_proj(x2d, ma, mb, wo, g, wc):
    rows = x2d.shape[0]
    tm = ROW_TILE
    row = lambda i: (i, 0)
    const = lambda i: (0, 0)
    kvrow = lambda i: (0, i, 0)
    out_shape = (
        jax.ShapeDtypeStruct((rows, D_MODEL), F32), jax.ShapeDtypeStruct((rows, C_W), BF16),
        jax.ShapeDtypeStruct((C_KV, rows, HEAD_DIM), BF16), jax.ShapeDtypeStruct((rows // BLOCK, C_KVW, BLOCK), BF16),
        jax.ShapeDtypeStruct((rows, C_W), BF16),
    )
    out_specs = (
        pl.BlockSpec((tm, D_MODEL), row), pl.BlockSpec((tm, C_W), row),
        pl.BlockSpec((C_KV, tm, HEAD_DIM), kvrow), pl.BlockSpec((tm // BLOCK, C_KVW, BLOCK), lambda i: (i, 0, 0)),
        pl.BlockSpec((tm, C_W), row),
    )
    in_specs = [
        pl.BlockSpec((tm, D_MODEL), row), pl.BlockSpec((tm, A_W), row), pl.BlockSpec((tm, B_VW), row),
        pl.BlockSpec((A_W + B_VW, D_MODEL), const), pl.BlockSpec((1, D_MODEL), const),
        pl.BlockSpec((D_MODEL, C_IN), const),
    ]
    return pl.pallas_call(
        _mid_proj_kernel, grid=(rows // tm,), in_specs=in_specs, out_specs=out_specs,
        out_shape=out_shape, compiler_params=_params(("parallel",)), name="mid_proj",
    )(x2d, ma, mb, wo, g, wc)


def _win_attn_kernel(q_ref, kp_ref, k_ref, kn_ref, vp_ref, v_ref, vn_ref, bias_ref, sink_ref, sg_ref, o_ref,
                     kw_all, vt_all, s_buf, p_buf, *, blocks_per_seq):
    j = pl.program_id(1)
    nblk = q_ref.shape[0] // BLOCK
    gw = C_GROUP * HEAD_DIM
    for g in range(C_KV):
        kw_all[g, 0:BLOCK] = kp_ref[g]
        kw_all[g, BLOCK:(nblk + 1) * BLOCK] = k_ref[g]
        kw_all[g, (nblk + 1) * BLOCK:(nblk + 2) * BLOCK] = kn_ref[g]
    vt_all[0] = vp_ref[0]
    vt_all[1:nblk + 1] = v_ref[...]
    vt_all[nblk + 1] = vn_ref[0]

    def block_rows(i):
        return pl.ds(pl.multiple_of(i * BLOCK, BLOCK), BLOCK)

    def scores(i, g):
        qs = _stack_heads_on_rows(q_ref[block_rows(i), g * gw:(g + 1) * gw], C_GROUP)
        kw = kw_all[g, pl.ds(pl.multiple_of(i * BLOCK, BLOCK), 3 * BLOCK), :]
        s_buf[g] = lax.dot_general(kw, qs, (((1,), (1,)), ((), ())), preferred_element_type=F32)

    def softmax(i, g):
        block_in_seq = j * nblk + i
        prev_bias = jnp.where(block_in_seq == 0, 3, 0)
        next_bias = jnp.where(block_in_seq == blocks_per_seq - 1, 3, 2)
        s_t = jnp.concatenate([s_buf[g, 0:BLOCK] + bias_ref[prev_bias, g],
                               s_buf[g, BLOCK:2 * BLOCK] + bias_ref[1, g],
                               s_buf[g, 2 * BLOCK:3 * BLOCK] + bias_ref[next_bias, g]], axis=0)
        sink = sink_ref[g]
        mx = jnp.maximum(jnp.max(s_t, axis=0, keepdims=True), sink)
        p_buf[g] = jnp.exp2(s_t - mx).astype(BF16)
        return jnp.exp2(sink - mx)

    def finish(i, g, sink_term):
        heads = slice(g * HEAD_DIM, (g + 1) * HEAD_DIM)
        vw_t = jnp.concatenate([vt_all[i, heads, :], vt_all[i + 1, heads, :], vt_all[i + 2, heads, :]], axis=1)
        acc = jnp.dot(_with_ones_rows(vw_t), p_buf[g], preferred_element_type=F32)
        denom = acc[HEAD_DIM:HEAD_DIM + 1] + sink_term
        o = _unstack_transposed(acc[:HEAD_DIM] / denom, C_GROUP, BLOCK)
        gate = sg_ref[block_rows(i), g * gw:(g + 1) * gw].astype(F32)
        o_ref[block_rows(i), g * gw:(g + 1) * gw] = (o * gate).astype(BF16)

    scores(0, 0)
    scores(0, 1)

    def body(step, sink_term0):
        for u in range(WIN_UNROLL):
            i = step * WIN_UNROLL + u
            nxt = jnp.minimum(i + 1, nblk - 1)
            scores(nxt, 0)
            sink_term1 = softmax(i, 1)
            finish(i, 0, sink_term0)
            scores(nxt, 1)
            sink_term0 = softmax(nxt, 0)
            finish(i, 1, sink_term1)
        return sink_term0

    lax.fori_loop(0, nblk // WIN_UNROLL, body, softmax(0, 0))


def _win_attn(q, k, vt, bias, sink, sg, batch, seq):
    nb = seq // BLOCK
    sup = WIN_SUPER
    nblk = sup // BLOCK
    q3 = q.reshape(batch, seq, C_W)
    sg3 = sg.reshape(batch, seq, C_W)
    k4 = k.reshape(C_KV, batch, seq, HEAD_DIM)
    rowspec = pl.BlockSpec((None, sup, C_W), lambda b, j: (b, j, 0))
    before = lambda j: jnp.maximum(j * nblk - 1, 0)
    after = lambda j: jnp.minimum((j + 1) * nblk, nb - 1)
    halo_k = (C_KV, None, BLOCK, HEAD_DIM)
    halo_v = (1, C_KVW, BLOCK)
    out = pl.pallas_call(
        functools.partial(_win_attn_kernel, blocks_per_seq=nb), grid=(batch, seq // sup),
        in_specs=[rowspec,
                  pl.BlockSpec(halo_k, lambda b, j: (0, b, before(j), 0)),
                  pl.BlockSpec((C_KV, None, sup, HEAD_DIM), lambda b, j: (0, b, j, 0)),
                  pl.BlockSpec(halo_k, lambda b, j: (0, b, after(j), 0)),
                  pl.BlockSpec(halo_v, lambda b, j: (b * nb + before(j), 0, 0)),
                  pl.BlockSpec((nblk, C_KVW, BLOCK), lambda b, j: (b * (nb // nblk) + j, 0, 0)),
                  pl.BlockSpec(halo_v, lambda b, j: (b * nb + after(j), 0, 0)),
                  pl.BlockSpec((4, C_KV, BLOCK, C_GROUP * BLOCK), lambda b, j: (0, 0, 0, 0)),
                  pl.BlockSpec((C_KV, 1, C_GROUP * BLOCK), lambda b, j: (0, 0, 0)), rowspec],
        out_specs=rowspec, out_shape=jax.ShapeDtypeStruct((batch, seq, C_W), BF16),
        scratch_shapes=[pltpu.VMEM((C_KV, sup + 2 * BLOCK, HEAD_DIM), BF16),
                        pltpu.VMEM((nblk + 2, C_KVW, BLOCK), BF16),
                        pltpu.VMEM((C_KV, 3 * BLOCK, C_GROUP * BLOCK), F32),
                        pltpu.VMEM((C_KV, 3 * BLOCK, C_GROUP * BLOCK), BF16)],
        compiler_params=_params(("parallel", "parallel")), name="win_attn",
    )(q3, k4, k4, k4, vt, vt, vt, bias, sink, sg3)
    return out.reshape(batch * seq, C_W)


def _out_proj_kernel(x_hbm, m_hbm, wo_ref, g_ref, y_hbm, x_buf, m_buf, y_buf, in_sem, out_sem):
    nbuf, tm = x_buf.shape[0], x_buf.shape[1]
    steps = x_hbm.shape[0] // tm

    def tile_rows(step):
        start = step * tm
        return pl.ds(start if isinstance(step, int) else pl.multiple_of(start, tm), tm)

    def in_copies(step):
        slot = step % nbuf
        return (pltpu.make_async_copy(x_hbm.at[tile_rows(step)], x_buf.at[slot], in_sem.at[0, slot]),
                pltpu.make_async_copy(m_hbm.at[tile_rows(step)], m_buf.at[slot], in_sem.at[1, slot]))

    def out_copy(step):
        slot = step % 2
        return pltpu.make_async_copy(y_buf.at[slot], y_hbm.at[tile_rows(step)], out_sem.at[slot])

    for step in range(nbuf - 1):
        for copy in in_copies(step):
            copy.start()

    def body(step, carry):
        @pl.when(step + nbuf - 1 < steps)
        def _():
            for copy in in_copies(step + nbuf - 1):
                copy.start()

        for copy in in_copies(step):
            copy.wait()

        @pl.when(step >= 2)
        def _():
            out_copy(step - 2).wait()

        slot, out_slot = step % nbuf, step % 2
        part_rows = tm // PROJ_SPLIT
        for part in range(PROJ_SPLIT):
            rows = slice(part * part_rows, (part + 1) * part_rows)
            x2 = x_buf[slot, rows, :] + jnp.dot(m_buf[slot, rows, :], wo_ref[...], preferred_element_type=F32)
            y_buf[out_slot, rows, :] = _rmsnorm_rows(x2, g_ref[...])
        out_copy(step).start()
        return carry

    lax.fori_loop(0, steps, body, 0)
    out_copy(steps - 2).wait()
    out_copy(steps - 1).wait()


def _out_proj(x2d, m, wo, g):
    rows = x2d.shape[0]
    tm, nbuf = OUT_ROW_TILE, OUT_INPUT_BUFFERS
    assert rows % tm == 0 and rows // tm >= max(2, nbuf)
    hbm = pl.BlockSpec(memory_space=pl.ANY)
    vmem = pl.BlockSpec(memory_space=pltpu.VMEM)
    return pl.pallas_call(
        _out_proj_kernel, in_specs=[hbm, hbm, vmem, vmem], out_specs=hbm,
        out_shape=jax.ShapeDtypeStruct((rows, D_MODEL), F32),
        scratch_shapes=[pltpu.VMEM((nbuf, tm, D_MODEL), F32), pltpu.VMEM((nbuf, tm, C_W), BF16),
                        pltpu.VMEM((2, tm, D_MODEL), F32),
                        pltpu.SemaphoreType.DMA((2, nbuf)), pltpu.SemaphoreType.DMA((2,))],
        compiler_params=pltpu.CompilerParams(vmem_limit_bytes=VMEM_LIMIT), name="out_proj",
    )(x2d, m, wo, g)


def _trunk(x, prm):
    batch, seq, _ = x.shape
    x2d = x.reshape(batch * seq, D_MODEL)
    qa, ka, va, sga, qb, kb, vb, sgb = _inproj_ab(
        x2d, seq, prm["g0"], prm["w_in_ab"], prm["gain"], prm["ones"], prm["tabs"])
    ma = _dense_attn(qa, ka, va, sga, batch, seq)
    mb = _retention(prm["log_gamma"], qb, kb, vb, sgb, batch, seq)
    x1, qc, kc, vc, sgc = _mid_proj(x2d, ma, mb, prm["w_out_ab"], prm["g1"], prm["w_in_c"])
    mc = _win_attn(qc, kc, vc, prm["bias"], prm["sink"], sgc, batch, seq)
    y = _out_proj(x1, mc, prm["w_out_c"], prm["gf"])
    return y.reshape(batch, seq, D_MODEL)


def kernel(x_prompt, x_sample, norm_g, w_in_ab, qk_norm_a, ret_decay, w_out_ab, w_in_c, sink_c, w_out_c,
           rel_bias, final_norm):
    assert w_in_ab.shape[0] == 1 and w_in_c.shape[0] == 1, "two-layer trunk: one layer of each kind"
    max_seq = max(x_prompt.shape[1], x_sample.shape[1])

    w_ab = w_in_ab[0].astype(BF16)
    gain = jnp.stack([jnp.tile(qk_norm_a[0, 0], 2) * (HEAD_DIM ** -0.5 * LOG2E),
                      jnp.tile(qk_norm_a[0, 1], 2)]).astype(F32)
    ones = np.kron(np.eye(2 * LANES // HEAD_DIM), np.ones((HEAD_DIM, HEAD_DIM)))

    rel = np.arange(4 * BLOCK)[:, None] - BLOCK - np.arange(BLOCK)[None, :]
    bucket_onehot = jax.nn.one_hot(jnp.asarray(_t5_bucket(rel).reshape(-1)), REL_BUCKETS, dtype=F32, axis=0)
    bias = jnp.einsum("bh,bn->hn", rel_bias.astype(F32) * LOG2E, bucket_onehot,
                      precision=lax.Precision.HIGHEST)
    bias = jnp.where(jnp.asarray((np.abs(rel) <= WINDOW).reshape(1, -1)), bias, NEG)
    bias = bias.reshape(C_KV, C_GROUP, 4, BLOCK, BLOCK).transpose(2, 0, 3, 1, 4)
    bias = bias.reshape(4, C_KV, BLOCK, C_GROUP * BLOCK)
    sink = jnp.repeat(sink_c[0].astype(F32) * LOG2E, BLOCK).reshape(C_KV, 1, C_GROUP * BLOCK)

    prm = {
        "g0": norm_g[0].reshape(1, D_MODEL), "g1": norm_g[1].reshape(1, D_MODEL),
        "gf": final_norm.reshape(1, D_MODEL),
        "w_in_ab": w_ab, "w_out_ab": w_out_ab[0].astype(BF16),
        "w_in_c": w_in_c[0].astype(BF16), "w_out_c": w_out_c[0].astype(BF16),
        "gain": gain, "ones": jnp.asarray(ones, BF16),
        "tabs": _rope_tables(_axial_angles(max_seq)) + _rope_tables(_linear_angles(max_seq)),
        "log_gamma": -jnp.exp(ret_decay[0].astype(F32)),
        "bias": bias, "sink": sink,
    }
    return _trunk(x_prompt, prm), _trunk(x_sample, prm)
```
